```python
import jax, jax.numpy as jnp
from jax import lax
import numpy as np


D_MODEL = 4096
BATCH = 1
SEQ = 8192
DEPTH = 1
DEC_BATCH = 128
DEC_SEQ = 4
PAST_LEN = 2048
PAGE_SIZE = 128

MIX_WIDTH = D_MODEL
RET_HEADS = 8
RET_DK = MIX_WIDTH // 2 // RET_HEADS
RET_DV = RET_DK
RET_CHUNK = 128
NSA_HEADS = 16
NSA_HD = MIX_WIDTH // 2 // NSA_HEADS
NSA_KV = 4
NSA_REP = NSA_HEADS // NSA_KV
L_CMP = 32
STRIDE_CMP = 16
CMP_HID = 256
L_SEL = 64
N_SELECT = 16
WINDOW = 512
Q_BLOCK = 128
D_FF = 11008
CONV_W = 3
N_KV_ROWS = 4
P_SIZES = [RET_HEADS * RET_DK, RET_HEADS * RET_DK, RET_HEADS * RET_DV, RET_HEADS * RET_DV,
           NSA_HEADS * NSA_HD, 6 * NSA_KV * NSA_HD, NSA_HEADS * 3]
P_IN = sum(P_SIZES)
EPS = 1e-6
NEG = -1e30
FORCE = 1e4

kernel_name = "hymba_retnet_nsa_convffn_step"


def rmsnorm(x, g):
    xf = x.astype(jnp.float32)
    y = xf * lax.rsqrt(jnp.mean(xf * xf, axis=-1, keepdims=True) + EPS)
    return (y * g.astype(jnp.float32)).astype(x.dtype)


def masked_softmax(s, mask, axis=-1):
    p = jax.nn.softmax(jnp.where(mask, s, NEG), axis=axis)
    return jnp.where(mask, p, 0.0)


def alibi_slopes():
    h = jnp.arange(1, NSA_HEADS + 1, dtype=jnp.float32)
    return jnp.exp2(-8.0 * h / NSA_HEADS).reshape(NSA_KV, NSA_REP)


def retention_log_decay():
    return jnp.log(1.0 - jnp.exp2(-5.0 - jnp.arange(RET_HEADS, dtype=jnp.float32)))


def block_pre(x, c, w_ada, b_ada, g_pre_mix, w_in):
    mod = jax.nn.silu(c) @ w_ada + b_ada
    mods = jnp.split(mod[:, None, :], 6, axis=-1)
    h = rmsnorm(x, g_pre_mix) * (1.0 + mods[1]) + mods[0]
    p = h @ w_in
    B, T = p.shape[:2]
    parts = jnp.split(p, np.cumsum(P_SIZES)[:-1].tolist(), axis=-1)
    rq = parts[0].reshape(B, T, RET_HEADS, RET_DK)
    rk = parts[1].reshape(B, T, RET_HEADS, RET_DK)
    rv = parts[2].reshape(B, T, RET_HEADS, RET_DV)
    rg = parts[3]
    nq = parts[4].reshape(B, T, NSA_KV, NSA_REP, NSA_HD)
    kv = parts[5].reshape(B, T, 6, NSA_KV, NSA_HD)
    ng = parts[6].reshape(B, T, NSA_KV, NSA_REP, 3)
    return mods, (rq, rk, rv, rg, nq, kv, ng)


def retention_chunk(state, q, k, v, log_g):
    C = q.shape[1]
    q = q.astype(jnp.float32)
    k = k.astype(jnp.float32) * RET_DK ** -0.5
    v = v.astype(jnp.float32)
    i = jnp.arange(C, dtype=jnp.float32)
    diff = i[:, None] - i[None, :]
    decay = jnp.where(diff >= 0, jnp.exp(log_g[:, None, None] * jnp.maximum(diff, 0.0)), 0.0)
    attn = jnp.einsum('bihd,bjhd->bhij', q, k) * decay
    o = jnp.einsum('bhij,bjhe->bihe', attn, v)
    o = o + jnp.einsum('bihd,bhde->bihe', q, state) * jnp.exp(log_g[None, :] * (i[:, None] + 1.0))[None, :, :, None]
    k_dec = k * jnp.exp(log_g[None, :] * (C - 1.0 - i[:, None]))[None, :, :, None]
    new_state = jnp.exp(log_g * C)[None, :, None, None] * state + jnp.einsum('bjhd,bjhe->bhde', k_dec, v)
    return new_state, o


def retention_prompt(q, k, v, log_g):
    B, T = q.shape[:2]
    nc = T // RET_CHUNK

    def chunks(a):
        return a.reshape(B, nc, RET_CHUNK, *a.shape[2:]).swapaxes(0, 1)

    state0 = jnp.zeros((B, RET_HEADS, RET_DK, RET_DV), jnp.float32)
    state, o = lax.scan(lambda s, xs: retention_chunk(s, xs[0], xs[1], xs[2], log_g),
                        state0, (chunks(q), chunks(k), chunks(v)))
    return o.swapaxes(0, 1).reshape(B, T, RET_HEADS, RET_DV), state


def retention_out(o, g, gn, dtype):
    mu = jnp.mean(o, axis=-1, keepdims=True)
    var = jnp.mean(jnp.square(o - mu), axis=-1, keepdims=True)
    on = ((o - mu) * lax.rsqrt(var + EPS)).reshape(o.shape[0], o.shape[1], -1)
    return (on * gn.astype(jnp.float32) * jax.nn.silu(g.astype(jnp.float32))).astype(dtype)


def compress(rows, pe, w1, w2):
    h = jax.nn.gelu(jnp.einsum('bnlgd,ldh->bngh', rows + pe[:, None, :], w1))
    return jnp.einsum('bngh,hd->bngd', h, w2)


def nsa_keys(kv4, cmp_pe, cmp_w1, cmp_w2):
    B, T = kv4.shape[:2]
    n_cmp = (T - L_CMP) // STRIDE_CMP + 1
    idx = jnp.arange(n_cmp)[:, None] * STRIDE_CMP + jnp.arange(L_CMP)[None, :]
    kc = compress(kv4[:, :, 0][:, idx], cmp_pe[0], cmp_w1[0], cmp_w2[0])
    vc = compress(kv4[:, :, 1][:, idx], cmp_pe[1], cmp_w1[1], cmp_w2[1])
    c_end = jnp.arange(n_cmp) * STRIDE_CMP + L_CMP - 1
    n_sel = -(-T // L_SEL)
    slc = jnp.pad(kv4[:, :, 2:4], ((0, 0), (0, n_sel * L_SEL - T), (0, 0), (0, 0), (0, 0)))
    slc = slc.reshape(B, n_sel, L_SEL, 2, NSA_KV, NSA_HD).transpose(3, 0, 4, 1, 2, 5)
    return kc, vc, c_end, slc[0], slc[1]


def nsa_core(q, q_pos, kc, vc, c_end, kb, vb, kw, vw, w_pos, gates, slopes):
    B = q.shape[0]
    scale = NSA_HD ** -0.5
    sl = slopes[:, :, None, None]
    dist_c = q_pos[:, None] - c_end[None, :]
    s_c = jnp.einsum('bqgrd,bngd->bgrqn', q, kc).astype(jnp.float32) * scale - sl * dist_c.astype(jnp.float32)
    p_c = masked_softmax(s_c, dist_c >= 0)
    o_c = jnp.einsum('bgrqn,bngd->bqgrd', p_c.astype(vc.dtype), vc)
    n_sel = kb.shape[2]
    blk = jnp.arange(n_sel)
    c_start = c_end - L_CMP + 1
    overlap = ((c_start[:, None] < (blk[None, :] + 1) * L_SEL) & (c_end[:, None] >= blk[None, :] * L_SEL)).astype(jnp.float32)
    imp = jnp.einsum('bgrqn,nj->bgqj', p_c, overlap)
    cur = (q_pos // L_SEL)[:, None]
    forced = (blk[None, :] == 0) | (blk[None, :] == cur) | (blk[None, :] == cur - 1)
    score = jnp.where(blk[None, :] > cur, -1.0, jnp.where(forced, FORCE, imp))
    vals, sel = lax.top_k(score, min(N_SELECT, n_sel))
    bi = jnp.arange(B)[:, None, None, None]
    gi = jnp.arange(NSA_KV)[None, :, None, None]
    ks = kb[bi, gi, sel]
    vs = vb[bi, gi, sel]
    pos = sel[..., None] * L_SEL + jnp.arange(L_SEL)
    dist_s = q_pos[None, None, :, None, None] - pos
    mask_s = ((vals >= 0)[..., None] & (dist_s >= 0))[:, :, None]
    s_s = (jnp.einsum('bqgrd,bgqkld->bgrqkl', q, ks).astype(jnp.float32) * scale
           - slopes[None, :, :, None, None, None] * dist_s[:, :, None].astype(jnp.float32))
    p_s = masked_softmax(s_s, mask_s, axis=(-2, -1))
    o_s = jnp.einsum('bgrqkl,bgqkld->bqgrd', p_s.astype(vs.dtype), vs)
    dist_w = q_pos[:, None] - w_pos[None, :]
    mask_w = (dist_w >= 0) & (dist_w <= WINDOW) & (w_pos[None, :] >= 0)
    s_w = jnp.einsum('bqgrd,bwgd->bgrqw', q, kw).astype(jnp.float32) * scale - sl * dist_w.astype(jnp.float32)
    p_w = masked_softmax(s_w, mask_w)
    o_w = jnp.einsum('bgrqw,bwgd->bqgrd', p_w.astype(vw.dtype), vw)
    g = jax.nn.sigmoid(gates.astype(jnp.float32))
    out = g[..., 0:1] * o_c + g[..., 1:2] * o_s + g[..., 2:3] * o_w
    return out.astype(q.dtype)


def nsa_prompt(nq, kv, ng, cmp_pe, cmp_w1, cmp_w2, slopes):
    B, T = nq.shape[:2]
    kc, vc, c_end, kb, vb = nsa_keys(kv[:, :, :N_KV_ROWS], cmp_pe, cmp_w1, cmp_w2)
    win = jnp.pad(kv[:, :, N_KV_ROWS:], ((0, 0), (WINDOW, 0), (0, 0), (0, 0), (0, 0)))

    def block(i):
        q0 = i * Q_BLOCK
        qb = lax.dynamic_slice_in_dim(nq, q0, Q_BLOCK, axis=1)
        gb = lax.dynamic_slice_in_dim(ng, q0, Q_BLOCK, axis=1)
        wb = lax.dynamic_slice_in_dim(win, q0, WINDOW + Q_BLOCK, axis=1)
        q_pos = q0 + jnp.arange(Q_BLOCK)
        w_pos = q0 - WINDOW + jnp.arange(WINDOW + Q_BLOCK)
        return nsa_core(qb, q_pos, kc, vc, c_end, kb, vb, wb[:, :, 0], wb[:, :, 1], w_pos, gb, slopes)

    o = lax.map(block, jnp.arange(T // Q_BLOCK))
    return o.swapaxes(0, 1).reshape(B, T, NSA_HEADS * NSA_HD)


def nsa_sample(nq, kv, ng, cache_kv, page_table, cache_win, cmp_pe, cmp_w1, cmp_w2, slopes):
    DB, S = nq.shape[:2]
    wb_len = cache_win.shape[1]
    q_pos = PAST_LEN + jnp.arange(S)
    w_pos = PAST_LEN - wb_len + jnp.arange(wb_len + S)

    def one(args):
        q1, kv1, g1, pt1, win1 = args
        rows = cache_kv[pt1].reshape(PAST_LEN, N_KV_ROWS, NSA_KV, NSA_HD)
        kv4 = jnp.concatenate([rows, kv1[:, :N_KV_ROWS]], axis=0)[None]
        kc, vc, c_end, kb, vb = nsa_keys(kv4, cmp_pe, cmp_w1, cmp_w2)
        wk = jnp.concatenate([win1, kv1[:, N_KV_ROWS:]], axis=0)[None]
        o = nsa_core(q1[None], q_pos, kc, vc, c_end, kb, vb, wk[:, :, 0], wk[:, :, 1], w_pos, g1[None], slopes)
        return o[0], wk[0, -wb_len:]

    o, new_win = lax.map(one, (nq, kv, ng, page_table, cache_win))
    return o.reshape(DB, S, NSA_HEADS * NSA_HD), new_win


def conv_ffn(h, prev, w_up, conv_w, conv_b, w_down):
    a, b = jnp.split(h @ w_up, 2, axis=-1)
    T = a.shape[1]
    a_ext = jnp.concatenate([prev.astype(a.dtype), a], axis=1)
    y = conv_b + sum(conv_w[j] * a_ext[:, j:j + T] for j in range(CONV_W))
    out = (jax.nn.gelu(y) * b) @ w_down
    return out, a_ext[:, -(CONV_W - 1):]


def block_post(x, mix, mods, w_out, g_post_mix, g_pre_ffn, w_up, conv_w, conv_b, w_down, g_post_ffn, conv_prev):
    x = x + mods[2] * rmsnorm(mix @ w_out, g_post_mix)
    h = rmsnorm(x, g_pre_ffn) * (1.0 + mods[4]) + mods[3]
    f, conv_new = conv_ffn(h, conv_prev, w_up, conv_w, conv_b, w_down)
    x = x + mods[5] * rmsnorm(f, g_post_ffn)
    return x, conv_new


def setup_inputs(seed: int = 0) -> dict:
    key = jax.random.key(seed)
    ks = jax.random.split(key, 26)
    f32 = jnp.float32
    n_pages = PAST_LEN // PAGE_SIZE
    n_used = DEC_BATCH * n_pages
    n_phys = (5 * n_used + 3) // 4
    win_buf = min(WINDOW, PAST_LEN)

    def nrm(k, shape, s):
        return jax.random.normal(k, shape, f32) * s

    def gain(k, shape):
        return 1.0 + nrm(k, shape, 0.01)

    page_table = jax.random.permutation(ks[0], n_phys)[:n_used].reshape(DEC_BATCH, n_pages).astype(jnp.int32)
    return {
        'x_prompt': nrm(ks[1], (BATCH, SEQ, D_MODEL), 1.0),
        'x_sample': nrm(ks[2], (DEC_BATCH, DEC_SEQ, D_MODEL), 1.0),
        'cache_kv': nrm(ks[3], (DEPTH, n_phys, PAGE_SIZE, N_KV_ROWS, NSA_KV, NSA_HD), 1.0),
        'cache_win': nrm(ks[4], (DEPTH, DEC_BATCH, win_buf, 2, NSA_KV, NSA_HD), 1.0),
        'state_ret': nrm(ks[5], (DEPTH, DEC_BATCH, RET_HEADS, RET_DK, RET_DV), 0.5),
        'state_conv': nrm(ks[6], (DEPTH, DEC_BATCH, CONV_W - 1, D_FF), 1.0),
        'page_table': page_table,
        'c_prompt': nrm(ks[7], (BATCH, D_MODEL), 1.0),
        'c_sample': nrm(ks[8], (DEC_BATCH, D_MODEL), 1.0),
        'w_ada': nrm(ks[9], (DEPTH, D_MODEL, 6 * D_MODEL), 0.5 * D_MODEL ** -0.5),
        'b_ada': nrm(ks[10], (DEPTH, 6 * D_MODEL), 0.01),
        'g_pre_mix': gain(ks[11], (DEPTH, D_MODEL)),
        'w_in': nrm(ks[12], (DEPTH, D_MODEL, P_IN), D_MODEL ** -0.5),
        'cmp_pe': nrm(ks[13], (DEPTH, 2, L_CMP, NSA_HD), 0.1),
        'cmp_w1': nrm(ks[14], (DEPTH, 2, L_CMP, NSA_HD, CMP_HID), (L_CMP * NSA_HD) ** -0.5),
        'cmp_w2': nrm(ks[15], (DEPTH, 2, CMP_HID, NSA_HD), CMP_HID ** -0.5),
        'ret_gn': gain(ks[16], (DEPTH, RET_HEADS * RET_DV)),
        'w_out': nrm(ks[17], (DEPTH, MIX_WIDTH, D_MODEL), MIX_WIDTH ** -0.5),
        'g_post_mix': gain(ks[18], (DEPTH, D_MODEL)),
        'g_pre_ffn': gain(ks[19], (DEPTH, D_MODEL)),
        'w_up': nrm(ks[20], (DEPTH, D_MODEL, 2 * D_FF), D_MODEL ** -0.5),
        'conv_w': nrm(ks[21], (DEPTH, CONV_W, D_FF), CONV_W ** -0.5),
        'conv_b': nrm(ks[22], (DEPTH, D_FF), 0.01),
        'w_down': nrm(ks[23], (DEPTH, D_FF, D_MODEL), D_FF ** -0.5),
        'g_post_ffn': gain(ks[24], (DEPTH, D_MODEL)),
    }


def reference(x_prompt, x_sample, cache_kv, cache_win, state_ret, state_conv, page_table, c_prompt, c_sample,
              w_ada, b_ada, g_pre_mix, w_in, cmp_pe, cmp_w1, cmp_w2, ret_gn, w_out, g_post_mix, g_pre_ffn,
              w_up, conv_w, conv_b, w_down, g_post_ffn):
    slopes = alibi_slopes()
    log_g = retention_log_decay()
    yp, ys = x_prompt, x_sample
    kvp_l, winp_l, retp_l, convp_l = [], [], [], []
    kvs_l, wins_l, rets_l, convs_l = [], [], [], []
    for l in range(DEPTH):
        post_w = (w_out[l], g_post_mix[l], g_pre_ffn[l], w_up[l], conv_w[l], conv_b[l], w_down[l], g_post_ffn[l])
        mods, (rq, rk, rv, rg, nq, kv, ng) = block_pre(yp, c_prompt, w_ada[l], b_ada[l], g_pre_mix[l], w_in[l])
        ret_o, ret_state = retention_prompt(rq, rk, rv, log_g)
        nsa_o = nsa_prompt(nq, kv, ng, cmp_pe[l], cmp_w1[l], cmp_w2[l], slopes)
        mix = jnp.concatenate([retention_out(ret_o, rg, ret_gn[l], yp.dtype), nsa_o], axis=-1)
        conv0 = jnp.zeros((yp.shape[0], CONV_W - 1, D_FF), yp.dtype)
        yp, conv_new = block_post(yp, mix, mods, *post_w, conv0)
        kvp_l.append(kv[:, :, :N_KV_ROWS])
        winp_l.append(kv[:, -min(WINDOW, kv.shape[1]):, N_KV_ROWS:])
        retp_l.append(ret_state)
        convp_l.append(conv_new)
        mods, (rq, rk, rv, rg, nq, kv, ng) = block_pre(ys, c_sample, w_ada[l], b_ada[l], g_pre_mix[l], w_in[l])
        ret_state, ret_o = retention_chunk(state_ret[l].astype(jnp.float32), rq, rk, rv, log_g)
        nsa_o, win_new = nsa_sample(nq, kv, ng, cache_kv[l], page_table, cache_win[l],
                                    cmp_pe[l], cmp_w1[l], cmp_w2[l], slopes)
        mix = jnp.concatenate([retention_out(ret_o, rg, ret_gn[l], ys.dtype), nsa_o], axis=-1)
        ys, conv_new = block_post(ys, mix, mods, *post_w, state_conv[l])
        kvs_l.append(kv[:, :, :N_KV_ROWS])
        wins_l.append(win_new)
        rets_l.append(ret_state)
        convs_l.append(conv_new)
    kv_rows_p = jnp.stack(kvp_l)
    win_p = jnp.stack(winp_l)
    ret_p = jnp.stack(retp_l)
    conv_p = jnp.stack(convp_l)
    kv_rows_s = jnp.stack(kvs_l)
    win_s = jnp.stack(wins_l)
    ret_s = jnp.stack(rets_l)
    conv_s = jnp.stack(convs_l)
    return (yp, ys, kv_rows_p, win_p, ret_p, conv_p, kv_rows_s, win_s, ret_s, conv_s)
```

```python
import functools

import jax
import jax.numpy as jnp
import numpy as np
from jax import lax
from jax.experimental import pallas as pl
from jax.experimental.pallas import tpu as pltpu

F32 = jnp.float32
BF16 = jnp.bfloat16

EPS = 1e-6
NEG = -1e30
FORCE = 1e4
RET_HEADS = 8
RET_CHUNK = 128
NSA_HEADS = 16
NSA_KV = 4
NSA_REP = NSA_HEADS // NSA_KV
L_CMP = 32
STRIDE_CMP = 16
L_SEL = 64
N_SELECT = 16
WINDOW = 512
CONV_W = 3

LANE = 128
SUBLANE = 8
VMEM_LIMIT = 56 * 1024 * 1024


def _cp(*sem):
    return pltpu.CompilerParams(dimension_semantics=sem, vmem_limit_bytes=VMEM_LIMIT)


def _silu(x):
    return x * jax.nn.sigmoid(x)


def _nt_dot(a, b):
    return lax.dot_general(a, b, (((1,), (1,)), ((), ())), preferred_element_type=F32)


def _tn_dot(a, b):
    return lax.dot_general(a, b, (((0,), (0,)), ((), ())), preferred_element_type=F32)


def _dot(a, b):
    return jnp.dot(a, b, preferred_element_type=F32)


def _ada_kernel(c_ref, w_ref, b_ref, o_ref):
    sc = _silu(c_ref[...]).astype(BF16)
    o_ref[...] = _dot(sc, w_ref[...].astype(BF16)) + b_ref[...]


def ada_mods(c_all, w_ada, b_ada, tn=512):
    m, d = c_all.shape
    n = w_ada.shape[1]
    return pl.pallas_call(
        _ada_kernel,
        out_shape=jax.ShapeDtypeStruct((m, n), F32),
        grid=(n // tn,),
        in_specs=[pl.BlockSpec((m, d), lambda j: (0, 0)),
                  pl.BlockSpec((d, tn), lambda j: (0, j)),
                  pl.BlockSpec((1, tn), lambda j: (0, j))],
        out_specs=pl.BlockSpec((m, tn), lambda j: (0, j)),
        compiler_params=_cp("arbitrary"),
        name="ada_mods",
    )(c_all, w_ada, b_ada.reshape(1, n))


def _rows(ref):
    v = ref[...]
    return v[0:1, :] if v.shape[0] == SUBLANE else v


def _rms(x, g):
    return x * lax.rsqrt(jnp.mean(x * x, axis=-1, keepdims=True) + EPS) * g


def _norm_mod_kernel(x_ref, g_ref, shift_ref, scale_ref, o_ref):
    y = _rms(x_ref[...], g_ref[...])
    o_ref[...] = (y * (1.0 + _rows(scale_ref)) + _rows(shift_ref)).astype(o_ref.dtype)


def _mod_spec(mods, tm, k, d):
    r = mods.shape[0]
    assert (r == SUBLANE) != (r == tm)
    return pl.BlockSpec((r, d), lambda i, k=k: (0, k))


def norm_mod(x, g, mods, k_shift, k_scale, tm):
    m, d = x.shape
    row = pl.BlockSpec((tm, d), lambda i: (i, 0))
    return pl.pallas_call(
        _norm_mod_kernel,
        out_shape=jax.ShapeDtypeStruct((m, d), BF16),
        grid=(m // tm,),
        in_specs=[row, pl.BlockSpec((1, d), lambda i: (0, 0)),
                  _mod_spec(mods, tm, k_shift, d), _mod_spec(mods, tm, k_scale, d)],
        out_specs=row,
        compiler_params=_cp("parallel"),
        name="norm_mod",
    )(x, g.reshape(1, d), mods, mods)


def _post_mix_kernel(f_ref, x_ref, g1_ref, g2_ref, gate_ref, shift_ref, scale_ref, x2_ref, h2_ref):
    x2 = x_ref[...] + _rows(gate_ref) * _rms(f_ref[...], g1_ref[...])
    x2_ref[...] = x2
    h2_ref[...] = (_rms(x2, g2_ref[...]) * (1.0 + _rows(scale_ref)) + _rows(shift_ref)).astype(h2_ref.dtype)


def post_mix(f, x, g_post_mix, g_pre_ffn, mods, tm):
    m, d = x.shape
    row = pl.BlockSpec((tm, d), lambda i: (i, 0))
    vec = pl.BlockSpec((1, d), lambda i: (0, 0))
    return pl.pallas_call(
        _post_mix_kernel,
        out_shape=(jax.ShapeDtypeStruct((m, d), F32), jax.ShapeDtypeStruct((m, d), BF16)),
        grid=(m // tm,),
        in_specs=[row, row, vec, vec, _mod_spec(mods, tm, 2, d), _mod_spec(mods, tm, 3, d),
                  _mod_spec(mods, tm, 4, d)],
        out_specs=(row, row),
        compiler_params=_cp("parallel"),
        name="post_mix",
    )(f, x, g_post_mix.reshape(1, d), g_pre_ffn.reshape(1, d), mods, mods, mods)


def _post_ffn_kernel(f_ref, x_ref, g_ref, gate_ref, y_ref):
    y_ref[...] = x_ref[...] + _rows(gate_ref) * _rms(f_ref[...], g_ref[...])


def post_ffn(f, x2, g_post_ffn, mods, tm):
    m, d = x2.shape
    row = pl.BlockSpec((tm, d), lambda i: (i, 0))
    return pl.pallas_call(
        _post_ffn_kernel,
        out_shape=jax.ShapeDtypeStruct((m, d), F32),
        grid=(m // tm,),
        in_specs=[row, row, pl.BlockSpec((1, d), lambda i: (0, 0)), _mod_spec(mods, tm, 5, d)],
        out_specs=row,
        compiler_params=_cp("parallel"),
        name="post_ffn",
    )(f, x2, g_post_ffn.reshape(1, d), mods)


def _mm_kernel(a_ref, w_ref, o_ref, *, nk):
    acc = _dot(a_ref[...], w_ref[...])
    if nk == 1:
        o_ref[...] = acc
    else:
        k = pl.program_id(2)

        @pl.when(k == 0)
        def _():
            o_ref[...] = acc

        @pl.when(k > 0)
        def _():
            o_ref[...] += acc


def matmul(a, w, tm, tn, tk=None):
    m, kd = a.shape
    n = w.shape[1]
    tk = kd if tk is None else tk
    nk = kd // tk
    assert m % tm == 0 and n % tn == 0 and kd % tk == 0
    return pl.pallas_call(
        functools.partial(_mm_kernel, nk=nk),
        out_shape=jax.ShapeDtypeStruct((m, n), F32),
        grid=(m // tm, n // tn, nk),
        in_specs=[pl.BlockSpec((tm, tk), lambda i, j, k: (i, k)),
                  pl.BlockSpec((tk, tn), lambda i, j, k: (k, j))],
        out_specs=pl.BlockSpec((tm, tn), lambda i, j, k: (i, j)),
        compiler_params=_cp("parallel", "parallel", "arbitrary"),
        name="matmul",
    )(a, w)


def _ret_out(o, gate, gn):
    mu = jnp.mean(o, axis=-1, keepdims=True)
    var = jnp.mean(jnp.square(o - mu), axis=-1, keepdims=True)
    return (o - mu) * lax.rsqrt(var + EPS) * gn * _silu(gate)


def _ret_prompt_kernel(q_ref, k_ref, v_ref, gate_ref, dmat_ref, dq_ref, dk_ref, gc_ref, gn_ref,
                       o_ref, st_ref, *, dk_scale):
    @pl.when(pl.program_id(1) == 0)
    def _():
        st_ref[...] = jnp.zeros_like(st_ref)

    k = k_ref[...] * dk_scale
    qb = q_ref[...].astype(BF16)
    vb = v_ref[...].astype(BF16)
    attn = _nt_dot(qb, k.astype(BF16)) * dmat_ref[...]
    st = st_ref[...]
    o = _dot(attn.astype(BF16), vb) + _dot(qb, st.astype(BF16)) * dq_ref[...]
    kdec = (k * dk_ref[...]).astype(BF16)
    st_ref[...] = gc_ref[...] * st + _tn_dot(kdec, vb)
    o_ref[...] = _ret_out(o, gate_ref[...], gn_ref[...]).astype(o_ref.dtype)


def _ret_tables(c):
    log_g = jnp.log(1.0 - jnp.exp2(-5.0 - jnp.arange(RET_HEADS, dtype=F32)))
    i = jnp.arange(c, dtype=F32)
    diff = i[:, None] - i[None, :]
    dmat = jnp.where(diff >= 0, jnp.exp(log_g[:, None, None] * jnp.maximum(diff, 0.0)), 0.0)
    dq = jnp.exp(log_g[:, None] * (i[None, :] + 1.0))[:, :, None]
    dk = jnp.exp(log_g[:, None] * (c - 1.0 - i[None, :]))[:, :, None]
    gc = jnp.exp(log_g * c)[:, None, None]
    return dmat, dq, dk, gc


def retention_prompt(p, ret_gn, dkv):
    t = p.shape[0]
    c = RET_CHUNK
    h = RET_HEADS
    dmat, dq, dk, gc = _ret_tables(c)

    def col(part):
        return pl.BlockSpec((c, dkv), lambda hh, cc, part=part: (cc, part * h + hh))

    def tab(shape):
        return pl.BlockSpec((None,) + shape, lambda hh, cc: (hh, 0, 0))

    return pl.pallas_call(
        functools.partial(_ret_prompt_kernel, dk_scale=dkv ** -0.5),
        out_shape=(jax.ShapeDtypeStruct((t, h * dkv), BF16), jax.ShapeDtypeStruct((h, dkv, dkv), F32)),
        grid=(h, t // c),
        in_specs=[col(0), col(1), col(2), col(3), tab((c, c)), tab((c, 1)), tab((c, 1)), tab((1, 1)),
                  pl.BlockSpec((1, dkv), lambda hh, cc: (0, hh))],
        out_specs=(pl.BlockSpec((c, dkv), lambda hh, cc: (cc, hh)),
                   pl.BlockSpec((None, dkv, dkv), lambda hh, cc: (hh, 0, 0))),
        compiler_params=_cp("parallel", "arbitrary"),
        name="retention_prompt",
    )(p, p, p, p, dmat, dq, dk, gc, ret_gn.reshape(1, h * dkv))


def _ret_sample_kernel(q_ref, k_ref, v_ref, gate_ref, st_ref, dmat_ref, dq_ref, dk_ref, gc_ref, gn_ref,
                       o_ref, nst_ref, *, dkv, s, dk_scale):
    rows = 2 * s
    rid = lax.broadcasted_iota(jnp.int32, (rows, 1), 0)
    first = rid < s
    for h in range(RET_HEADS):
        sl = slice(h * dkv, (h + 1) * dkv)
        q = q_ref[:, sl]
        k = k_ref[:, sl] * dk_scale
        v = v_ref[:, sl]
        attn = _nt_dot(q, k) * dmat_ref[h]
        o = _dot(attn, v)
        dq = dq_ref[h]
        kdec = k * dk_ref[h]
        for b in range(2):
            st = st_ref[b, h]
            mine = first if b == 0 else jnp.logical_not(first)
            o = o + jnp.where(mine, _dot(q, st) * dq, 0.0)
            nst_ref[b, h] = gc_ref[h] * st + _tn_dot(jnp.where(mine, kdec, 0.0), v)
        o_ref[:, sl] = _ret_out(o, gate_ref[:, sl], gn_ref[:, sl]).astype(o_ref.dtype)


def retention_sample(p, state, ret_gn, s):
    db, h, dkv, _ = state.shape
    rows = 2 * s
    assert rows == SUBLANE and db % 2 == 0
    log_g = jnp.log(1.0 - jnp.exp2(-5.0 - jnp.arange(h, dtype=F32)))
    i = jnp.arange(rows)
    tok = (i % s).astype(F32)
    same = (i[:, None] // s) == (i[None, :] // s)
    diff = tok[:, None] - tok[None, :]
    dmat = jnp.where(same & (diff >= 0), jnp.exp(log_g[:, None, None] * jnp.maximum(diff, 0.0)), 0.0)
    dq = jnp.exp(log_g[:, None] * (tok[None, :] + 1.0))[:, :, None]
    dk = jnp.exp(log_g[:, None] * (s - 1.0 - tok[None, :]))[:, :, None]
    gc = jnp.exp(log_g * s)[:, None, None]
    w = h * dkv

    def col(part):
        return pl.BlockSpec((rows, w), lambda i, part=part: (i, part))

    def full(a):
        return pl.BlockSpec(a.shape, lambda i: (0,) * a.ndim)

    st_spec = pl.BlockSpec((2, h, dkv, dkv), lambda i: (i, 0, 0, 0))
    return pl.pallas_call(
        functools.partial(_ret_sample_kernel, dkv=dkv, s=s, dk_scale=dkv ** -0.5),
        out_shape=(jax.ShapeDtypeStruct((db * s, w), BF16), jax.ShapeDtypeStruct(state.shape, F32)),
        grid=(db // 2,),
        in_specs=[col(0), col(1), col(2), col(3), st_spec, full(dmat), full(dq), full(dk), full(gc),
                  pl.BlockSpec((1, w), lambda i: (0, 0))],
        out_specs=(pl.BlockSpec((rows, w), lambda i: (i, 0)), st_spec),
        compiler_params=_cp("parallel"),
        name="retention_sample",
    )(p, p, p, p, state, dmat, dq, dk, gc, ret_gn.reshape(1, w))


def _compress_block(load_rows, pe_ref, w1_ref, w2, accb_ref, nh):
    half = L_CMP // 2
    acc_a = None
    acc_b = None
    for l in range(half):
        y = load_rows(l)
        da = _dot((y + pe_ref[l:l + 1, :]).astype(BF16), w1_ref[l])
        db = _dot((y + pe_ref[half + l:half + l + 1, :]).astype(BF16), w1_ref[half + l])
        acc_a = da if acc_a is None else acc_a + da
        acc_b = db if acc_b is None else acc_b + db
    accb_ref[0:nh, :] = acc_b
    accb_ref[nh:nh + SUBLANE, :] = jnp.zeros((SUBLANE, acc_b.shape[1]), F32)
    h = jax.nn.gelu(acc_a + accb_ref[pl.ds(1, nh), :])
    return _dot(h.astype(BF16), w2)


def _masked_softmax(s, mask):
    s = jnp.where(mask, s, NEG)
    e = jnp.exp(s - jnp.max(s, axis=-1, keepdims=True))
    return jnp.where(mask, e / jnp.sum(e, axis=-1, keepdims=True), 0.0)


def _importance(psum, ov):
    hi = psum.astype(BF16)
    lo = (psum - hi.astype(F32)).astype(BF16)
    return _dot(hi, ov) + _dot(lo, ov)


def _select_blocks(imp, q_pos, n_sel):
    rows, nb = imp.shape
    blk = lax.broadcasted_iota(jnp.int32, (1, nb), 1)
    blkf = blk.astype(F32)
    cur = q_pos // L_SEL
    forced = (blk == 0) | (blk == cur) | (blk == cur - 1)
    score = jnp.where(blk > cur, -1.0, jnp.where(forced, FORCE, imp))
    score = jnp.where(blk >= n_sel, -2.0, score)
    work = score
    picked = jnp.zeros((rows, nb), jnp.bool_)
    for _ in range(min(N_SELECT, n_sel)):
        m = jnp.max(work, axis=-1, keepdims=True)
        first = jnp.min(jnp.where(work == m, blkf, float(nb)), axis=-1, keepdims=True)
        hit = blkf == first
        picked = picked | hit
        work = jnp.where(hit, -3.0, work)
    return jnp.where(picked & (score >= 0.0), 1.0, 0.0)


def _overlap_matrix(nc, nb):
    n = jnp.arange(nc)[:, None]
    j = jnp.arange(nb)[None, :]
    c_start = n * STRIDE_CMP
    c_end = c_start + L_CMP - 1
    return ((c_start < (j + 1) * L_SEL) & (c_end >= j * L_SEL)).astype(BF16)


def _expand_matrix(nb, nkeys):
    return (jnp.arange(nb)[:, None] == (jnp.arange(nkeys)[None, :] // L_SEL)).astype(BF16)


def _slopes():
    h = jnp.arange(1, NSA_HEADS + 1, dtype=F32)
    return jnp.exp2(-8.0 * h / NSA_HEADS).reshape(NSA_KV, NSA_REP)


def _compress_prompt_kernel(x_ref, pe_ref, w1_ref, w2_ref, o_ref, accb_ref):
    nh = x_ref.shape[0] // STRIDE_CMP
    o_ref[...] = _compress_block(lambda l: x_ref[pl.ds(l, nh, stride=STRIDE_CMP), :],
                                 pe_ref, w1_ref, w2_ref[...], accb_ref, nh)


def compress_prompt(p, kv_col, cmp_pe, w1b, w2b, hd):
    t = p.shape[0]
    nh = t // STRIDE_CMP
    hid = w1b.shape[-1]
    cb = kv_col // hd
    return pl.pallas_call(
        _compress_prompt_kernel,
        out_shape=jax.ShapeDtypeStruct((2, NSA_KV, nh, hd), F32),
        grid=(2, NSA_KV),
        in_specs=[pl.BlockSpec((t, hd), lambda kind, g: (0, cb + kind * NSA_KV + g)),
                  pl.BlockSpec((None, L_CMP, hd), lambda kind, g: (kind, 0, 0)),
                  pl.BlockSpec((None, L_CMP, hd, hid), lambda kind, g: (kind, 0, 0, 0)),
                  pl.BlockSpec((None, hid, hd), lambda kind, g: (kind, 0, 0))],
        out_specs=pl.BlockSpec((None, None, nh, hd), lambda kind, g: (kind, g, 0, 0)),
        scratch_shapes=[pltpu.VMEM((nh + SUBLANE, hid), F32)],
        compiler_params=_cp("parallel", "parallel"),
        name="compress_prompt",
    )(p, cmp_pe, w1b, w2b)


def _cmp_select_kernel(q_ref, kc_ref, vc_ref, ov_ref, sl_ref, oc_ref, sel_ref, *, tq, hd, n_sel, scale):
    nc = kc_ref.shape[0]
    q_pos = pl.program_id(1) * tq + lax.broadcasted_iota(jnp.int32, (tq, 1), 0)
    c_end = lax.broadcasted_iota(jnp.int32, (1, nc), 1) * STRIDE_CMP + (L_CMP - 1)
    dist = q_pos - c_end
    mask = dist >= 0
    distf = dist.astype(F32)
    kcb = kc_ref[...].astype(BF16)
    vcb = vc_ref[...].astype(BF16)
    psum = jnp.zeros((tq, nc), F32)
    for r in range(NSA_REP):
        sl = slice(r * hd, (r + 1) * hd)
        s = _nt_dot(q_ref[:, sl].astype(BF16), kcb) * scale - sl_ref[r][:, 0:1] * distf
        p = _masked_softmax(s, mask)
        psum = psum + p
        oc_ref[:, sl] = _dot(p.astype(BF16), vcb)
    sel_ref[...] = _select_blocks(_importance(psum, ov_ref[...]), q_pos, n_sel).astype(sel_ref.dtype)


def cmp_select_prompt(p, q_col, kcvc, tq=128):
    t = p.shape[0]
    _, g_, nc, hd = kcvc.shape
    n_sel = -(-t // L_SEL)
    assert N_SELECT <= n_sel <= LANE
    w = NSA_REP * hd
    slopes = jnp.broadcast_to(_slopes()[:, :, None, None], (NSA_KV, NSA_REP, 1, LANE))
    return pl.pallas_call(
        functools.partial(_cmp_select_kernel, tq=tq, hd=hd, n_sel=n_sel, scale=hd ** -0.5),
        out_shape=(jax.ShapeDtypeStruct((t, NSA_KV * w), F32), jax.ShapeDtypeStruct((t, NSA_KV * LANE), BF16)),
        grid=(NSA_KV, t // tq),
        in_specs=[pl.BlockSpec((tq, w), lambda g, i: (i, q_col // w + g)),
                  pl.BlockSpec((None, None, nc, hd), lambda g, i: (0, g, 0, 0)),
                  pl.BlockSpec((None, None, nc, hd), lambda g, i: (1, g, 0, 0)),
                  pl.BlockSpec((nc, LANE), lambda g, i: (0, 0)),
                  pl.BlockSpec((None, NSA_REP, 1, LANE), lambda g, i: (g, 0, 0, 0))],
        out_specs=(pl.BlockSpec((tq, w), lambda g, i: (i, g)),
                   pl.BlockSpec((tq, LANE), lambda g, i: (i, g))),
        compiler_params=_cp("parallel", "parallel"),
        name="cmp_select_prompt",
    )(p, kcvc, kcvc, _overlap_matrix(nc, LANE), slopes)


def _flash_update(q, kb, vb, bias, mask, m_ref, l_ref, acc_ref, r, scale):
    s = jnp.where(mask, _nt_dot(q, kb) * scale - bias, NEG)
    m_old = m_ref[r]
    m_new = jnp.maximum(m_old, jnp.max(s, axis=-1, keepdims=True))
    alpha = jnp.exp(m_old - m_new)
    e = jnp.where(mask, jnp.exp(s - m_new), 0.0)
    l_ref[r] = alpha * l_ref[r] + jnp.sum(e, axis=-1, keepdims=True)
    acc_ref[r] = alpha * acc_ref[r] + _dot(e.astype(BF16), vb)
    m_ref[r] = m_new


def _slc_win_kernel(qi_ref, ki_ref, wlo_ref, q_ref, sel_ref, e_ref, ks_ref, vs_ref, kw_ref, vw_ref,
                    gate_ref, oc_ref, sl_ref, o_ref, ms, ls, accs, mw, lw, accw, *, tq, tk, hd, scale):
    step = pl.program_id(1)
    qt = qi_ref[step]
    kt = ki_ref[step]

    @pl.when(kt == 0)
    def _():
        for m_ref, l_ref, acc_ref in ((ms, ls, accs), (mw, lw, accw)):
            m_ref[...] = jnp.full(m_ref.shape, NEG, F32)
            l_ref[...] = jnp.zeros(l_ref.shape, F32)
            acc_ref[...] = jnp.zeros(acc_ref.shape, F32)

    q_pos = qt * tq + lax.broadcasted_iota(jnp.int32, (tq, 1), 0)
    k_pos = kt * tk + lax.broadcasted_iota(jnp.int32, (1, tk), 1)
    dist = q_pos - k_pos
    distf = dist.astype(F32)
    causal = dist >= 0
    mask_s = causal & (_dot(sel_ref[...], e_ref[...]) > 0.5)
    ksb = ks_ref[...].astype(BF16)
    vsb = vs_ref[...].astype(BF16)
    for r in range(NSA_REP):
        q = q_ref[:, r * hd:(r + 1) * hd].astype(BF16)
        _flash_update(q, ksb, vsb, sl_ref[r][:, 0:1] * distf, mask_s, ms, ls, accs, r, scale)

    @pl.when(kt >= wlo_ref[step])
    def _():
        mask_w = causal & (dist <= WINDOW)
        kwb = kw_ref[...].astype(BF16)
        vwb = vw_ref[...].astype(BF16)
        for r in range(NSA_REP):
            q = q_ref[:, r * hd:(r + 1) * hd].astype(BF16)
            _flash_update(q, kwb, vwb, sl_ref[r][:, 0:1] * distf, mask_w, mw, lw, accw, r, scale)

    @pl.when(kt == (qt * tq + tq - 1) // tk)
    def _():
        gate = jax.nn.sigmoid(gate_ref[...])
        for r in range(NSA_REP):
            sl = slice(r * hd, (r + 1) * hd)
            out = (gate[:, 3 * r:3 * r + 1] * oc_ref[:, sl]
                   + gate[:, 3 * r + 1:3 * r + 2] * (accs[r] / ls[r])
                   + gate[:, 3 * r + 2:3 * r + 3] * (accw[r] / lw[r]))
            o_ref[:, sl] = out.astype(o_ref.dtype)


def slc_win_prompt(p, q_col, kv_col, gates, o_c, sel, tq=128, tk=512):
    t = p.shape[0]
    hd = o_c.shape[1] // NSA_HEADS
    w = NSA_REP * hd
    tk = min(tk, t)
    assert t % tq == 0 and t % tk == 0 and tk % tq == 0
    qi, ki, wlo = [], [], []
    for i in range(t // tq):
        last = (i * tq + tq - 1) // tk
        for k in range(last + 1):
            qi.append(i)
            ki.append(k)
            wlo.append(max(0, (i * tq - WINDOW) // tk))
    qi, ki, wlo = (jnp.asarray(a, jnp.int32) for a in (qi, ki, wlo))
    cb = kv_col // hd
    slopes = jnp.broadcast_to(_slopes()[:, :, None, None], (NSA_KV, NSA_REP, 1, LANE))

    def kv_spec(kind, win):
        if win:
            return pl.BlockSpec((tk, hd), lambda g, s, qi, ki, wlo: (jnp.maximum(ki[s], wlo[s]), cb + kind * NSA_KV + g))
        return pl.BlockSpec((tk, hd), lambda g, s, qi, ki, wlo: (ki[s], cb + kind * NSA_KV + g))

    qrow = lambda g, s, qi, ki, wlo: (qi[s], g)
    return pl.pallas_call(
        functools.partial(_slc_win_kernel, tq=tq, tk=tk, hd=hd, scale=hd ** -0.5),
        out_shape=jax.ShapeDtypeStruct((t, NSA_KV * w), BF16),
        grid_spec=pltpu.PrefetchScalarGridSpec(
            num_scalar_prefetch=3,
            grid=(NSA_KV, int(qi.shape[0])),
            in_specs=[pl.BlockSpec((tq, w), lambda g, s, qi, ki, wlo: (qi[s], q_col // w + g)),
                      pl.BlockSpec((tq, LANE), qrow),
                      pl.BlockSpec((LANE, tk), lambda g, s, qi, ki, wlo: (0, ki[s])),
                      kv_spec(2, False), kv_spec(3, False), kv_spec(4, True), kv_spec(5, True),
                      pl.BlockSpec((None, tq, 3 * NSA_REP), lambda g, s, qi, ki, wlo: (g, qi[s], 0)),
                      pl.BlockSpec((tq, w), qrow),
                      pl.BlockSpec((None, NSA_REP, 1, LANE), lambda g, s, qi, ki, wlo: (g, 0, 0, 0))],
            out_specs=pl.BlockSpec((tq, w), qrow),
            scratch_shapes=[pltpu.VMEM((NSA_REP, tq, 1), F32), pltpu.VMEM((NSA_REP, tq, 1), F32),
                            pltpu.VMEM((NSA_REP, tq, hd), F32),
                            pltpu.VMEM((NSA_REP, tq, 1), F32), pltpu.VMEM((NSA_REP, tq, 1), F32),
                            pltpu.VMEM((NSA_REP, tq, hd), F32)]),
        compiler_params=_cp("parallel", "arbitrary"),
        name="slc_win_prompt",
    )(qi, ki, wlo, p, sel, _expand_matrix(LANE, t), p, p, p, p, gates, o_c, slopes)


def _page_specs(n_pages, page, width, col_block):
    return [pl.BlockSpec((None, page, width), lambda b, pt, j=j: (pt[b * n_pages + j], 0, col_block))
            for j in range(n_pages)]


def _compress_sample_kernel(pt_ref, *refs, n_pages, hd):
    page_refs = refs[:n_pages]
    pe_ref, w1_ref, w2_ref, o_ref, accb_ref, x_ref = refs[n_pages:]
    page = page_refs[0].shape[0]
    nh = n_pages * page // STRIDE_CMP
    for kind in range(2):
        for g in range(NSA_KV):
            c0 = (kind * NSA_KV + g) * hd
            for j, pr in enumerate(page_refs):
                x_ref[j * page:(j + 1) * page, :] = pr[:, c0:c0 + hd]
            o_ref[kind * NSA_KV + g] = _compress_block(
                lambda l: x_ref[pl.ds(l, nh, stride=STRIDE_CMP), :],
                pe_ref.at[kind], w1_ref.at[kind], w2_ref[kind], accb_ref, nh)


def compress_sample(cache3, page_table, cmp_pe, w1b, w2b, hd):
    db, n_pages = page_table.shape
    page = cache3.shape[1]
    nh = n_pages * page // STRIDE_CMP
    hid = w1b.shape[-1]
    full = lambda a: pl.BlockSpec(a.shape, lambda b, pt: (0,) * a.ndim)
    return pl.pallas_call(
        functools.partial(_compress_sample_kernel, n_pages=n_pages, hd=hd),
        out_shape=jax.ShapeDtypeStruct((db, 2 * NSA_KV, nh, hd), F32),
        grid_spec=pltpu.PrefetchScalarGridSpec(
            num_scalar_prefetch=1,
            grid=(db,),
            in_specs=_page_specs(n_pages, page, 2 * NSA_KV * hd, 0) + [full(cmp_pe), full(w1b), full(w2b)],
            out_specs=pl.BlockSpec((None, 2 * NSA_KV, nh, hd), lambda b, pt: (b, 0, 0, 0)),
            scratch_shapes=[pltpu.VMEM((nh + SUBLANE, hid), F32), pltpu.VMEM((n_pages * page, hd), F32)]),
        compiler_params=_cp("parallel"),
        name="compress_sample",
    )(page_table.reshape(-1), *([cache3] * n_pages), cmp_pe, w1b, w2b)


def _softmax2(s_a, mask_a, s_b, mask_b):
    s_a = jnp.where(mask_a, s_a, NEG)
    s_b = jnp.where(mask_b, s_b, NEG)
    m = jnp.maximum(jnp.max(s_a, axis=-1, keepdims=True), jnp.max(s_b, axis=-1, keepdims=True))
    e_a = jnp.where(mask_a, jnp.exp(s_a - m), 0.0)
    e_b = jnp.where(mask_b, jnp.exp(s_b - m), 0.0)
    return e_a, e_b, jnp.sum(e_a, axis=-1, keepdims=True) + jnp.sum(e_b, axis=-1, keepdims=True)


def _nsa_sample_kernel(pt_ref, *refs, n_pages, hd, s_new, scale):
    page_refs = refs[:n_pages]
    (q_ref, gate_ref, kcvc_ref, new_ref, win_ref, ov_ref, e_ref, rm_ref, sl_ref, o_ref, wn_ref) = refs[n_pages:]
    page = page_refs[0].shape[0]
    past = n_pages * page
    wb = win_ref.shape[0]
    gw = NSA_KV * hd
    rows = NSA_REP * s_new
    nc = kcvc_ref.shape[1]
    n_cmp = (past + s_new - L_CMP) // STRIDE_CMP + 1
    n_sel = -(-(past + s_new) // L_SEL)
    tok = lax.broadcasted_iota(jnp.int32, (rows, 1), 0) % s_new
    q_pos = past + tok
    tnew = lax.broadcasted_iota(jnp.int32, (1, SUBLANE), 1)
    dist_new = tok - tnew
    mask_new = (dist_new >= 0) & (tnew < s_new)
    for g in range(NSA_KV):
        qf = q_ref[g]
        q = qf.astype(BF16)
        slope = sl_ref[g]
        n = lax.broadcasted_iota(jnp.int32, (1, nc), 1)
        dist_c = q_pos - (n * STRIDE_CMP + (L_CMP - 1))
        mask_c = (dist_c >= 0) & (n < n_cmp)
        s_c = _nt_dot(q, kcvc_ref[g].astype(BF16)) * scale - slope * dist_c.astype(F32)
        p_c = _masked_softmax(s_c, mask_c)
        o_c = _dot(p_c.astype(BF16), kcvc_ref[NSA_KV + g].astype(BF16))
        picked = _select_blocks(_importance(_dot(rm_ref[...], p_c), ov_ref[...]), q_pos, n_sel)
        key_mask = _dot(picked.astype(BF16), e_ref[...]) > 0.5
        ks = jnp.concatenate([pr[:, g * hd:(g + 1) * hd].astype(BF16) for pr in page_refs], axis=0)
        vs = jnp.concatenate([pr[:, gw + g * hd:gw + (g + 1) * hd].astype(BF16) for pr in page_refs], axis=0)
        dist_s = q_pos - lax.broadcasted_iota(jnp.int32, (1, past), 1)
        s_s = _nt_dot(q, ks) * scale - slope * dist_s.astype(F32)
        k_new = new_ref[:, 2 * gw + g * hd:2 * gw + (g + 1) * hd]
        v_new = new_ref[:, 3 * gw + g * hd:3 * gw + (g + 1) * hd]
        s_n = _nt_dot(qf, k_new) * scale - slope * dist_new.astype(F32)
        cur_picked = picked[:, past // L_SEL:past // L_SEL + 1] > 0.5
        e_s, e_n, l_s = _softmax2(s_s, key_mask & (dist_s >= 0), s_n, mask_new & cur_picked)
        o_s = (_dot(e_s.astype(BF16), vs) + _dot(e_n, v_new)) / l_s
        dist_w = wb + tok - lax.broadcasted_iota(jnp.int32, (1, wb), 1)
        s_w = _nt_dot(q, win_ref[:, g * hd:(g + 1) * hd].astype(BF16)) * scale - slope * dist_w.astype(F32)
        kw_new = new_ref[:, 4 * gw + g * hd:4 * gw + (g + 1) * hd]
        vw_new = new_ref[:, 5 * gw + g * hd:5 * gw + (g + 1) * hd]
        s_wn = _nt_dot(qf, kw_new) * scale - slope * dist_new.astype(F32)
        e_w, e_wn, l_w = _softmax2(s_w, (dist_w >= 0) & (dist_w <= WINDOW), s_wn, mask_new)
        o_w = (_dot(e_w.astype(BF16), win_ref[:, gw + g * hd:gw + (g + 1) * hd].astype(BF16))
               + _dot(e_wn, vw_new)) / l_w
        gate = jax.nn.sigmoid(gate_ref[g])
        o_ref[g] = (gate[:, 0:1] * o_c + gate[:, 1:2] * o_s + gate[:, 2:3] * o_w).astype(o_ref.dtype)
    wn_ref[pl.ds(0, wb - s_new), :] = win_ref[pl.ds(s_new, wb - s_new), :]
    wn_ref[pl.ds(wb - s_new, s_new), :] = new_ref[0:s_new, 4 * gw:6 * gw]


def nsa_sample(cache3, page_table, q, gates, kcvc, kv_new, win, s_new):
    db, n_pages = page_table.shape
    page = cache3.shape[1]
    past = n_pages * page
    hd = q.shape[-1]
    rows = q.shape[2]
    nc = kcvc.shape[2]
    wb = win.shape[1]
    assert wb <= past and wb <= WINDOW and N_SELECT <= -(-(past + s_new) // L_SEL) <= LANE
    assert past % L_SEL == 0 and s_new <= SUBLANE and s_new <= L_SEL
    i = jnp.arange(rows)
    rm = (i[:, None] % s_new == i[None, :] % s_new).astype(F32)
    slopes = jnp.repeat(_slopes(), s_new, axis=1)[:, :, None]
    ov = _overlap_matrix(nc, LANE)
    em = _expand_matrix(LANE, past)
    per_b = lambda a: pl.BlockSpec((None,) + a.shape[1:], lambda b, pt: (b,) + (0,) * (a.ndim - 1))
    full = lambda a: pl.BlockSpec(a.shape, lambda b, pt: (0,) * a.ndim)
    return pl.pallas_call(
        functools.partial(_nsa_sample_kernel, n_pages=n_pages, hd=hd, s_new=s_new, scale=hd ** -0.5),
        out_shape=(jax.ShapeDtypeStruct(q.shape, BF16), jax.ShapeDtypeStruct(win.shape, F32)),
        grid_spec=pltpu.PrefetchScalarGridSpec(
            num_scalar_prefetch=1,
            grid=(db,),
            in_specs=_page_specs(n_pages, page, 2 * NSA_KV * hd, 1)
            + [per_b(q), per_b(gates), per_b(kcvc), per_b(kv_new), per_b(win), full(ov), full(em), full(rm),
               full(slopes)],
            out_specs=(per_b(q), per_b(win))),
        compiler_params=_cp("parallel"),
        name="nsa_sample",
    )(page_table.reshape(-1), *([cache3] * n_pages), q, gates, kcvc, kv_new, win, ov, em, rm, slopes)


def _ffn_up_prompt_kernel(h_ref, halo_ref, wa_ref, wb_ref, cw_ref, cb_ref, u_ref, tail_ref, ext_ref, *, tm):
    wa = wa_ref[...]
    h = h_ref[...]
    a = _dot(h, wa)
    prev = _dot(halo_ref[...], wa)
    ext_ref[0:SUBLANE, :] = jnp.where(pl.program_id(0) > 0, prev, 0.0)
    ext_ref[SUBLANE:SUBLANE + tm, :] = a
    y = (cb_ref[...] + cw_ref[0:1, :] * ext_ref[pl.ds(SUBLANE - 2, tm), :]
         + cw_ref[1:2, :] * ext_ref[pl.ds(SUBLANE - 1, tm), :] + cw_ref[2:3, :] * a)
    u_ref[...] = (jax.nn.gelu(y) * _dot(h, wb_ref[...])).astype(u_ref.dtype)
    tail_ref[...] = a[tm - SUBLANE:tm, :]


def ffn_up_prompt(h2, w_up_b, conv_w, conv_b, tm, tn):
    m, d = h2.shape
    ff = w_up_b.shape[1] // 2
    nj = ff // tn
    assert CONV_W == 3 and m % tm == 0 and ff % tn == 0
    return pl.pallas_call(
        functools.partial(_ffn_up_prompt_kernel, tm=tm),
        out_shape=(jax.ShapeDtypeStruct((m, ff), BF16), jax.ShapeDtypeStruct((m // tm, SUBLANE, ff), F32)),
        grid=(m // tm, nj),
        in_specs=[pl.BlockSpec((tm, d), lambda i, j: (i, 0)),
                  pl.BlockSpec((SUBLANE, d), lambda i, j: (jnp.maximum(i * (tm // SUBLANE) - 1, 0), 0)),
                  pl.BlockSpec((d, tn), lambda i, j: (0, j)),
                  pl.BlockSpec((d, tn), lambda i, j: (0, nj + j)),
                  pl.BlockSpec((CONV_W, tn), lambda i, j: (0, j)),
                  pl.BlockSpec((1, tn), lambda i, j: (0, j))],
        out_specs=(pl.BlockSpec((tm, tn), lambda i, j: (i, j)),
                   pl.BlockSpec((None, SUBLANE, tn), lambda i, j: (i, 0, j))),
        scratch_shapes=[pltpu.VMEM((tm + SUBLANE, tn), F32)],
        compiler_params=_cp("parallel", "parallel"),
        name="ffn_up_prompt",
    )(h2, h2, w_up_b, w_up_b, conv_w, conv_b.reshape(1, ff))


def _ffn_up_sample_kernel(h_ref, prev_ref, wa_ref, wb_ref, cw_ref, cb_ref, u_ref, tail_ref, ext_ref, *, db, m):
    h = h_ref[...]
    a = _dot(h, wa_ref[...])
    ext_ref[0:2 * db, :] = prev_ref[...]
    ext_ref[2 * db:2 * db + m, :] = a
    y = (cb_ref[...] + cw_ref[0:1, :] * ext_ref[0:m, :] + cw_ref[1:2, :] * ext_ref[db:db + m, :]
         + cw_ref[2:3, :] * a)
    u_ref[...] = (jax.nn.gelu(y) * _dot(h, wb_ref[...])).astype(u_ref.dtype)
    tail_ref[...] = ext_ref[m:m + 2 * db, :]


def ffn_up_sample(h2, prev, w_up_b, conv_w, conv_b, db, tn):
    m, d = h2.shape
    ff = w_up_b.shape[1] // 2
    nj = ff // tn
    assert CONV_W == 3 and m >= 2 * db and db % SUBLANE == 0
    return pl.pallas_call(
        functools.partial(_ffn_up_sample_kernel, db=db, m=m),
        out_shape=(jax.ShapeDtypeStruct((m, ff), BF16), jax.ShapeDtypeStruct((2 * db, ff), F32)),
        grid=(nj,),
        in_specs=[pl.BlockSpec((m, d), lambda j: (0, 0)),
                  pl.BlockSpec((2 * db, tn), lambda j: (0, j)),
                  pl.BlockSpec((d, tn), lambda j: (0, j)),
                  pl.BlockSpec((d, tn), lambda j: (0, nj + j)),
                  pl.BlockSpec((CONV_W, tn), lambda j: (0, j)),
                  pl.BlockSpec((1, tn), lambda j: (0, j))],
        out_specs=(pl.BlockSpec((m, tn), lambda j: (0, j)), pl.BlockSpec((2 * db, tn), lambda j: (0, j))),
        scratch_shapes=[pltpu.VMEM((m + 2 * db, tn), F32)],
        compiler_params=_cp("parallel"),
        name="ffn_up_sample",
    )(h2, prev, w_up_b, w_up_b, conv_w, conv_b.reshape(1, ff))


TN_IN = 768
TN_FF = 512
TK_DOWN = 1024


def _round_up(n, m):
    return -(-n // m) * m


def _pad_cols(a, n):
    return jnp.pad(a, ((0, 0), (0, n - a.shape[1])))


def _post(x, mix, mods, tm_row, tm_mm, w_out_b, g_post_mix, g_pre_ffn, ffn_up, w_down_b, g_post_ffn):
    f = matmul(mix, w_out_b, tm_mm, 1024)
    x2, h2 = post_mix(f, x, g_post_mix, g_pre_ffn, mods, tm_row)
    u, tail = ffn_up(h2)
    f2 = matmul(u, w_down_b, tm_mm, 2048, TK_DOWN)
    return post_ffn(f2, x2, g_post_ffn, mods, tm_row), tail


def kernel(x_prompt, x_sample, cache_kv, cache_win, state_ret, state_conv, page_table, c_prompt, c_sample,
           w_ada, b_ada, g_pre_mix, w_in, cmp_pe, cmp_w1, cmp_w2, ret_gn, w_out, g_post_mix, g_pre_ffn,
           w_up, conv_w, conv_b, w_down, g_post_ffn):
    depth = w_in.shape[0]
    bp, t, d = x_prompt.shape
    db, s, _ = x_sample.shape
    assert bp == 1
    h, dkv = state_ret.shape[2], state_ret.shape[3]
    n_phys, page, n_rows, g, hd = cache_kv.shape[1:]
    assert h == RET_HEADS and g == NSA_KV and n_rows == 4
    wb = cache_win.shape[2]
    ff = w_up.shape[2] // 2
    rw = h * dkv
    q_col = 4 * rw
    kv_col = q_col + NSA_HEADS * hd
    ng_col = kv_col + 6 * g * hd
    n_gate = NSA_HEADS * 3
    np_ = _round_up(ng_col + n_gate, TN_IN)
    ffp = _round_up(ff, max(TN_FF, TK_DOWN))
    gw = g * hd
    tm_p = min(1024, t)
    tm_s = s * db

    yp = x_prompt.reshape(t, d)
    ys = jnp.swapaxes(x_sample, 0, 1).reshape(s * db, d)
    c_all = jnp.concatenate([c_prompt, jnp.zeros((SUBLANE - 1, d), F32), c_sample], axis=0)
    outs = [[] for _ in range(8)]
    for l in range(depth):
        w_in_b = _pad_cols(w_in[l], np_).astype(BF16)
        w_out_b = w_out[l].astype(BF16)
        w_up_b = jnp.concatenate([_pad_cols(w_up[l][:, :ff], ffp), _pad_cols(w_up[l][:, ff:], ffp)],
                                 axis=1).astype(BF16)
        w_down_b = jnp.pad(w_down[l], ((0, ffp - ff), (0, 0))).astype(BF16)
        conv_w_p = _pad_cols(conv_w[l], ffp)
        conv_b_p = jnp.pad(conv_b[l], (0, ffp - ff))
        w1b = cmp_w1[l].astype(BF16)
        w2b = cmp_w2[l].astype(BF16)
        post_w = (w_out_b, g_post_mix[l], g_pre_ffn[l])

        mods = ada_mods(c_all, w_ada[l], b_ada[l])
        mods_p = mods[0:SUBLANE]
        mods_s = mods[SUBLANE:SUBLANE + db]

        hp = norm_mod(yp, g_pre_mix[l], mods_p, 0, 1, 256)
        p = matmul(hp, w_in_b, tm_p, TN_IN)
        mix_ret, ret_state = retention_prompt(p, ret_gn[l], dkv)
        kcvc = compress_prompt(p, kv_col, cmp_pe[l], w1b, w2b, hd)
        o_c, sel = cmp_select_prompt(p, q_col, kcvc)
        gates = jnp.swapaxes(p[:, ng_col:ng_col + n_gate].reshape(t, g, 3 * NSA_REP), 0, 1)
        nsa_o = slc_win_prompt(p, q_col, kv_col, gates, o_c, sel)
        mix = jnp.concatenate([mix_ret, nsa_o], axis=1)
        yp_new, tails = _post(yp, mix, mods_p, 256, tm_p, *post_w,
                              lambda h2: ffn_up_prompt(h2, w_up_b, conv_w_p, conv_b_p, tm_p, TN_FF),
                              w_down_b, g_post_ffn[l])
        wp = min(WINDOW, t)
        outs[0].append(p[:, kv_col:kv_col + 4 * gw].reshape(1, t, 4, g, hd))
        outs[1].append(p[t - wp:, kv_col + 4 * gw:kv_col + 6 * gw].reshape(1, wp, 2, g, hd))
        outs[2].append(ret_state[None])
        outs[3].append(tails[-1, SUBLANE - (CONV_W - 1):, :ff][None])
        yp = yp_new

        hs = norm_mod(ys, g_pre_mix[l], mods_s, 0, 1, db)
        ps_t = matmul(hs, w_in_b, tm_s, TN_IN)
        ps = jnp.swapaxes(ps_t.reshape(s, db, np_), 0, 1).reshape(db * s, np_)
        mix_ret_s, ret_s = retention_sample(ps, state_ret[l], ret_gn[l], s)
        cache3 = cache_kv[l].reshape(n_phys, page, n_rows * gw)
        kcvc_s = compress_sample(cache3, page_table, cmp_pe[l], w1b, w2b, hd)

        def head_rows(a, last):
            a = a.reshape(db, s, g, NSA_REP, last)
            return a.transpose(0, 2, 3, 1, 4).reshape(db, g, NSA_REP * s, last)

        q_s = head_rows(ps[:, q_col:kv_col], hd)
        gates_s = head_rows(ps[:, ng_col:ng_col + n_gate], 3)
        kv_new = jnp.pad(ps[:, kv_col:ng_col].reshape(db, s, 6 * gw), ((0, 0), (0, SUBLANE - s), (0, 0)))
        o_s, win_new = nsa_sample(cache3, page_table, q_s, gates_s, kcvc_s, kv_new,
                                  cache_win[l].reshape(db, wb, 2 * gw), s)
        nsa_o_s = o_s.reshape(db, g, NSA_REP, s, hd).transpose(0, 3, 1, 2, 4).reshape(db * s, NSA_HEADS * hd)
        mix_s = jnp.concatenate([mix_ret_s, nsa_o_s], axis=1)
        mix_s = jnp.swapaxes(mix_s.reshape(db, s, -1), 0, 1).reshape(s * db, -1)
        prev = _pad_cols(jnp.swapaxes(state_conv[l], 0, 1).reshape((CONV_W - 1) * db, ff), ffp)
        ys_new, tail_s = _post(ys, mix_s, mods_s, db, tm_s, *post_w,
                               lambda h2: ffn_up_sample(h2, prev, w_up_b, conv_w_p, conv_b_p, db, TN_FF),
                               w_down_b, g_post_ffn[l])
        outs[4].append(ps[:, kv_col:kv_col + 4 * gw].reshape(db, s, 4, g, hd))
        outs[5].append(win_new.reshape(db, wb, 2, g, hd))
        outs[6].append(ret_s)
        outs[7].append(jnp.swapaxes(tail_s[:, :ff].reshape(CONV_W - 1, db, ff), 0, 1))
        ys = ys_new

    y_prompt = yp.reshape(1, t, d)
    y_sample = jnp.swapaxes(ys.reshape(s, db, d), 0, 1)
    return (y_prompt, y_sample) + tuple(jnp.stack(o) for o in outs)
```

```python
import functools

import jax
import jax.numpy as jnp
import numpy as np
from jax import lax
from jax.experimental import pallas as pl
from jax.experimental.pallas import tpu as pltpu

F32 = jnp.float32
BF16 = jnp.bfloat16

EPS = 1e-6
NEG = -1e30
FORCE = 1e4
RET_HEADS = 8
RET_CHUNK = 128
NSA_HEADS = 16
NSA_KV = 4
NSA_REP = NSA_HEADS // NSA_KV
L_CMP = 32
STRIDE_CMP = 16
L_SEL = 64
N_SELECT = 16
WINDOW = 512
CONV_W = 3

LANE = 128
SUBLANE = 8
VMEM_LIMIT = 56 * 1024 * 1024


def _cp(*sem):
    return pltpu.CompilerParams(dimension_semantics=sem, vmem_limit_bytes=VMEM_LIMIT)


def _silu(x):
    return x * jax.nn.sigmoid(x)


def _nt_dot(a, b):
    return lax.dot_general(a, b, (((1,), (1,)), ((), ())), preferred_element_type=F32)


def _tn_dot(a, b):
    return lax.dot_general(a, b, (((0,), (0,)), ((), ())), preferred_element_type=F32)


def _dot(a, b):
    return jnp.dot(a, b, preferred_element_type=F32)


def _ada_kernel(c_ref, w_ref, b_ref, o_ref):
    sc = _silu(c_ref[...]).astype(BF16)
    o_ref[...] = _dot(sc, w_ref[...].astype(BF16)) + b_ref[...]


def ada_mods(c_all, w_ada, b_ada, tn=512):
    m, d = c_all.shape
    n = w_ada.shape[1]
    return pl.pallas_call(
        _ada_kernel,
        out_shape=jax.ShapeDtypeStruct((m, n), F32),
        grid=(n // tn,),
        in_specs=[pl.BlockSpec((m, d), lambda j: (0, 0)),
                  pl.BlockSpec((d, tn), lambda j: (0, j)),
                  pl.BlockSpec((1, tn), lambda j: (0, j))],
        out_specs=pl.BlockSpec((m, tn), lambda j: (0, j)),
        compiler_params=_cp("arbitrary"),
        name="ada_mods",
    )(c_all, w_ada, b_ada.reshape(1, n))


def _rows(ref):
    v = ref[...]
    return v[0:1, :] if v.shape[0] == SUBLANE else v


def _rms(x, g):
    return x * lax.rsqrt(jnp.mean(x * x, axis=-1, keepdims=True) + EPS) * g


def _norm_mod_kernel(x_ref, g_ref, shift_ref, scale_ref, o_ref):
    y = _rms(x_ref[...], g_ref[...])
    o_ref[...] = (y * (1.0 + _rows(scale_ref)) + _rows(shift_ref)).astype(o_ref.dtype)


def _mod_spec(mods, tm, k, d):
    r = mods.shape[0]
    assert (r == SUBLANE) != (r == tm)
    return pl.BlockSpec((r, d), lambda i, k=k: (0, k))


def norm_mod(x, g, mods, k_shift, k_scale, tm):
    m, d = x.shape
    row = pl.BlockSpec((tm, d), lambda i: (i, 0))
    return pl.pallas_call(
        _norm_mod_kernel,
        out_shape=jax.ShapeDtypeStruct((m, d), BF16),
        grid=(m // tm,),
        in_specs=[row, pl.BlockSpec((1, d), lambda i: (0, 0)),
                  _mod_spec(mods, tm, k_shift, d), _mod_spec(mods, tm, k_scale, d)],
        out_specs=row,
        compiler_params=_cp("parallel"),
        name="norm_mod",
    )(x, g.reshape(1, d), mods, mods)


def _post_mix_kernel(f_ref, x_ref, g1_ref, g2_ref, gate_ref, shift_ref, scale_ref, x2_ref, h2_ref):
    x2 = x_ref[...] + _rows(gate_ref) * _rms(f_ref[...], g1_ref[...])
    x2_ref[...] = x2
    h2_ref[...] = (_rms(x2, g2_ref[...]) * (1.0 + _rows(scale_ref)) + _rows(shift_ref)).astype(h2_ref.dtype)


def post_mix(f, x, g_post_mix, g_pre_ffn, mods, tm):
    m, d = x.shape
    row = pl.BlockSpec((tm, d), lambda i: (i, 0))
    vec = pl.BlockSpec((1, d), lambda i: (0, 0))
    return pl.pallas_call(
        _post_mix_kernel,
        out_shape=(jax.ShapeDtypeStruct((m, d), F32), jax.ShapeDtypeStruct((m, d), BF16)),
        grid=(m // tm,),
        in_specs=[row, row, vec, vec, _mod_spec(mods, tm, 2, d), _mod_spec(mods, tm, 3, d),
                  _mod_spec(mods, tm, 4, d)],
        out_specs=(row, row),
        compiler_params=_cp("parallel"),
        name="post_mix",
    )(f, x, g_post_mix.reshape(1, d), g_pre_ffn.reshape(1, d), mods, mods, mods)


def _post_ffn_kernel(f_ref, x_ref, g_ref, gate_ref, y_ref):
    y_ref[...] = x_ref[...] + _rows(gate_ref) * _rms(f_ref[...], g_ref[...])


def post_ffn(f, x2, g_post_ffn, mods, tm):
    m, d = x2.shape
    row = pl.BlockSpec((tm, d), lambda i: (i, 0))
    return pl.pallas_call(
        _post_ffn_kernel,
        out_shape=jax.ShapeDtypeStruct((m, d), F32),
        grid=(m // tm,),
        in_specs=[row, row, pl.BlockSpec((1, d), lambda i: (0, 0)), _mod_spec(mods, tm, 5, d)],
        out_specs=row,
        compiler_params=_cp("parallel"),
        name="post_ffn",
    )(f, x2, g_post_ffn.reshape(1, d), mods)


def _mm_kernel(a_ref, w_ref, o_ref, *, nk):
    acc = _dot(a_ref[...], w_ref[...])
    if nk == 1:
        o_ref[...] = acc
    else:
        k = pl.program_id(2)

        @pl.when(k == 0)
        def _():
            o_ref[...] = acc

        @pl.when(k > 0)
        def _():
            o_ref[...] += acc


def matmul(a, w, tm, tn, tk=None):
    m, kd = a.shape
    n = w.shape[1]
    tk = kd if tk is None else tk
    nk = kd // tk
    assert m % tm == 0 and n % tn == 0 and kd % tk == 0
    return pl.pallas_call(
        functools.partial(_mm_kernel, nk=nk),
        out_shape=jax.ShapeDtypeStruct((m, n), F32),
        grid=(m // tm, n // tn, nk),
        in_specs=[pl.BlockSpec((tm, tk), lambda i, j, k: (i, k)),
                  pl.BlockSpec((tk, tn), lambda i, j, k: (k, j))],
        out_specs=pl.BlockSpec((tm, tn), lambda i, j, k: (i, j)),
        compiler_params=_cp("parallel", "parallel", "arbitrary"),
        name="matmul",
    )(a, w)


def _ret_out(o, gate, gn):
    mu = jnp.mean(o, axis=-1, keepdims=True)
    var = jnp.mean(jnp.square(o - mu), axis=-1, keepdims=True)
    return (o - mu) * lax.rsqrt(var + EPS) * gn * _silu(gate)


def _ret_prompt_kernel(q_ref, k_ref, v_ref, gate_ref, dmat_ref, dq_ref, dk_ref, gc_ref, gn_ref,
                       o_ref, st_ref, *, dk_scale):
    @pl.when(pl.program_id(1) == 0)
    def _():
        st_ref[...] = jnp.zeros_like(st_ref)

    k = k_ref[...] * dk_scale
    qb = q_ref[...].astype(BF16)
    vb = v_ref[...].astype(BF16)
    attn = _nt_dot(qb, k.astype(BF16)) * dmat_ref[...]
    st = st_ref[...]
    o = _dot(attn.astype(BF16), vb) + _dot(qb, st.astype(BF16)) * dq_ref[...]
    kdec = (k * dk_ref[...]).astype(BF16)
    st_ref[...] = gc_ref[...] * st + _tn_dot(kdec, vb)
    o_ref[...] = _ret_out(o, gate_ref[...], gn_ref[...]).astype(o_ref.dtype)


def _ret_tables(c):
    log_g = jnp.log(1.0 - jnp.exp2(-5.0 - jnp.arange(RET_HEADS, dtype=F32)))
    i = jnp.arange(c, dtype=F32)
    diff = i[:, None] - i[None, :]
    dmat = jnp.where(diff >= 0, jnp.exp(log_g[:, None, None] * jnp.maximum(diff, 0.0)), 0.0)
    dq = jnp.exp(log_g[:, None] * (i[None, :] + 1.0))[:, :, None]
    dk = jnp.exp(log_g[:, None] * (c - 1.0 - i[None, :]))[:, :, None]
    gc = jnp.exp(log_g * c)[:, None, None]
    return dmat, dq, dk, gc


def retention_prompt(p, ret_gn, dkv):
    t = p.shape[0]
    c = RET_CHUNK
    h = RET_HEADS
    dmat, dq, dk, gc = _ret_tables(c)

    def col(part):
        return pl.BlockSpec((c, dkv), lambda hh, cc, part=part: (cc, part * h + hh))

    def tab(shape):
        return pl.BlockSpec((None,) + shape, lambda hh, cc: (hh, 0, 0))

    return pl.pallas_call(
        functools.partial(_ret_prompt_kernel, dk_scale=dkv ** -0.5),
        out_shape=(jax.ShapeDtypeStruct((t, h * dkv), BF16), jax.ShapeDtypeStruct((h, dkv, dkv), F32)),
        grid=(h, t // c),
        in_specs=[col(0), col(1), col(2), col(3), tab((c, c)), tab((c, 1)), tab((c, 1)), tab((1, 1)),
                  pl.BlockSpec((1, dkv), lambda hh, cc: (0, hh))],
        out_specs=(pl.BlockSpec((c, dkv), lambda hh, cc: (cc, hh)),
                   pl.BlockSpec((None, dkv, dkv), lambda hh, cc: (hh, 0, 0))),
        compiler_params=_cp("parallel", "arbitrary"),
        name="retention_prompt",
    )(p, p, p, p, dmat, dq, dk, gc, ret_gn.reshape(1, h * dkv))


def _ret_sample_kernel(q_ref, k_ref, v_ref, gate_ref, st_ref, dmat_ref, dq_ref, dk_ref, gc_ref, gn_ref,
                       o_ref, nst_ref, *, dkv, s, dk_scale):
    rows = 2 * s
    rid = lax.broadcasted_iota(jnp.int32, (rows, 1), 0)
    first = rid < s
    for h in range(RET_HEADS):
        sl = slice(h * dkv, (h + 1) * dkv)
        q = q_ref[:, sl]
        k = k_ref[:, sl] * dk_scale
        v = v_ref[:, sl]
        attn = _nt_dot(q, k) * dmat_ref[h]
        o = _dot(attn, v)
        dq = dq_ref[h]
        kdec = k * dk_ref[h]
        for b in range(2):
            st = st_ref[b, h]
            mine = first if b == 0 else jnp.logical_not(first)
            o = o + jnp.where(mine, _dot(q, st) * dq, 0.0)
            nst_ref[b, h] = gc_ref[h] * st + _tn_dot(jnp.where(mine, kdec, 0.0), v)
        o_ref[:, sl] = _ret_out(o, gate_ref[:, sl], gn_ref[:, sl]).astype(o_ref.dtype)


def retention_sample(p, state, b0, ret_gn, s):
    _, h, dkv, _ = state.shape
    db = p.shape[0] // s
    rows = 2 * s
    assert rows == SUBLANE and db % 2 == 0
    log_g = jnp.log(1.0 - jnp.exp2(-5.0 - jnp.arange(h, dtype=F32)))
    i = jnp.arange(rows)
    tok = (i % s).astype(F32)
    same = (i[:, None] // s) == (i[None, :] // s)
    diff = tok[:, None] - tok[None, :]
    dmat = jnp.where(same & (diff >= 0), jnp.exp(log_g[:, None, None] * jnp.maximum(diff, 0.0)), 0.0)
    dq = jnp.exp(log_g[:, None] * (tok[None, :] + 1.0))[:, :, None]
    dk = jnp.exp(log_g[:, None] * (s - 1.0 - tok[None, :]))[:, :, None]
    gc = jnp.exp(log_g * s)[:, None, None]
    w = h * dkv

    def col(part):
        return pl.BlockSpec((rows, w), lambda i, part=part: (i, part))

    def full(a):
        return pl.BlockSpec(a.shape, lambda i: (0,) * a.ndim)

    assert b0 % 2 == 0
    st_block = (2, h, dkv, dkv)
    return pl.pallas_call(
        functools.partial(_ret_sample_kernel, dkv=dkv, s=s, dk_scale=dkv ** -0.5),
        out_shape=(jax.ShapeDtypeStruct((db * s, w), BF16), jax.ShapeDtypeStruct((db, h, dkv, dkv), F32)),
        grid=(db // 2,),
        in_specs=[col(0), col(1), col(2), col(3), pl.BlockSpec(st_block, lambda i: (b0 // 2 + i, 0, 0, 0)),
                  full(dmat), full(dq), full(dk), full(gc), pl.BlockSpec((1, w), lambda i: (0, 0))],
        out_specs=(pl.BlockSpec((rows, w), lambda i: (i, 0)), pl.BlockSpec(st_block, lambda i: (i, 0, 0, 0))),
        compiler_params=_cp("parallel"),
        name="retention_sample",
    )(p, p, p, p, state, dmat, dq, dk, gc, ret_gn.reshape(1, w))


def _compress_block(load_rows, pe_ref, w1_ref, w2, accb_ref, nh):
    pairs = L_CMP // 4
    acc_a = None
    acc_b = None
    for i in range(pairs):
        y = jnp.concatenate([load_rows(2 * i), load_rows(2 * i + 1)], axis=1)
        da = _dot((y + pe_ref[i:i + 1, :]).astype(BF16), w1_ref[i])
        db = _dot((y + pe_ref[pairs + i:pairs + i + 1, :]).astype(BF16), w1_ref[pairs + i])
        acc_a = da if acc_a is None else acc_a + da
        acc_b = db if acc_b is None else acc_b + db
    accb_ref[0:nh, :] = acc_b
    accb_ref[nh:nh + SUBLANE, :] = jnp.zeros((SUBLANE, acc_b.shape[1]), F32)
    h = jax.nn.gelu(acc_a + accb_ref[pl.ds(1, nh), :])
    return _dot(h.astype(BF16), w2)


def _masked_softmax(s, mask):
    s = jnp.where(mask, s, NEG)
    e = jnp.exp(s - jnp.max(s, axis=-1, keepdims=True))
    return jnp.where(mask, e / jnp.sum(e, axis=-1, keepdims=True), 0.0)


def _importance(psum, ov):
    hi = psum.astype(BF16)
    lo = (psum - hi.astype(F32)).astype(BF16)
    return _dot(hi, ov) + _dot(lo, ov)


def _select_blocks_t(imp_t, q_pos, n_sel):
    nb = imp_t.shape[0]
    blk = lax.broadcasted_iota(jnp.int32, (nb, 1), 0)
    blkf = blk.astype(F32)
    cur = q_pos // L_SEL
    forced = (blk == 0) | (blk == cur) | (blk == cur - 1)
    score = jnp.where(blk > cur, -1.0, jnp.where(forced, FORCE, imp_t))
    score = jnp.where(blk >= n_sel, -2.0, score)
    taken = -3.0
    work = score
    for _ in range(min(N_SELECT, n_sel)):
        m = jnp.max(work, axis=0, keepdims=True)
        first = jnp.min(jnp.where(work == m, blkf, float(nb)), axis=0, keepdims=True)
        work = jnp.where(blkf == first, taken, work)
    return jnp.where((work == taken) & (score >= 0.0), 1.0, 0.0)


def _overlap_matrix(nc, nb):
    n = jnp.arange(nc)[:, None]
    j = jnp.arange(nb)[None, :]
    c_start = n * STRIDE_CMP
    c_end = c_start + L_CMP - 1
    return ((c_start < (j + 1) * L_SEL) & (c_end >= j * L_SEL)).astype(BF16)


def _expand_matrix(nb, nkeys):
    return (jnp.arange(nb)[:, None] == (jnp.arange(nkeys)[None, :] // L_SEL)).astype(BF16)


def _slopes():
    h = jnp.arange(1, NSA_HEADS + 1, dtype=F32)
    return jnp.exp2(-8.0 * h / NSA_HEADS).reshape(NSA_KV, NSA_REP)


def _compress_prompt_kernel(x_ref, pe_ref, w1_ref, w2_ref, o_ref, accb_ref):
    nh = x_ref.shape[0] // STRIDE_CMP
    o_ref[...] = _compress_block(lambda l: x_ref[pl.ds(l, nh, stride=STRIDE_CMP), :],
                                 pe_ref, w1_ref, w2_ref[...], accb_ref, nh)


def compress_prompt(p, kv_col, cmp_pe, w1b, w2b, hd):
    t = p.shape[0]
    nh = t // STRIDE_CMP
    hid = w1b.shape[-1]
    cb = kv_col // hd
    return pl.pallas_call(
        _compress_prompt_kernel,
        out_shape=jax.ShapeDtypeStruct((2, NSA_KV, nh, hd), F32),
        grid=(2, NSA_KV),
        in_specs=[pl.BlockSpec((t, hd), lambda kind, g: (0, cb + kind * NSA_KV + g)),
                  pl.BlockSpec((None, L_CMP // 2, 2 * hd), lambda kind, g: (kind, 0, 0)),
                  pl.BlockSpec((None, L_CMP // 2, 2 * hd, hid), lambda kind, g: (kind, 0, 0, 0)),
                  pl.BlockSpec((None, hid, hd), lambda kind, g: (kind, 0, 0))],
        out_specs=pl.BlockSpec((None, None, nh, hd), lambda kind, g: (kind, g, 0, 0)),
        scratch_shapes=[pltpu.VMEM((nh + SUBLANE, hid), F32)],
        compiler_params=_cp("parallel", "parallel"),
        name="compress_prompt",
    )(p, cmp_pe, w1b, w2b)


MASK_BIG = 2.0 ** 99
LOG2E = 1.4426950408889634


def _cmp_win_select_kernel(q_ref, kc_ref, vc_ref, ov_ref, sl_ref, gate_ref, *rest, tq, hd, n_sel, n_prev, scale):
    kw_refs = rest[:n_prev + 1]
    vw_refs = rest[n_prev + 1:2 * n_prev + 2]
    part_ref, unsel_ref = rest[2 * n_prev + 2:]
    qt = pl.program_id(1)
    nc = kc_ref.shape[0]
    nw = (n_prev + 1) * tq
    q_pos = qt * tq + lax.broadcasted_iota(jnp.int32, (tq, 1), 0)
    dist_c = q_pos - (lax.broadcasted_iota(jnp.int32, (1, nc), 1) * STRIDE_CMP + (L_CMP - 1))
    mask_c = dist_c >= 0
    dist_cf = dist_c.astype(F32)
    kcb = kc_ref[...].astype(BF16)
    vcb = vc_ref[...].astype(BF16)
    w_pos = (qt - n_prev) * tq + lax.broadcasted_iota(jnp.int32, (1, nw), 1)
    dist_w = q_pos - w_pos
    mask_w = (dist_w >= 0) & (dist_w <= WINDOW) & (w_pos >= 0)
    dist_wf = dist_w.astype(F32)
    kwb = jnp.concatenate([r[...].astype(BF16) for r in kw_refs], axis=0)
    vwb = jnp.concatenate([r[...].astype(BF16) for r in vw_refs], axis=0)
    gate = jax.nn.sigmoid(gate_ref[...])
    psum = jnp.zeros((tq, nc), F32)
    for r in range(NSA_REP):
        sl = slice(r * hd, (r + 1) * hd)
        qb = q_ref[:, sl].astype(BF16)
        slope = sl_ref[r][:, 0:1]
        p_c = _masked_softmax(_nt_dot(qb, kcb) * scale - slope * dist_cf, mask_c)
        psum = psum + p_c
        p_w = _masked_softmax(_nt_dot(qb, kwb) * scale - slope * dist_wf, mask_w)
        part_ref[:, sl] = (gate[:, 3 * r:3 * r + 1] * _dot(p_c.astype(BF16), vcb)
                           + gate[:, 3 * r + 2:3 * r + 3] * _dot(p_w.astype(BF16), vwb))
    q_pos_row = qt * tq + lax.broadcasted_iota(jnp.int32, (1, tq), 1)
    picked_t = _select_blocks_t(_importance(psum, ov_ref[...]).T, q_pos_row, n_sel)
    unsel_ref[...] = ((picked_t.T - 1.0) * MASK_BIG).astype(unsel_ref.dtype)


def cmp_win_select_prompt(p, q_col, kv_col, kcvc, gates, tq=256):
    t = p.shape[0]
    nc, hd = kcvc.shape[2:]
    n_sel = -(-t // L_SEL)
    tq = min(tq, t)
    assert N_SELECT <= n_sel <= LANE and WINDOW % tq == 0 and t % tq == 0
    n_prev = WINDOW // tq
    w = NSA_REP * hd
    cb = kv_col // hd
    slopes = jnp.broadcast_to(_slopes()[:, :, None, None], (NSA_KV, NSA_REP, 1, LANE))

    def win_specs(kind):
        return [pl.BlockSpec((tq, hd), lambda g, i, j=j: (jnp.maximum(i - j, 0), cb + kind * NSA_KV + g))
                for j in range(n_prev, -1, -1)]

    return pl.pallas_call(
        functools.partial(_cmp_win_select_kernel, tq=tq, hd=hd, n_sel=n_sel, n_prev=n_prev, scale=hd ** -0.5),
        out_shape=(jax.ShapeDtypeStruct((t, NSA_KV * w), F32), jax.ShapeDtypeStruct((t, NSA_KV * LANE), BF16)),
        grid=(NSA_KV, t // tq),
        in_specs=[pl.BlockSpec((tq, w), lambda g, i: (i, q_col // w + g)),
                  pl.BlockSpec((None, None, nc, hd), lambda g, i: (0, g, 0, 0)),
                  pl.BlockSpec((None, None, nc, hd), lambda g, i: (1, g, 0, 0)),
                  pl.BlockSpec((nc, LANE), lambda g, i: (0, 0)),
                  pl.BlockSpec((None, NSA_REP, 1, LANE), lambda g, i: (g, 0, 0, 0)),
                  pl.BlockSpec((None, tq, 3 * NSA_REP), lambda g, i: (g, i, 0))]
        + win_specs(4) + win_specs(5),
        out_specs=(pl.BlockSpec((tq, w), lambda g, i: (i, g)),
                   pl.BlockSpec((tq, LANE), lambda g, i: (i, g))),
        compiler_params=_cp("parallel", "parallel"),
        name="cmp_win_select_prompt",
    )(p, kcvc, kcvc, _overlap_matrix(nc, LANE), slopes, gates, *([p] * (2 * n_prev + 2)))


def _slc_kernel(qi_ref, ki_ref, q_ref, unsel_ref, al_ref, k_ref, v_ref, kf_ref, gate_ref, part_ref,
                o_ref, qa_ref, m_ref, l_ref, acc_ref, *, tq, tk, hd, scale):
    step = pl.program_id(1)
    qt = qi_ref[step]
    kt = ki_ref[step]

    @pl.when(kt == 0)
    def _():
        q_pos = (qt * tq + lax.broadcasted_iota(jnp.int32, (tq, 1), 0)).astype(F32)
        lane = lax.broadcasted_iota(jnp.int32, (1, LANE), 1)
        unsel = unsel_ref[...]
        for r in range(NSA_REP):
            rows = slice(r * tq, (r + 1) * tq)
            qa_ref[rows, 0:hd] = (q_ref[:, r * hd:(r + 1) * hd] * (scale * LOG2E)).astype(BF16)
            qa_ref[rows, hd:hd + LANE] = unsel
            al = al_ref[r]
            qa_ref[rows, hd + LANE:hd + 2 * LANE] = jnp.where(
                lane == ALIBI_ROWCONST, -al[:, ALIBI_SLOPE:ALIBI_SLOPE + 1] * q_pos, al).astype(BF16)
        m_ref[...] = jnp.full(m_ref.shape, M_FLOOR, F32)
        l_ref[...] = jnp.zeros(l_ref.shape, F32)
        acc_ref[...] = jnp.zeros(acc_ref.shape, F32)

    ka = jnp.concatenate([k_ref[...].astype(BF16), kf_ref[...]], axis=1)
    vb = v_ref[...].astype(BF16)
    last = (qt * tq + tq - 1) // tk

    def update(r, s):
        rows = slice(r * tq, (r + 1) * tq)
        m_old = m_ref[rows, :]
        m_new = jnp.maximum(m_old, jnp.max(s, axis=-1, keepdims=True))
        alpha = jnp.exp2(m_old - m_new)
        e = jnp.exp2(s - m_new)
        l_ref[rows, :] = alpha * l_ref[rows, :] + jnp.sum(e, axis=-1, keepdims=True)
        acc_ref[rows, :] = alpha * acc_ref[rows, :] + _dot(e.astype(BF16), vb)
        m_ref[rows, :] = m_new

    @pl.when(kt != last)
    def _():
        for r in range(NSA_REP):
            update(r, _nt_dot(qa_ref[r * tq:(r + 1) * tq, :], ka))

    @pl.when(kt == last)
    def _():
        causal = (qt * tq + lax.broadcasted_iota(jnp.int32, (tq, 1), 0)
                  >= kt * tk + lax.broadcasted_iota(jnp.int32, (1, tk), 1))
        gate = jax.nn.sigmoid(gate_ref[...])
        for r in range(NSA_REP):
            update(r, jnp.where(causal, _nt_dot(qa_ref[r * tq:(r + 1) * tq, :], ka), NEG))
            rows = slice(r * tq, (r + 1) * tq)
            sl = slice(r * hd, (r + 1) * hd)
            out = part_ref[:, sl] + gate[:, 3 * r + 1:3 * r + 2] * (acc_ref[rows, :] / l_ref[rows, :])
            o_ref[:, sl] = out.astype(o_ref.dtype)


ALIBI_SLOPE = 8
ALIBI_ROWCONST = 6
M_FLOOR = -1e29


def _alibi_tables(t):
    slope = _slopes() * LOG2E
    s1 = slope.astype(BF16).astype(F32)
    s2 = (slope - s1).astype(BF16).astype(F32)
    s3 = (slope - s1 - s2).astype(BF16).astype(F32)
    zero = jnp.zeros_like(slope)
    lanes = [s1, s2, s3, s1, s2, s3, zero, zero, slope] + [zero] * (LANE - 9)
    qf = jnp.stack(lanes, axis=-1)[:, :, None, :]
    pos = jnp.arange(t)
    hi = (pos // L_SEL * L_SEL).astype(F32)[:, None]
    lo = (pos % L_SEL).astype(F32)[:, None]
    one = jnp.ones((t, 1), F32)
    kf = jnp.concatenate([hi, hi, hi, lo, lo, lo, one, jnp.zeros((t, LANE - 7), F32)], axis=1).astype(BF16)
    return qf, kf


def slc_prompt(p, q_col, kv_col, gates, part, unsel, tq=256, tk=512):
    t = p.shape[0]
    hd = part.shape[1] // NSA_HEADS
    w = NSA_REP * hd
    tq, tk = min(tq, t), min(tk, t)
    assert t % tq == 0 and t % tk == 0 and tk % tq == 0 and tk % L_SEL == 0 and hd == LANE
    qi, ki = [], []
    for i in range(t // tq):
        for k in range((i * tq + tq - 1) // tk + 1):
            qi.append(i)
            ki.append(k)
    qi, ki = jnp.asarray(qi, jnp.int32), jnp.asarray(ki, jnp.int32)
    cb = kv_col // hd
    al_q, al_k = _alibi_tables(t)
    kfeat = jnp.concatenate([_expand_matrix(LANE, t).T, al_k], axis=1)

    def kv_spec(kind):
        return pl.BlockSpec((tk, hd), lambda g, s, qi, ki: (ki[s], cb + kind * NSA_KV + g))

    qrow = lambda g, s, qi, ki: (qi[s], g)
    return pl.pallas_call(
        functools.partial(_slc_kernel, tq=tq, tk=tk, hd=hd, scale=hd ** -0.5),
        out_shape=jax.ShapeDtypeStruct((t, NSA_KV * w), BF16),
        grid_spec=pltpu.PrefetchScalarGridSpec(
            num_scalar_prefetch=2,
            grid=(NSA_KV, int(qi.shape[0])),
            in_specs=[pl.BlockSpec((tq, w), lambda g, s, qi, ki: (qi[s], q_col // w + g)),
                      pl.BlockSpec((tq, LANE), qrow),
                      pl.BlockSpec((None, NSA_REP, 1, LANE), lambda g, s, qi, ki: (g, 0, 0, 0)),
                      kv_spec(2), kv_spec(3),
                      pl.BlockSpec((tk, 2 * LANE), lambda g, s, qi, ki: (ki[s], 0)),
                      pl.BlockSpec((None, tq, 3 * NSA_REP), lambda g, s, qi, ki: (g, qi[s], 0)),
                      pl.BlockSpec((tq, w), qrow)],
            out_specs=pl.BlockSpec((tq, w), qrow),
            scratch_shapes=[pltpu.VMEM((NSA_REP * tq, hd + 2 * LANE), BF16),
                            pltpu.VMEM((NSA_REP * tq, 1), F32), pltpu.VMEM((NSA_REP * tq, 1), F32),
                            pltpu.VMEM((NSA_REP * tq, hd), F32)]),
        compiler_params=_cp("parallel", "arbitrary"),
        name="slc_prompt",
    )(qi, ki, p, unsel, al_q, p, p, kfeat, gates, part)


KV_ROWS = 4 * NSA_KV
WIN_ROWS = 2 * NSA_KV


def _page_specs(n_pages, rows, hd, page0):
    return [pl.BlockSpec((None, rows, hd), lambda b, pt, j=j: (page0 + pt[b * n_pages + j], 0, 0))
            for j in range(n_pages)]


def _head_rows(ref, kind, g, n, rows_per_pos):
    return ref[pl.ds(kind * NSA_KV + g, n, stride=rows_per_pos), :]


def _compress_sample_kernel(pt_ref, *refs, n_pages, hd):
    page_refs = refs[:n_pages]
    pe_ref, w1_ref, w2_ref, o_ref, accb_ref, x_ref = refs[n_pages:]
    page = page_refs[0].shape[0] // KV_ROWS
    nh = n_pages * page // STRIDE_CMP
    for kind in range(2):
        for g in range(NSA_KV):
            for j, pr in enumerate(page_refs):
                x_ref[j * page:(j + 1) * page, :] = _head_rows(pr, kind, g, page, KV_ROWS)
            o_ref[kind * NSA_KV + g] = _compress_block(
                lambda l: x_ref[pl.ds(l, nh, stride=STRIDE_CMP), :],
                pe_ref.at[kind], w1_ref.at[kind], w2_ref[kind], accb_ref, nh)


def compress_sample(cache3, page0, page_table, cmp_pe, w1b, w2b, hd):
    db, n_pages = page_table.shape
    page = cache3.shape[1] // KV_ROWS
    nh = n_pages * page // STRIDE_CMP
    hid = w1b.shape[-1]
    full = lambda a: pl.BlockSpec(a.shape, lambda b, pt: (0,) * a.ndim)
    return pl.pallas_call(
        functools.partial(_compress_sample_kernel, n_pages=n_pages, hd=hd),
        out_shape=jax.ShapeDtypeStruct((db, 2 * NSA_KV, nh, hd), F32),
        grid_spec=pltpu.PrefetchScalarGridSpec(
            num_scalar_prefetch=1,
            grid=(db,),
            in_specs=_page_specs(n_pages, page * KV_ROWS, hd, page0) + [full(cmp_pe), full(w1b), full(w2b)],
            out_specs=pl.BlockSpec((None, 2 * NSA_KV, nh, hd), lambda b, pt: (b, 0, 0, 0)),
            scratch_shapes=[pltpu.VMEM((nh + SUBLANE, hid), F32), pltpu.VMEM((n_pages * page, hd), F32)]),
        compiler_params=_cp("parallel"),
        name="compress_sample",
    )(page_table.reshape(-1), *([cache3] * n_pages), cmp_pe, w1b, w2b)


def _softmax2(s_a, mask_a, s_b, mask_b):
    s_a = jnp.where(mask_a, s_a, NEG)
    s_b = jnp.where(mask_b, s_b, NEG)
    m = jnp.maximum(jnp.max(s_a, axis=-1, keepdims=True), jnp.max(s_b, axis=-1, keepdims=True))
    e_a = jnp.where(mask_a, jnp.exp(s_a - m), 0.0)
    e_b = jnp.where(mask_b, jnp.exp(s_b - m), 0.0)
    return e_a, e_b, jnp.sum(e_a, axis=-1, keepdims=True) + jnp.sum(e_b, axis=-1, keepdims=True)


def _nsa_sample_kernel(pt_ref, *refs, n_pages, hd, s_new, scale):
    page_refs = refs[:n_pages]
    (q_ref, gate_ref, kcvc_ref, new_ref, nw_ref, win_ref, ov_ref, e_ref, rm_ref, sl_ref, o_ref, wn_ref,
     imp_ref) = refs[n_pages:]
    page = page_refs[0].shape[0] // KV_ROWS
    past = n_pages * page
    wb = win_ref.shape[0] // WIN_ROWS
    gw = NSA_KV * hd
    rows = NSA_REP * s_new
    nc = kcvc_ref.shape[1]
    n_cmp = (past + s_new - L_CMP) // STRIDE_CMP + 1
    n_sel = -(-(past + s_new) // L_SEL)
    tok = lax.broadcasted_iota(jnp.int32, (rows, 1), 0) % s_new
    q_pos = past + tok
    tnew = lax.broadcasted_iota(jnp.int32, (1, SUBLANE), 1)
    dist_new = tok - tnew
    mask_new = (dist_new >= 0) & (tnew < s_new)
    n = lax.broadcasted_iota(jnp.int32, (1, nc), 1)
    dist_c = q_pos - (n * STRIDE_CMP + (L_CMP - 1))
    mask_c = (dist_c >= 0) & (n < n_cmp)
    imp_ref[...] = jnp.zeros(imp_ref.shape, F32)
    o_cs = []
    for g in range(NSA_KV):
        s_c = (_nt_dot(q_ref[g].astype(BF16), kcvc_ref[g].astype(BF16)) * scale
               - sl_ref[g] * dist_c.astype(F32))
        p_c = _masked_softmax(s_c, mask_c)
        o_cs.append(_dot(p_c.astype(BF16), kcvc_ref[NSA_KV + g].astype(BF16)))
        imp_ref[g * rows:(g + 1) * rows, :] = _importance(_dot(rm_ref[...], p_c), ov_ref[...])
    lane_pos = past + lax.broadcasted_iota(jnp.int32, (1, LANE), 1) % s_new
    picked_all = _select_blocks_t(imp_ref[...].T, lane_pos, n_sel).T
    for g in range(NSA_KV):
        qf = q_ref[g]
        q = qf.astype(BF16)
        slope = sl_ref[g]
        o_c = o_cs[g]
        picked = picked_all[g * rows:(g + 1) * rows, :]
        key_mask = _dot(picked.astype(BF16), e_ref[...]) > 0.5
        ks = jnp.concatenate([_head_rows(pr, 2, g, page, KV_ROWS).astype(BF16) for pr in page_refs], axis=0)
        vs = jnp.concatenate([_head_rows(pr, 3, g, page, KV_ROWS).astype(BF16) for pr in page_refs], axis=0)
        dist_s = q_pos - lax.broadcasted_iota(jnp.int32, (1, past), 1)
        s_s = _nt_dot(q, ks) * scale - slope * dist_s.astype(F32)
        k_new = new_ref[:, 2 * gw + g * hd:2 * gw + (g + 1) * hd]
        v_new = new_ref[:, 3 * gw + g * hd:3 * gw + (g + 1) * hd]
        s_n = _nt_dot(qf, k_new) * scale - slope * dist_new.astype(F32)
        cur_picked = picked[:, past // L_SEL:past // L_SEL + 1] > 0.5
        e_s, e_n, l_s = _softmax2(s_s, key_mask & (dist_s >= 0), s_n, mask_new & cur_picked)
        o_s = (_dot(e_s.astype(BF16), vs) + _dot(e_n, v_new)) / l_s
        dist_w = wb + tok - lax.broadcasted_iota(jnp.int32, (1, wb), 1)
        s_w = (_nt_dot(q, _head_rows(win_ref, 0, g, wb, WIN_ROWS).astype(BF16)) * scale
               - slope * dist_w.astype(F32))
        kw_new = new_ref[:, 4 * gw + g * hd:4 * gw + (g + 1) * hd]
        vw_new = new_ref[:, 5 * gw + g * hd:5 * gw + (g + 1) * hd]
        s_wn = _nt_dot(qf, kw_new) * scale - slope * dist_new.astype(F32)
        e_w, e_wn, l_w = _softmax2(s_w, (dist_w >= 0) & (dist_w <= WINDOW), s_wn, mask_new)
        o_w = (_dot(e_w.astype(BF16), _head_rows(win_ref, 1, g, wb, WIN_ROWS).astype(BF16))
               + _dot(e_wn, vw_new)) / l_w
        gate = jax.nn.sigmoid(gate_ref[g])
        o_ref[g] = (gate[:, 0:1] * o_c + gate[:, 1:2] * o_s + gate[:, 2:3] * o_w).astype(o_ref.dtype)
    keep = (wb - s_new) * WIN_ROWS
    wn_ref[0:keep, :] = win_ref[s_new * WIN_ROWS:wb * WIN_ROWS, :]
    wn_ref[keep:wb * WIN_ROWS, :] = nw_ref[...]


def nsa_sample(cache3, page0, page_table, q, gates, kcvc, kv_new, win_new, win, win0, s_new):
    db, n_pages = page_table.shape
    page = cache3.shape[1] // KV_ROWS
    past = n_pages * page
    hd = q.shape[-1]
    rows = q.shape[2]
    nc = kcvc.shape[2]
    wb = win.shape[1] // WIN_ROWS
    assert wb <= past and wb <= WINDOW and N_SELECT <= -(-(past + s_new) // L_SEL) <= LANE
    assert past % L_SEL == 0 and s_new <= SUBLANE and s_new <= L_SEL and NSA_KV * rows <= LANE
    i = jnp.arange(rows)
    rm = (i[:, None] % s_new == i[None, :] % s_new).astype(F32)
    slopes = jnp.repeat(_slopes(), s_new, axis=1)[:, :, None]
    ov = _overlap_matrix(nc, LANE)
    em = _expand_matrix(LANE, past)
    per_b = lambda a: pl.BlockSpec((None,) + a.shape[1:], lambda b, pt: (b,) + (0,) * (a.ndim - 1))
    full = lambda a: pl.BlockSpec(a.shape, lambda b, pt: (0,) * a.ndim)
    return pl.pallas_call(
        functools.partial(_nsa_sample_kernel, n_pages=n_pages, hd=hd, s_new=s_new, scale=hd ** -0.5),
        out_shape=(jax.ShapeDtypeStruct(q.shape, BF16), jax.ShapeDtypeStruct((db,) + win.shape[1:], F32)),
        grid_spec=pltpu.PrefetchScalarGridSpec(
            num_scalar_prefetch=1,
            grid=(db,),
            in_specs=_page_specs(n_pages, page * KV_ROWS, hd, page0)
            + [per_b(q), per_b(gates), per_b(kcvc), per_b(kv_new), per_b(win_new),
               pl.BlockSpec((None,) + win.shape[1:], lambda b, pt: (win0 + b, 0, 0)),
               full(ov), full(em), full(rm), full(slopes)],
            out_specs=(per_b(q), per_b(win)),
            scratch_shapes=[pltpu.VMEM((LANE, LANE), F32)]),
        compiler_params=_cp("parallel"),
        name="nsa_sample",
    )(page_table.reshape(-1), *([cache3] * n_pages), q, gates, kcvc, kv_new, win_new, win, ov, em, rm, slopes)


def _ffn_up_prompt_kernel(h_ref, halo_ref, wa_ref, wb_ref, cw_ref, cb_ref, u_ref, tail_ref, ext_ref, *, tm):
    wa = wa_ref[...]
    h = h_ref[...]
    a = _dot(h, wa)
    prev = _dot(halo_ref[...], wa)
    ext_ref[0:SUBLANE, :] = jnp.where(pl.program_id(0) > 0, prev, 0.0)
    ext_ref[SUBLANE:SUBLANE + tm, :] = a
    y = (cb_ref[...] + cw_ref[0:1, :] * ext_ref[pl.ds(SUBLANE - 2, tm), :]
         + cw_ref[1:2, :] * ext_ref[pl.ds(SUBLANE - 1, tm), :] + cw_ref[2:3, :] * a)
    u_ref[...] = (jax.nn.gelu(y) * _dot(h, wb_ref[...])).astype(u_ref.dtype)
    tail_ref[...] = a[tm - SUBLANE:tm, :]


def ffn_up_prompt(h2, w_up_b, conv_w, conv_b, tm, tn):
    m, d = h2.shape
    ff = w_up_b.shape[1] // 2
    nj = ff // tn
    assert CONV_W == 3 and m % tm == 0 and ff % tn == 0
    return pl.pallas_call(
        functools.partial(_ffn_up_prompt_kernel, tm=tm),
        out_shape=(jax.ShapeDtypeStruct((m, ff), BF16), jax.ShapeDtypeStruct((m // tm, SUBLANE, ff), F32)),
        grid=(m // tm, nj),
        in_specs=[pl.BlockSpec((tm, d), lambda i, j: (i, 0)),
                  pl.BlockSpec((SUBLANE, d), lambda i, j: (jnp.maximum(i * (tm // SUBLANE) - 1, 0), 0)),
                  pl.BlockSpec((d, tn), lambda i, j: (0, j)),
                  pl.BlockSpec((d, tn), lambda i, j: (0, nj + j)),
                  pl.BlockSpec((CONV_W, tn), lambda i, j: (0, j)),
                  pl.BlockSpec((1, tn), lambda i, j: (0, j))],
        out_specs=(pl.BlockSpec((tm, tn), lambda i, j: (i, j)),
                   pl.BlockSpec((None, SUBLANE, tn), lambda i, j: (i, 0, j))),
        scratch_shapes=[pltpu.VMEM((tm + SUBLANE, tn), F32)],
        compiler_params=_cp("parallel", "parallel"),
        name="ffn_up_prompt",
    )(h2, h2, w_up_b, w_up_b, conv_w, conv_b.reshape(1, ff))


def _ffn_up_sample_kernel(h_ref, prev_ref, wa_ref, wb_ref, cw_ref, cb_ref, u_ref, tail_ref, ext_ref, *, db, m):
    h = h_ref[...]
    a = _dot(h, wa_ref[...])
    ext_ref[0:2 * db, :] = prev_ref[...]
    ext_ref[2 * db:2 * db + m, :] = a
    y = (cb_ref[...] + cw_ref[0:1, :] * ext_ref[0:m, :] + cw_ref[1:2, :] * ext_ref[db:db + m, :]
         + cw_ref[2:3, :] * a)
    u_ref[...] = (jax.nn.gelu(y) * _dot(h, wb_ref[...])).astype(u_ref.dtype)
    tail_ref[...] = ext_ref[m:m + 2 * db, :]


def ffn_up_sample(h2, prev, w_up_b, conv_w, conv_b, db, tn):
    m, d = h2.shape
    ff = w_up_b.shape[1] // 2
    nj = ff // tn
    assert CONV_W == 3 and m >= 2 * db and db % SUBLANE == 0
    return pl.pallas_call(
        functools.partial(_ffn_up_sample_kernel, db=db, m=m),
        out_shape=(jax.ShapeDtypeStruct((m, ff), BF16), jax.ShapeDtypeStruct((2 * db, ff), F32)),
        grid=(nj,),
        in_specs=[pl.BlockSpec((m, d), lambda j: (0, 0)),
                  pl.BlockSpec((2 * db, tn), lambda j: (0, j)),
                  pl.BlockSpec((d, tn), lambda j: (0, j)),
                  pl.BlockSpec((d, tn), lambda j: (0, nj + j)),
                  pl.BlockSpec((CONV_W, tn), lambda j: (0, j)),
                  pl.BlockSpec((1, tn), lambda j: (0, j))],
        out_specs=(pl.BlockSpec((m, tn), lambda j: (0, j)), pl.BlockSpec((2 * db, tn), lambda j: (0, j))),
        scratch_shapes=[pltpu.VMEM((m + 2 * db, tn), F32)],
        compiler_params=_cp("parallel"),
        name="ffn_up_sample",
    )(h2, prev, w_up_b, w_up_b, conv_w, conv_b.reshape(1, ff))


TN_IN = 768
TN_FF = 512
TK_DOWN = 1024


def _round_up(n, m):
    return -(-n // m) * m


def _pad_cols(a, n):
    return jnp.pad(a, ((0, 0), (0, n - a.shape[1])))


def _post(x, mix, mods, tm_row, tm_mm, w_out_b, g_post_mix, g_pre_ffn, ffn_up, w_down_b, g_post_ffn):
    f = matmul(mix, w_out_b, tm_mm, 1024)
    x2, h2 = post_mix(f, x, g_post_mix, g_pre_ffn, mods, tm_row)
    u, tail = ffn_up(h2)
    f2 = matmul(u, w_down_b, tm_mm, 2048, TK_DOWN)
    return post_ffn(f2, x2, g_post_ffn, mods, tm_row), tail


def kernel(x_prompt, x_sample, cache_kv, cache_win, state_ret, state_conv, page_table, c_prompt, c_sample,
           w_ada, b_ada, g_pre_mix, w_in, cmp_pe, cmp_w1, cmp_w2, ret_gn, w_out, g_post_mix, g_pre_ffn,
           w_up, conv_w, conv_b, w_down, g_post_ffn):
    depth = w_in.shape[0]
    bp, t, d = x_prompt.shape
    db, s, _ = x_sample.shape
    assert bp == 1
    h, dkv = state_ret.shape[2], state_ret.shape[3]
    n_phys, page, n_rows, g, hd = cache_kv.shape[1:]
    assert h == RET_HEADS and g == NSA_KV and n_rows == 4
    wb = cache_win.shape[2]
    ff = w_up.shape[2] // 2
    rw = h * dkv
    q_col = 4 * rw
    kv_col = q_col + NSA_HEADS * hd
    ng_col = kv_col + 6 * g * hd
    n_gate = NSA_HEADS * 3
    np_ = _round_up(ng_col + n_gate, TN_IN)
    ffp = _round_up(ff, max(TN_FF, TK_DOWN))
    gw = g * hd
    tm_p = min(1024, t)
    tm_s = s * db

    yp = x_prompt.reshape(t, d)
    ys = jnp.swapaxes(x_sample, 0, 1).reshape(s * db, d)
    c_all = jnp.concatenate([c_prompt, jnp.zeros((SUBLANE - 1, d), F32), c_sample], axis=0)
    outs = [[] for _ in range(8)]
    for l in range(depth):
        w_in_b = _pad_cols(w_in[l], np_).astype(BF16)
        w_out_b = w_out[l].astype(BF16)
        w_up_b = jnp.concatenate([_pad_cols(w_up[l][:, :ff], ffp), _pad_cols(w_up[l][:, ff:], ffp)],
                                 axis=1).astype(BF16)
        w_down_b = jnp.pad(w_down[l], ((0, ffp - ff), (0, 0))).astype(BF16)
        conv_w_p = _pad_cols(conv_w[l], ffp)
        conv_b_p = jnp.pad(conv_b[l], (0, ffp - ff))
        pe2 = cmp_pe[l].reshape(2, L_CMP // 2, 2 * hd)
        w1b = cmp_w1[l].reshape(2, L_CMP // 2, 2 * hd, -1).astype(BF16)
        w2b = cmp_w2[l].astype(BF16)
        post_w = (w_out_b, g_post_mix[l], g_pre_ffn[l])

        mods = ada_mods(c_all, w_ada[l], b_ada[l])
        mods_p = mods[0:SUBLANE]
        mods_s = mods[SUBLANE:SUBLANE + db]

        hp = norm_mod(yp, g_pre_mix[l], mods_p, 0, 1, 256)
        p = matmul(hp, w_in_b, tm_p, TN_IN)
        mix_ret, ret_state = retention_prompt(p, ret_gn[l], dkv)
        kcvc = compress_prompt(p, kv_col, pe2, w1b, w2b, hd)
        gates = jnp.swapaxes(p[:, ng_col:ng_col + n_gate].reshape(t, g, 3 * NSA_REP), 0, 1)
        part, unsel = cmp_win_select_prompt(p, q_col, kv_col, kcvc, gates)
        nsa_o = slc_prompt(p, q_col, kv_col, gates, part, unsel)
        mix = jnp.concatenate([mix_ret, nsa_o], axis=1)
        yp_new, tails = _post(yp, mix, mods_p, 256, tm_p, *post_w,
                              lambda h2: ffn_up_prompt(h2, w_up_b, conv_w_p, conv_b_p, tm_p, TN_FF),
                              w_down_b, g_post_ffn[l])
        wp = min(WINDOW, t)
        outs[0].append(p[:, kv_col:kv_col + 4 * gw].reshape(1, t, 4, g, hd))
        outs[1].append(p[t - wp:, kv_col + 4 * gw:kv_col + 6 * gw].reshape(1, wp, 2, g, hd))
        outs[2].append(ret_state[None])
        outs[3].append(tails[-1, SUBLANE - (CONV_W - 1):, :ff][None])
        yp = yp_new

        hs = norm_mod(ys, g_pre_mix[l], mods_s, 0, 1, db)
        ps_t = matmul(hs, w_in_b, tm_s, TN_IN)
        ps = jnp.swapaxes(ps_t.reshape(s, db, np_), 0, 1).reshape(db * s, np_)
        mix_ret_s, ret_s = retention_sample(ps, state_ret.reshape(depth * db, h, dkv, dkv), l * db, ret_gn[l], s)
        cache3 = cache_kv.reshape(depth * n_phys, page * KV_ROWS, hd)
        kcvc_s = compress_sample(cache3, l * n_phys, page_table, pe2, w1b, w2b, hd)

        def head_rows(a, last):
            a = a.reshape(db, s, g, NSA_REP, last)
            return a.transpose(0, 2, 3, 1, 4).reshape(db, g, NSA_REP * s, last)

        q_s = head_rows(ps[:, q_col:kv_col], hd)
        gates_s = head_rows(ps[:, ng_col:ng_col + n_gate], 3)
        kv_new = jnp.pad(ps[:, kv_col:ng_col].reshape(db, s, 6 * gw), ((0, 0), (0, SUBLANE - s), (0, 0)))
        win_rows = ps[:, kv_col + 4 * gw:ng_col].reshape(db, s * WIN_ROWS, hd)
        o_s, win_new = nsa_sample(cache3, l * n_phys, page_table, q_s, gates_s, kcvc_s, kv_new, win_rows,
                                  cache_win.reshape(depth * db, wb * WIN_ROWS, hd), l * db, s)
        nsa_o_s = o_s.reshape(db, g, NSA_REP, s, hd).transpose(0, 3, 1, 2, 4).reshape(db * s, NSA_HEADS * hd)
        mix_s = jnp.concatenate([mix_ret_s, nsa_o_s], axis=1)
        mix_s = jnp.swapaxes(mix_s.reshape(db, s, -1), 0, 1).reshape(s * db, -1)
        prev = _pad_cols(jnp.swapaxes(state_conv[l], 0, 1).reshape((CONV_W - 1) * db, ff), ffp)
        ys_new, tail_s = _post(ys, mix_s, mods_s, db, tm_s, *post_w,
                               lambda h2: ffn_up_sample(h2, prev, w_up_b, conv_w_p, conv_b_p, db, TN_FF),
                               w_down_b, g_post_ffn[l])
        outs[4].append(ps[:, kv_col:kv_col + 4 * gw].reshape(db, s, 4, g, hd))
        outs[5].append(win_new.reshape(db, wb, 2, g, hd))
        outs[6].append(ret_s)
        outs[7].append(jnp.swapaxes(tail_s[:, :ff].reshape(CONV_W - 1, db, ff), 0, 1))
        ys = ys_new

    y_prompt = yp.reshape(1, t, d)
    y_sample = jnp.swapaxes(ys.reshape(s, db, d), 0, 1)
    return (y_prompt, y_sample) + tuple(jnp.stack(o) for o in outs)
```

```python
import functools

import jax
import jax.numpy as jnp
import numpy as np
from jax import lax
from jax.experimental import pallas as pl
from jax.experimental.pallas import tpu as pltpu

F32 = jnp.float32
BF16 = jnp.bfloat16

EPS = 1e-6
NEG = -1e30
FORCE = 1e4
RET_HEADS = 8
RET_CHUNK = 128
NSA_HEADS = 16
NSA_KV = 4
NSA_REP = NSA_HEADS // NSA_KV
L_CMP = 32
STRIDE_CMP = 16
L_SEL = 64
N_SELECT = 16
WINDOW = 512
CONV_W = 3

LANE = 128
SUBLANE = 8
VMEM_LIMIT = 56 * 1024 * 1024


def _cp(*sem):
    return pltpu.CompilerParams(dimension_semantics=sem, vmem_limit_bytes=VMEM_LIMIT)


def _silu(x):
    return x * jax.nn.sigmoid(x)


def _nt_dot(a, b):
    return lax.dot_general(a, b, (((1,), (1,)), ((), ())), preferred_element_type=F32)


def _tn_dot(a, b):
    return lax.dot_general(a, b, (((0,), (0,)), ((), ())), preferred_element_type=F32)


def _dot(a, b):
    return jnp.dot(a, b, preferred_element_type=F32)


def _ada_kernel(c_ref, w_ref, b_ref, o_ref):
    sc = _silu(c_ref[...]).astype(BF16)
    o_ref[...] = _dot(sc, w_ref[...].astype(BF16)) + b_ref[...]


def ada_mods(c_all, w_ada, b_ada, tn=512):
    m, d = c_all.shape
    n = w_ada.shape[1]
    return pl.pallas_call(
        _ada_kernel,
        out_shape=jax.ShapeDtypeStruct((m, n), F32),
        grid=(n // tn,),
        in_specs=[pl.BlockSpec((m, d), lambda j: (0, 0)),
                  pl.BlockSpec((d, tn), lambda j: (0, j)),
                  pl.BlockSpec((1, tn), lambda j: (0, j))],
        out_specs=pl.BlockSpec((m, tn), lambda j: (0, j)),
        compiler_params=_cp("arbitrary"),
        name="ada_mods",
    )(c_all, w_ada, b_ada.reshape(1, n))


def _rows(ref):
    v = ref[...]
    return v[0:1, :] if v.shape[0] == SUBLANE else v


def _rms(x, g):
    return x * lax.rsqrt(jnp.mean(x * x, axis=-1, keepdims=True) + EPS) * g


def _norm_mod_kernel(x_ref, g_ref, shift_ref, scale_ref, o_ref):
    y = _rms(x_ref[...], g_ref[...])
    o_ref[...] = (y * (1.0 + _rows(scale_ref)) + _rows(shift_ref)).astype(o_ref.dtype)


def _mod_spec(mods, tm, k, d):
    r = mods.shape[0]
    assert (r == SUBLANE) != (r == tm)
    return pl.BlockSpec((r, d), lambda i, k=k: (0, k))


def norm_mod(x, g, mods, k_shift, k_scale, tm):
    m, d = x.shape
    row = pl.BlockSpec((tm, d), lambda i: (i, 0))
    return pl.pallas_call(
        _norm_mod_kernel,
        out_shape=jax.ShapeDtypeStruct((m, d), BF16),
        grid=(m // tm,),
        in_specs=[row, pl.BlockSpec((1, d), lambda i: (0, 0)),
                  _mod_spec(mods, tm, k_shift, d), _mod_spec(mods, tm, k_scale, d)],
        out_specs=row,
        compiler_params=_cp("parallel"),
        name="norm_mod",
    )(x, g.reshape(1, d), mods, mods)


def _post_mix_kernel(f_ref, x_ref, g1_ref, g2_ref, gate_ref, shift_ref, scale_ref, x2_ref, h2_ref):
    x2 = x_ref[...] + _rows(gate_ref) * _rms(f_ref[...], g1_ref[...])
    x2_ref[...] = x2
    h2_ref[...] = (_rms(x2, g2_ref[...]) * (1.0 + _rows(scale_ref)) + _rows(shift_ref)).astype(h2_ref.dtype)


def post_mix(f, x, g_post_mix, g_pre_ffn, mods, tm):
    m, d = x.shape
    row = pl.BlockSpec((tm, d), lambda i: (i, 0))
    vec = pl.BlockSpec((1, d), lambda i: (0, 0))
    return pl.pallas_call(
        _post_mix_kernel,
        out_shape=(jax.ShapeDtypeStruct((m, d), F32), jax.ShapeDtypeStruct((m, d), BF16)),
        grid=(m // tm,),
        in_specs=[row, row, vec, vec, _mod_spec(mods, tm, 2, d), _mod_spec(mods, tm, 3, d),
                  _mod_spec(mods, tm, 4, d)],
        out_specs=(row, row),
        compiler_params=_cp("parallel"),
        name="post_mix",
    )(f, x, g_post_mix.reshape(1, d), g_pre_ffn.reshape(1, d), mods, mods, mods)


def _post_ffn_kernel(f_ref, x_ref, g_ref, gate_ref, y_ref):
    y_ref[...] = x_ref[...] + _rows(gate_ref) * _rms(f_ref[...], g_ref[...])


def post_ffn(f, x2, g_post_ffn, mods, tm):
    m, d = x2.shape
    row = pl.BlockSpec((tm, d), lambda i: (i, 0))
    return pl.pallas_call(
        _post_ffn_kernel,
        out_shape=jax.ShapeDtypeStruct((m, d), F32),
        grid=(m // tm,),
        in_specs=[row, row, pl.BlockSpec((1, d), lambda i: (0, 0)), _mod_spec(mods, tm, 5, d)],
        out_specs=row,
        compiler_params=_cp("parallel"),
        name="post_ffn",
    )(f, x2, g_post_ffn.reshape(1, d), mods)


def _mm_kernel(a_ref, w_ref, o_ref, *, nk):
    acc = _dot(a_ref[...], w_ref[...])
    if nk == 1:
        o_ref[...] = acc
    else:
        k = pl.program_id(2)

        @pl.when(k == 0)
        def _():
            o_ref[...] = acc

        @pl.when(k > 0)
        def _():
            o_ref[...] += acc


def matmul(a, w, tm, tn, tk=None):
    m, kd = a.shape
    n = w.shape[1]
    tk = kd if tk is None else tk
    nk = kd // tk
    assert m % tm == 0 and n % tn == 0 and kd % tk == 0
    return pl.pallas_call(
        functools.partial(_mm_kernel, nk=nk),
        out_shape=jax.ShapeDtypeStruct((m, n), F32),
        grid=(m // tm, n // tn, nk),
        in_specs=[pl.BlockSpec((tm, tk), lambda i, j, k: (i, k)),
                  pl.BlockSpec((tk, tn), lambda i, j, k: (k, j))],
        out_specs=pl.BlockSpec((tm, tn), lambda i, j, k: (i, j)),
        compiler_params=_cp("parallel", "parallel", "arbitrary"),
        name="matmul",
    )(a, w)


def _ret_out(o, gate, gn):
    mu = jnp.mean(o, axis=-1, keepdims=True)
    var = jnp.mean(jnp.square(o - mu), axis=-1, keepdims=True)
    return (o - mu) * lax.rsqrt(var + EPS) * gn * _silu(gate)


def _ret_prompt_kernel(q_ref, k_ref, v_ref, gate_ref, dmat_ref, dq_ref, dk_ref, gc_ref, gn_ref,
                       o_ref, st_ref, *, dk_scale):
    @pl.when(pl.program_id(1) == 0)
    def _():
        st_ref[...] = jnp.zeros_like(st_ref)

    k = k_ref[...] * dk_scale
    qb = q_ref[...].astype(BF16)
    vb = v_ref[...].astype(BF16)
    attn = _nt_dot(qb, k.astype(BF16)) * dmat_ref[...]
    st = st_ref[...]
    o = _dot(attn.astype(BF16), vb) + _dot(qb, st.astype(BF16)) * dq_ref[...]
    kdec = (k * dk_ref[...]).astype(BF16)
    st_ref[...] = gc_ref[...] * st + _tn_dot(kdec, vb)
    o_ref[...] = _ret_out(o, gate_ref[...], gn_ref[...]).astype(o_ref.dtype)


def _ret_tables(c):
    log_g = jnp.log(1.0 - jnp.exp2(-5.0 - jnp.arange(RET_HEADS, dtype=F32)))
    i = jnp.arange(c, dtype=F32)
    diff = i[:, None] - i[None, :]
    dmat = jnp.where(diff >= 0, jnp.exp(log_g[:, None, None] * jnp.maximum(diff, 0.0)), 0.0)
    dq = jnp.exp(log_g[:, None] * (i[None, :] + 1.0))[:, :, None]
    dk = jnp.exp(log_g[:, None] * (c - 1.0 - i[None, :]))[:, :, None]
    gc = jnp.exp(log_g * c)[:, None, None]
    return dmat, dq, dk, gc


def retention_prompt(p, ret_gn, dkv):
    t = p.shape[0]
    c = RET_CHUNK
    h = RET_HEADS
    dmat, dq, dk, gc = _ret_tables(c)

    def col(part):
        return pl.BlockSpec((c, dkv), lambda hh, cc, part=part: (cc, part * h + hh))

    def tab(shape):
        return pl.BlockSpec((None,) + shape, lambda hh, cc: (hh, 0, 0))

    return pl.pallas_call(
        functools.partial(_ret_prompt_kernel, dk_scale=dkv ** -0.5),
        out_shape=(jax.ShapeDtypeStruct((t, h * dkv), BF16), jax.ShapeDtypeStruct((h, dkv, dkv), F32)),
        grid=(h, t // c),
        in_specs=[col(0), col(1), col(2), col(3), tab((c, c)), tab((c, 1)), tab((c, 1)), tab((1, 1)),
                  pl.BlockSpec((1, dkv), lambda hh, cc: (0, hh))],
        out_specs=(pl.BlockSpec((c, dkv), lambda hh, cc: (cc, hh)),
                   pl.BlockSpec((None, dkv, dkv), lambda hh, cc: (hh, 0, 0))),
        compiler_params=_cp("parallel", "arbitrary"),
        name="retention_prompt",
    )(p, p, p, p, dmat, dq, dk, gc, ret_gn.reshape(1, h * dkv))


def _ret_sample_kernel(q_ref, k_ref, v_ref, gate_ref, st_ref, dmat_ref, dq_ref, dk_ref, gc_ref, gn_ref,
                       o_ref, nst_ref, *, dkv, s, dk_scale):
    rows = 2 * s
    rid = lax.broadcasted_iota(jnp.int32, (rows, 1), 0)
    first = rid < s
    for h in range(RET_HEADS):
        sl = slice(h * dkv, (h + 1) * dkv)
        q = q_ref[:, sl]
        k = k_ref[:, sl] * dk_scale
        v = v_ref[:, sl]
        attn = _nt_dot(q, k) * dmat_ref[h]
        o = _dot(attn, v)
        dq = dq_ref[h]
        kdec = k * dk_ref[h]
        for b in range(2):
            st = st_ref[b, h]
            mine = first if b == 0 else jnp.logical_not(first)
            o = o + jnp.where(mine, _dot(q, st) * dq, 0.0)
            nst_ref[b, h] = gc_ref[h] * st + _tn_dot(jnp.where(mine, kdec, 0.0), v)
        o_ref[:, sl] = _ret_out(o, gate_ref[:, sl], gn_ref[:, sl]).astype(o_ref.dtype)


def retention_sample(p, state, b0, ret_gn, s):
    _, h, dkv, _ = state.shape
    db = p.shape[0] // s
    rows = 2 * s
    assert rows == SUBLANE and db % 2 == 0
    log_g = jnp.log(1.0 - jnp.exp2(-5.0 - jnp.arange(h, dtype=F32)))
    i = jnp.arange(rows)
    tok = (i % s).astype(F32)
    same = (i[:, None] // s) == (i[None, :] // s)
    diff = tok[:, None] - tok[None, :]
    dmat = jnp.where(same & (diff >= 0), jnp.exp(log_g[:, None, None] * jnp.maximum(diff, 0.0)), 0.0)
    dq = jnp.exp(log_g[:, None] * (tok[None, :] + 1.0))[:, :, None]
    dk = jnp.exp(log_g[:, None] * (s - 1.0 - tok[None, :]))[:, :, None]
    gc = jnp.exp(log_g * s)[:, None, None]
    w = h * dkv

    def col(part):
        return pl.BlockSpec((rows, w), lambda i, part=part: (i, part))

    def full(a):
        return pl.BlockSpec(a.shape, lambda i: (0,) * a.ndim)

    assert b0 % 2 == 0
    st_block = (2, h, dkv, dkv)
    return pl.pallas_call(
        functools.partial(_ret_sample_kernel, dkv=dkv, s=s, dk_scale=dkv ** -0.5),
        out_shape=(jax.ShapeDtypeStruct((db * s, w), BF16), jax.ShapeDtypeStruct((db, h, dkv, dkv), F32)),
        grid=(db // 2,),
        in_specs=[col(0), col(1), col(2), col(3), pl.BlockSpec(st_block, lambda i: (b0 // 2 + i, 0, 0, 0)),
                  full(dmat), full(dq), full(dk), full(gc), pl.BlockSpec((1, w), lambda i: (0, 0))],
        out_specs=(pl.BlockSpec((rows, w), lambda i: (i, 0)), pl.BlockSpec(st_block, lambda i: (i, 0, 0, 0))),
        compiler_params=_cp("parallel"),
        name="retention_sample",
    )(p, p, p, p, state, dmat, dq, dk, gc, ret_gn.reshape(1, w))


def _compress_block(load_rows, pe_ref, w1_ref, w2, accb_ref, nh):
    pairs = L_CMP // 4
    acc_a = None
    acc_b = None
    for i in range(pairs):
        y = jnp.concatenate([load_rows(2 * i), load_rows(2 * i + 1)], axis=1)
        da = _dot((y + pe_ref[i:i + 1, :]).astype(BF16), w1_ref[i])
        db = _dot((y + pe_ref[pairs + i:pairs + i + 1, :]).astype(BF16), w1_ref[pairs + i])
        acc_a = da if acc_a is None else acc_a + da
        acc_b = db if acc_b is None else acc_b + db
    accb_ref[0:nh, :] = acc_b
    accb_ref[nh:nh + SUBLANE, :] = jnp.zeros((SUBLANE, acc_b.shape[1]), F32)
    h = jax.nn.gelu(acc_a + accb_ref[pl.ds(1, nh), :])
    return _dot(h.astype(BF16), w2)


def _masked_softmax(s, mask):
    s = jnp.where(mask, s, NEG)
    e = jnp.exp(s - jnp.max(s, axis=-1, keepdims=True))
    return jnp.where(mask, e / jnp.sum(e, axis=-1, keepdims=True), 0.0)


def _importance(psum, ov):
    hi = psum.astype(BF16)
    lo = (psum - hi.astype(F32)).astype(BF16)
    return _dot(hi, ov) + _dot(lo, ov)


def _select_blocks_t(imp_t, q_pos, n_sel):
    nb = imp_t.shape[0]
    blk = lax.broadcasted_iota(jnp.int32, (nb, 1), 0)
    blkf = blk.astype(F32)
    cur = q_pos // L_SEL
    forced = (blk == 0) | (blk == cur) | (blk == cur - 1)
    score = jnp.where(blk > cur, -1.0, jnp.where(forced, FORCE, imp_t))
    score = jnp.where(blk >= n_sel, -2.0, score)
    taken = -3.0
    work = score
    for _ in range(min(N_SELECT, n_sel)):
        m = jnp.max(work, axis=0, keepdims=True)
        first = jnp.min(jnp.where(work == m, blkf, float(nb)), axis=0, keepdims=True)
        work = jnp.where(blkf == first, taken, work)
    return jnp.where((work == taken) & (score >= 0.0), 1.0, 0.0)


def _overlap_matrix(nc, nb):
    n = jnp.arange(nc)[:, None]
    j = jnp.arange(nb)[None, :]
    c_start = n * STRIDE_CMP
    c_end = c_start + L_CMP - 1
    return ((c_start < (j + 1) * L_SEL) & (c_end >= j * L_SEL)).astype(BF16)


def _expand_matrix(nb, nkeys):
    return (jnp.arange(nb)[:, None] == (jnp.arange(nkeys)[None, :] // L_SEL)).astype(BF16)


def _slopes():
    h = jnp.arange(1, NSA_HEADS + 1, dtype=F32)
    return jnp.exp2(-8.0 * h / NSA_HEADS).reshape(NSA_KV, NSA_REP)


def _compress_prompt_kernel(x_ref, pe_ref, w1_ref, w2_ref, o_ref, accb_ref):
    nh = x_ref.shape[0] // STRIDE_CMP
    o_ref[...] = _compress_block(lambda l: x_ref[pl.ds(l, nh, stride=STRIDE_CMP), :],
                                 pe_ref, w1_ref, w2_ref[...], accb_ref, nh)


def compress_prompt(p, kv_col, cmp_pe, w1b, w2b, hd):
    t = p.shape[0]
    nh = t // STRIDE_CMP
    hid = w1b.shape[-1]
    cb = kv_col // hd
    return pl.pallas_call(
        _compress_prompt_kernel,
        out_shape=jax.ShapeDtypeStruct((2, NSA_KV, nh, hd), F32),
        grid=(2, NSA_KV),
        in_specs=[pl.BlockSpec((t, hd), lambda kind, g: (0, cb + kind * NSA_KV + g)),
                  pl.BlockSpec((None, L_CMP // 2, 2 * hd), lambda kind, g: (kind, 0, 0)),
                  pl.BlockSpec((None, L_CMP // 2, 2 * hd, hid), lambda kind, g: (kind, 0, 0, 0)),
                  pl.BlockSpec((None, hid, hd), lambda kind, g: (kind, 0, 0))],
        out_specs=pl.BlockSpec((None, None, nh, hd), lambda kind, g: (kind, g, 0, 0)),
        scratch_shapes=[pltpu.VMEM((nh + SUBLANE, hid), F32)],
        compiler_params=_cp("parallel", "parallel"),
        name="compress_prompt",
    )(p, cmp_pe, w1b, w2b)


MASK_BIG = 2.0 ** 99
LOG2E = 1.4426950408889634


def _cmp_win_select_kernel(q_ref, kc_ref, vc_ref, ov_ref, sl_ref, gate_ref, *rest, tq, hd, n_sel, n_prev, scale):
    kw_refs = rest[:n_prev + 1]
    vw_refs = rest[n_prev + 1:2 * n_prev + 2]
    part_ref, unsel_ref = rest[2 * n_prev + 2:]
    qt = pl.program_id(1)
    nc = kc_ref.shape[0]
    nw = (n_prev + 1) * tq
    q_pos = qt * tq + lax.broadcasted_iota(jnp.int32, (tq, 1), 0)
    dist_c = q_pos - (lax.broadcasted_iota(jnp.int32, (1, nc), 1) * STRIDE_CMP + (L_CMP - 1))
    mask_c = dist_c >= 0
    dist_cf = dist_c.astype(F32)
    kcb = kc_ref[...].astype(BF16)
    vcb = vc_ref[...].astype(BF16)
    w_pos = (qt - n_prev) * tq + lax.broadcasted_iota(jnp.int32, (1, nw), 1)
    dist_w = q_pos - w_pos
    mask_w = (dist_w >= 0) & (dist_w <= WINDOW) & (w_pos >= 0)
    dist_wf = dist_w.astype(F32)
    kwb = jnp.concatenate([r[...].astype(BF16) for r in kw_refs], axis=0)
    vwb = jnp.concatenate([r[...].astype(BF16) for r in vw_refs], axis=0)
    gate = jax.nn.sigmoid(gate_ref[...])
    heads = range(NSA_REP)
    qbs = [q_ref[:, r * hd:(r + 1) * hd].astype(BF16) for r in heads]
    s_cs = [_nt_dot(qbs[r], kcb) for r in heads]
    s_ws = [_nt_dot(qbs[r], kwb) for r in heads]
    p_cs = [_masked_softmax(s_cs[r] * scale - sl_ref[r][:, 0:1] * dist_cf, mask_c) for r in heads]
    p_ws = [_masked_softmax(s_ws[r] * scale - sl_ref[r][:, 0:1] * dist_wf, mask_w) for r in heads]
    psum = sum(p_cs[1:], p_cs[0])
    for r in heads:
        part_ref[:, r * hd:(r + 1) * hd] = (gate[:, 3 * r:3 * r + 1] * _dot(p_cs[r].astype(BF16), vcb)
                                            + gate[:, 3 * r + 2:3 * r + 3] * _dot(p_ws[r].astype(BF16), vwb))
    q_pos_row = qt * tq + lax.broadcasted_iota(jnp.int32, (1, tq), 1)
    picked_t = _select_blocks_t(_importance(psum, ov_ref[...]).T, q_pos_row, n_sel)
    unsel_ref[...] = (picked_t - 1.0) * MASK_BIG


def cmp_win_select_prompt(p, q_col, kv_col, kcvc, gates, tq=256):
    t = p.shape[0]
    nc, hd = kcvc.shape[2:]
    n_sel = -(-t // L_SEL)
    tq = min(tq, t)
    assert N_SELECT <= n_sel <= LANE and WINDOW % tq == 0 and t % tq == 0
    n_prev = WINDOW // tq
    w = NSA_REP * hd
    cb = kv_col // hd
    slopes = jnp.broadcast_to(_slopes()[:, :, None, None], (NSA_KV, NSA_REP, 1, LANE))

    def win_specs(kind):
        return [pl.BlockSpec((tq, hd), lambda g, i, j=j: (jnp.maximum(i - j, 0), cb + kind * NSA_KV + g))
                for j in range(n_prev, -1, -1)]

    return pl.pallas_call(
        functools.partial(_cmp_win_select_kernel, tq=tq, hd=hd, n_sel=n_sel, n_prev=n_prev, scale=hd ** -0.5),
        out_shape=(jax.ShapeDtypeStruct((t, NSA_KV * w), F32), jax.ShapeDtypeStruct((NSA_KV, LANE, t), F32)),
        grid=(NSA_KV, t // tq),
        in_specs=[pl.BlockSpec((tq, w), lambda g, i: (i, q_col // w + g)),
                  pl.BlockSpec((None, None, nc, hd), lambda g, i: (0, g, 0, 0)),
                  pl.BlockSpec((None, None, nc, hd), lambda g, i: (1, g, 0, 0)),
                  pl.BlockSpec((nc, LANE), lambda g, i: (0, 0)),
                  pl.BlockSpec((None, NSA_REP, 1, LANE), lambda g, i: (g, 0, 0, 0)),
                  pl.BlockSpec((None, tq, 3 * NSA_REP), lambda g, i: (g, i, 0))]
        + win_specs(4) + win_specs(5),
        out_specs=(pl.BlockSpec((tq, w), lambda g, i: (i, g)),
                   pl.BlockSpec((None, LANE, tq), lambda g, i: (g, 0, i))),
        compiler_params=_cp("parallel", "parallel"),
        name="cmp_win_select_prompt",
    )(p, kcvc, kcvc, _overlap_matrix(nc, LANE), slopes, gates, *([p] * (2 * n_prev + 2)))


def _slc_kernel(qi_ref, ki_ref, q_ref, unsel_ref, al_ref, k_ref, vt_ref, kf_ref, gate_ref, part_ref,
                o_ref, qa_ref, al_sc, m_ref, l_ref, acc_ref, *, tq, tk, hd, scale):
    step = pl.program_id(1)
    qt = qi_ref[step]
    kt = ki_ref[step]

    cols = [slice(r * tq, (r + 1) * tq) for r in range(NSA_REP)]

    @pl.when(kt == 0)
    def _():
        q_pos = (qt * tq + lax.broadcasted_iota(jnp.int32, (1, tq), 1)).astype(F32)
        sub = lax.broadcasted_iota(jnp.int32, (SUBLANE, 1), 0)
        for r in range(NSA_REP):
            qa_ref[0:hd, cols[r]] = (q_ref[:, r * hd:(r + 1) * hd] * (scale * LOG2E)).T.astype(BF16)
            tab = jnp.concatenate([al_ref[r]] * (tq // LANE), axis=1)
            slope = tab[ALIBI_SLOPE:ALIBI_SLOPE + 1, :]
            al_sc[:, cols[r]] = jnp.where(sub == ALIBI_ROWCONST, -slope * q_pos,
                                          jnp.where(sub == ALIBI_SLOPE, 0.0, tab))
        qa_ref[hd + FEAT:2 * hd, :] = jnp.zeros((hd - FEAT, NSA_REP * tq), BF16)
        m_ref[...] = jnp.full(m_ref.shape, M_FLOOR, F32)
        l_ref[...] = jnp.zeros(l_ref.shape, F32)
        acc_ref[...] = jnp.zeros(acc_ref.shape, F32)

    picks = unsel_ref[pl.ds(pl.multiple_of(kt * (tk // L_SEL), SUBLANE), tk // L_SEL), :]
    feat = jnp.concatenate([jnp.concatenate([picks] * NSA_REP, axis=1), al_sc[...]], axis=0)
    qa_ref[hd:hd + FEAT, :] = feat.astype(BF16)
    ka = jnp.concatenate([k_ref[...].astype(BF16), kf_ref[...]], axis=1)
    vtb = vt_ref[...].astype(BF16)
    last = (qt * tq + tq - 1) // tk

    def update(scores):
        for r in range(NSA_REP):
            s = scores[r]
            m_old = m_ref[:, cols[r]]
            m_new = jnp.maximum(m_old, jnp.max(s, axis=0, keepdims=True))
            alpha = jnp.exp2(m_old - m_new)
            e = jnp.exp2(s - m_new)
            l_ref[:, cols[r]] = alpha * l_ref[:, cols[r]] + jnp.sum(e, axis=0, keepdims=True)
            acc_ref[:, cols[r]] = alpha * acc_ref[:, cols[r]] + _dot(vtb, e.astype(BF16))
            m_ref[:, cols[r]] = m_new

    @pl.when(kt != last)
    def _():
        update([_dot(ka, qa_ref[:, cols[r]]) for r in range(NSA_REP)])

    @pl.when(kt == last)
    def _():
        causal = (kt * tk + lax.broadcasted_iota(jnp.int32, (tk, 1), 0)
                  <= qt * tq + lax.broadcasted_iota(jnp.int32, (1, tq), 1))
        gate = jax.nn.sigmoid(gate_ref[...])
        update([jnp.where(causal, _dot(ka, qa_ref[:, cols[r]]), NEG) for r in range(NSA_REP)])
        for r in range(NSA_REP):
            sl = slice(r * hd, (r + 1) * hd)
            o_s = (acc_ref[:, cols[r]] / l_ref[:, cols[r]]).T
            o_ref[:, sl] = (part_ref[:, sl] + gate[:, 3 * r + 1:3 * r + 2] * o_s).astype(o_ref.dtype)


FEAT = 2 * SUBLANE
ALIBI_ROWCONST = 6
ALIBI_SLOPE = 7
M_FLOOR = -1e29


def _alibi_tables(t, tk):
    slope = _slopes() * LOG2E
    s1 = slope.astype(BF16).astype(F32)
    s2 = (slope - s1).astype(BF16).astype(F32)
    s3 = (slope - s1 - s2).astype(BF16).astype(F32)
    zero = jnp.zeros_like(slope)
    qf = jnp.stack([s1, s2, s3, s1, s2, s3, zero, slope], axis=-1)
    qf = jnp.broadcast_to(qf[:, :, :, None], qf.shape + (LANE,))
    pos = jnp.arange(t)
    blk = (pos // L_SEL) % (tk // L_SEL)
    onehot = (blk[:, None] == jnp.arange(SUBLANE)[None, :]).astype(F32)
    hi = (pos // L_SEL * L_SEL).astype(F32)[:, None]
    lo = (pos % L_SEL).astype(F32)[:, None]
    one = jnp.ones((t, 1), F32)
    kf = jnp.concatenate([onehot, hi, hi, hi, lo, lo, lo, one, jnp.zeros((t, LANE - FEAT + 1), F32)], axis=1)
    return qf, kf.astype(BF16)


def slc_prompt(p, q_col, kv_col, gates, part, unsel_t, tq=256, tk=512):
    t = p.shape[0]
    hd = part.shape[1] // NSA_HEADS
    w = NSA_REP * hd
    tq, tk = min(tq, t), min(tk, t)
    assert t % tq == 0 and t % tk == 0 and tk % tq == 0 and tk == SUBLANE * L_SEL and hd == LANE
    qi, ki = [], []
    for i in range(t // tq):
        for k in range((i * tq + tq - 1) // tk + 1):
            qi.append(i)
            ki.append(k)
    qi, ki = jnp.asarray(qi, jnp.int32), jnp.asarray(ki, jnp.int32)
    cb = kv_col // hd
    al_q, kfeat = _alibi_tables(t, tk)
    v_t = jnp.swapaxes(p[:, kv_col + 3 * NSA_KV * hd:kv_col + 4 * NSA_KV * hd].reshape(t, NSA_KV, hd), 0, 1)
    v_t = jnp.swapaxes(v_t, 1, 2)
    qrow = lambda g, s, qi, ki: (qi[s], g)
    rows = NSA_REP * tq
    return pl.pallas_call(
        functools.partial(_slc_kernel, tq=tq, tk=tk, hd=hd, scale=hd ** -0.5),
        out_shape=jax.ShapeDtypeStruct((t, NSA_KV * w), BF16),
        grid_spec=pltpu.PrefetchScalarGridSpec(
            num_scalar_prefetch=2,
            grid=(NSA_KV, int(qi.shape[0])),
            in_specs=[pl.BlockSpec((tq, w), lambda g, s, qi, ki: (qi[s], q_col // w + g)),
                      pl.BlockSpec((None, LANE, tq), lambda g, s, qi, ki: (g, 0, qi[s])),
                      pl.BlockSpec((None, NSA_REP, SUBLANE, LANE), lambda g, s, qi, ki: (g, 0, 0, 0)),
                      pl.BlockSpec((tk, hd), lambda g, s, qi, ki: (ki[s], cb + 2 * NSA_KV + g)),
                      pl.BlockSpec((None, hd, tk), lambda g, s, qi, ki: (g, 0, ki[s])),
                      pl.BlockSpec((tk, LANE), lambda g, s, qi, ki: (ki[s], 0)),
                      pl.BlockSpec((None, tq, 3 * NSA_REP), lambda g, s, qi, ki: (g, qi[s], 0)),
                      pl.BlockSpec((tq, w), qrow)],
            out_specs=pl.BlockSpec((tq, w), qrow),
            scratch_shapes=[pltpu.VMEM((2 * hd, rows), BF16), pltpu.VMEM((SUBLANE, rows), F32),
                            pltpu.VMEM((1, rows), F32), pltpu.VMEM((1, rows), F32),
                            pltpu.VMEM((hd, rows), F32)]),
        compiler_params=_cp("parallel", "arbitrary"),
        name="slc_prompt",
    )(qi, ki, p, unsel_t, al_q, p, v_t, kfeat, gates, part)


KV_ROWS = 4 * NSA_KV
WIN_ROWS = 2 * NSA_KV


def _page_specs(n_pages, rows, hd, page0):
    return [pl.BlockSpec((None, rows, hd), lambda b, pt, j=j: (page0 + pt[b * n_pages + j], 0, 0))
            for j in range(n_pages)]


def _head_rows(ref, kind, g, n, rows_per_pos):
    return ref[pl.ds(kind * NSA_KV + g, n, stride=rows_per_pos), :]


def _compress_sample_kernel(pt_ref, *refs, n_pages, hd):
    page_refs = refs[:n_pages]
    pe_ref, mask_ref, w1_ref, w2_ref, o_ref, acc_ref = refs[n_pages:]
    page = page_refs[0].shape[0] // KV_ROWS
    per_page = page // STRIDE_CMP
    nh = n_pages * per_page
    half = L_CMP // 2
    hid = w2_ref.shape[0] // 2
    mask = mask_ref[...][None]

    def by_kind(x):
        x3 = x.reshape(nh, WIN_ROWS, x.shape[-1])
        return jnp.concatenate([x3, x3], axis=-1)

    for l in range(half):
        y = by_kind(jnp.concatenate(
            [pr[pl.ds((m * STRIDE_CMP + l) * KV_ROWS, WIN_ROWS), :] for pr in page_refs for m in range(per_page)],
            axis=0))
        for part in range(2):
            lhs = ((y + pe_ref[part * half + l][None]) * mask).astype(BF16).reshape(nh * WIN_ROWS, 2 * hd)
            d = _dot(lhs, w1_ref[part * half + l])
            cols = slice(part * hid, (part + 1) * hid)
            if l == 0:
                acc_ref[0:nh * WIN_ROWS, cols] = d
            else:
                acc_ref[0:nh * WIN_ROWS, cols] += d
    acc_ref[nh * WIN_ROWS:(nh + 1) * WIN_ROWS, :] = jnp.zeros((WIN_ROWS, 2 * hid), F32)
    h = jax.nn.gelu(acc_ref[0:nh * WIN_ROWS, 0:hid] + acc_ref[WIN_ROWS:(nh + 1) * WIN_ROWS, hid:2 * hid])
    h3 = h.reshape(nh, WIN_ROWS, hid)
    kind1 = lax.broadcasted_iota(jnp.int32, (1, WIN_ROWS, 1), 1) >= NSA_KV
    h2 = jnp.concatenate([jnp.where(kind1, 0.0, h3), jnp.where(kind1, h3, 0.0)], axis=-1)
    o_ref[...] = _dot(h2.astype(BF16).reshape(nh * WIN_ROWS, 2 * hid), w2_ref[...])


def compress_sample(cache3, page0, page_table, cmp_pe, cmp_w1, cmp_w2, hd):
    db, n_pages = page_table.shape
    page = cache3.shape[1] // KV_ROWS
    nh = n_pages * page // STRIDE_CMP
    hid = cmp_w1.shape[-1]
    assert L_CMP == 2 * STRIDE_CMP
    kind1 = (jnp.arange(WIN_ROWS) >= NSA_KV)[None, :, None]
    pe0 = jnp.broadcast_to(cmp_pe[0][:, None, :], (L_CMP, WIN_ROWS, hd))
    pe1 = jnp.broadcast_to(cmp_pe[1][:, None, :], (L_CMP, WIN_ROWS, hd))
    pe8 = jnp.concatenate([jnp.where(kind1, 0.0, pe0), jnp.where(kind1, pe1, 0.0)], axis=-1)
    mask8 = jnp.concatenate([jnp.where(kind1[0], 0.0, jnp.ones((WIN_ROWS, hd), F32)),
                             jnp.where(kind1[0], jnp.ones((WIN_ROWS, hd), F32), 0.0)], axis=-1)
    w1s = jnp.concatenate([cmp_w1[0], cmp_w1[1]], axis=1).astype(BF16)
    w2s = jnp.concatenate([cmp_w2[0], cmp_w2[1]], axis=0).astype(BF16)
    full = lambda a: pl.BlockSpec(a.shape, lambda b, pt: (0,) * a.ndim)
    return pl.pallas_call(
        functools.partial(_compress_sample_kernel, n_pages=n_pages, hd=hd),
        out_shape=jax.ShapeDtypeStruct((db, nh * WIN_ROWS, hd), F32),
        grid_spec=pltpu.PrefetchScalarGridSpec(
            num_scalar_prefetch=1,
            grid=(db,),
            in_specs=_page_specs(n_pages, page * KV_ROWS, hd, page0) + [full(pe8), full(mask8), full(w1s), full(w2s)],
            out_specs=pl.BlockSpec((None, nh * WIN_ROWS, hd), lambda b, pt: (b, 0, 0)),
            scratch_shapes=[pltpu.VMEM(((nh + 1) * WIN_ROWS, 2 * hid), F32)]),
        compiler_params=_cp("parallel"),
        name="compress_sample",
    )(page_table.reshape(-1), *([cache3] * n_pages), pe8, mask8, w1s, w2s)


def _softmax2(s_a, mask_a, s_b, mask_b):
    s_a = jnp.where(mask_a, s_a, NEG)
    s_b = jnp.where(mask_b, s_b, NEG)
    m = jnp.maximum(jnp.max(s_a, axis=-1, keepdims=True), jnp.max(s_b, axis=-1, keepdims=True))
    e_a = jnp.where(mask_a, jnp.exp(s_a - m), 0.0)
    e_b = jnp.where(mask_b, jnp.exp(s_b - m), 0.0)
    return e_a, e_b, jnp.sum(e_a, axis=-1, keepdims=True) + jnp.sum(e_b, axis=-1, keepdims=True)


def _nsa_sample_kernel(pt_ref, *refs, n_pages, hd, s_new, scale):
    page_refs = refs[:n_pages]
    (q_ref, gate_ref, kcvc_ref, new_ref, nw_ref, win_ref, ov_ref, e_ref, rm_ref, sl_ref, o_ref, wn_ref,
     imp_ref) = refs[n_pages:]
    page = page_refs[0].shape[0] // KV_ROWS
    past = n_pages * page
    wb = win_ref.shape[0] // WIN_ROWS
    gw = NSA_KV * hd
    rows = NSA_REP * s_new
    nc = kcvc_ref.shape[0] // WIN_ROWS
    n_cmp = (past + s_new - L_CMP) // STRIDE_CMP + 1
    n_sel = -(-(past + s_new) // L_SEL)
    tok = lax.broadcasted_iota(jnp.int32, (rows, 1), 0) % s_new
    q_pos = past + tok
    tnew = lax.broadcasted_iota(jnp.int32, (1, SUBLANE), 1)
    dist_new = tok - tnew
    mask_new = (dist_new >= 0) & (tnew < s_new)
    n = lax.broadcasted_iota(jnp.int32, (1, nc), 1)
    dist_c = q_pos - (n * STRIDE_CMP + (L_CMP - 1))
    mask_c = (dist_c >= 0) & (n < n_cmp)
    imp_ref[...] = jnp.zeros(imp_ref.shape, F32)
    o_cs = []
    for g in range(NSA_KV):
        s_c = (_nt_dot(q_ref[g].astype(BF16), _head_rows(kcvc_ref, 0, g, nc, WIN_ROWS).astype(BF16)) * scale
               - sl_ref[g] * dist_c.astype(F32))
        p_c = _masked_softmax(s_c, mask_c)
        o_cs.append(_dot(p_c.astype(BF16), _head_rows(kcvc_ref, 1, g, nc, WIN_ROWS).astype(BF16)))
        imp_ref[g * rows:(g + 1) * rows, :] = _importance(_dot(rm_ref[...], p_c), ov_ref[...])
    lane_pos = past + lax.broadcasted_iota(jnp.int32, (1, LANE), 1) % s_new
    picked_all = _select_blocks_t(imp_ref[...].T, lane_pos, n_sel).T
    groups = range(NSA_KV)
    dist_s = q_pos - lax.broadcasted_iota(jnp.int32, (1, past), 1)
    dist_w = wb + tok - lax.broadcasted_iota(jnp.int32, (1, wb), 1)
    mask_w = (dist_w >= 0) & (dist_w <= WINDOW)

    def new_rows(kind, g):
        return new_ref[:, kind * gw + g * hd:kind * gw + (g + 1) * hd]

    def scores(q, keys, dist, g):
        return _nt_dot(q, keys) * scale - sl_ref[g] * dist.astype(F32)

    qs = [q_ref[g].astype(BF16) for g in groups]
    picked = [picked_all[g * rows:(g + 1) * rows, :] for g in groups]
    key_masks = [(_dot(picked[g].astype(BF16), e_ref[...]) > 0.5) & (dist_s >= 0) for g in groups]
    cur_masks = [mask_new & (picked[g][:, past // L_SEL:past // L_SEL + 1] > 0.5) for g in groups]
    s_s = [scores(qs[g], jnp.concatenate([_head_rows(pr, 2, g, page, KV_ROWS).astype(BF16) for pr in page_refs],
                                         axis=0), dist_s, g) for g in groups]
    s_n = [scores(q_ref[g], new_rows(2, g), dist_new, g) for g in groups]
    s_w = [scores(qs[g], _head_rows(win_ref, 0, g, wb, WIN_ROWS).astype(BF16), dist_w, g) for g in groups]
    s_wn = [scores(q_ref[g], new_rows(4, g), dist_new, g) for g in groups]
    sm_s = [_softmax2(s_s[g], key_masks[g], s_n[g], cur_masks[g]) for g in groups]
    sm_w = [_softmax2(s_w[g], mask_w, s_wn[g], mask_new) for g in groups]
    for g in groups:
        e_s, e_n, l_s = sm_s[g]
        e_w, e_wn, l_w = sm_w[g]
        vs = jnp.concatenate([_head_rows(pr, 3, g, page, KV_ROWS).astype(BF16) for pr in page_refs], axis=0)
        o_s = (_dot(e_s.astype(BF16), vs) + _dot(e_n, new_rows(3, g))) / l_s
        o_w = (_dot(e_w.astype(BF16), _head_rows(win_ref, 1, g, wb, WIN_ROWS).astype(BF16))
               + _dot(e_wn, new_rows(5, g))) / l_w
        gate = jax.nn.sigmoid(gate_ref[g])
        o_ref[g] = (gate[:, 0:1] * o_cs[g] + gate[:, 1:2] * o_s + gate[:, 2:3] * o_w).astype(o_ref.dtype)
    keep = (wb - s_new) * WIN_ROWS
    wn_ref[0:keep, :] = win_ref[s_new * WIN_ROWS:wb * WIN_ROWS, :]
    wn_ref[keep:wb * WIN_ROWS, :] = nw_ref[...]


def nsa_sample(cache3, page0, page_table, q, gates, kcvc, kv_new, win_new, win, win0, s_new):
    db, n_pages = page_table.shape
    page = cache3.shape[1] // KV_ROWS
    past = n_pages * page
    hd = q.shape[-1]
    rows = q.shape[2]
    nc = kcvc.shape[1] // WIN_ROWS
    wb = win.shape[1] // WIN_ROWS
    assert wb <= past and wb <= WINDOW and N_SELECT <= -(-(past + s_new) // L_SEL) <= LANE
    assert past % L_SEL == 0 and s_new <= SUBLANE and s_new <= L_SEL and NSA_KV * rows <= LANE
    i = jnp.arange(rows)
    rm = (i[:, None] % s_new == i[None, :] % s_new).astype(F32)
    slopes = jnp.repeat(_slopes(), s_new, axis=1)[:, :, None]
    ov = _overlap_matrix(nc, LANE)
    em = _expand_matrix(LANE, past)
    per_b = lambda a: pl.BlockSpec((None,) + a.shape[1:], lambda b, pt: (b,) + (0,) * (a.ndim - 1))
    full = lambda a: pl.BlockSpec(a.shape, lambda b, pt: (0,) * a.ndim)
    return pl.pallas_call(
        functools.partial(_nsa_sample_kernel, n_pages=n_pages, hd=hd, s_new=s_new, scale=hd ** -0.5),
        out_shape=(jax.ShapeDtypeStruct(q.shape, BF16), jax.ShapeDtypeStruct((db,) + win.shape[1:], F32)),
        grid_spec=pltpu.PrefetchScalarGridSpec(
            num_scalar_prefetch=1,
            grid=(db,),
            in_specs=_page_specs(n_pages, page * KV_ROWS, hd, page0)
            + [per_b(q), per_b(gates), per_b(kcvc), per_b(kv_new), per_b(win_new),
               pl.BlockSpec((None,) + win.shape[1:], lambda b, pt: (win0 + b, 0, 0)),
               full(ov), full(em), full(rm), full(slopes)],
            out_specs=(per_b(q), per_b(win)),
            scratch_shapes=[pltpu.VMEM((LANE, LANE), F32)]),
        compiler_params=_cp("parallel"),
        name="nsa_sample",
    )(page_table.reshape(-1), *([cache3] * n_pages), q, gates, kcvc, kv_new, win_new, win, ov, em, rm, slopes)


def _ffn_up_prompt_kernel(h_ref, halo_ref, wa_ref, wb_ref, cw_ref, cb_ref, u_ref, tail_ref, ext_ref, *, tm):
    wa = wa_ref[...]
    h = h_ref[...]
    a = _dot(h, wa)
    prev = _dot(halo_ref[...], wa)
    ext_ref[0:SUBLANE, :] = jnp.where(pl.program_id(0) > 0, prev, 0.0)
    ext_ref[SUBLANE:SUBLANE + tm, :] = a
    y = (cb_ref[...] + cw_ref[0:1, :] * ext_ref[pl.ds(SUBLANE - 2, tm), :]
         + cw_ref[1:2, :] * ext_ref[pl.ds(SUBLANE - 1, tm), :] + cw_ref[2:3, :] * a)
    u_ref[...] = (jax.nn.gelu(y) * _dot(h, wb_ref[...])).astype(u_ref.dtype)
    tail_ref[...] = a[tm - SUBLANE:tm, :]


def ffn_up_prompt(h2, w_up_b, conv_w, conv_b, tm, tn):
    m, d = h2.shape
    ff = w_up_b.shape[1] // 2
    nj = ff // tn
    assert CONV_W == 3 and m % tm == 0 and ff % tn == 0
    return pl.pallas_call(
        functools.partial(_ffn_up_prompt_kernel, tm=tm),
        out_shape=(jax.ShapeDtypeStruct((m, ff), BF16), jax.ShapeDtypeStruct((m // tm, SUBLANE, ff), F32)),
        grid=(m // tm, nj),
        in_specs=[pl.BlockSpec((tm, d), lambda i, j: (i, 0)),
                  pl.BlockSpec((SUBLANE, d), lambda i, j: (jnp.maximum(i * (tm // SUBLANE) - 1, 0), 0)),
                  pl.BlockSpec((d, tn), lambda i, j: (0, j)),
                  pl.BlockSpec((d, tn), lambda i, j: (0, nj + j)),
                  pl.BlockSpec((CONV_W, tn), lambda i, j: (0, j)),
                  pl.BlockSpec((1, tn), lambda i, j: (0, j))],
        out_specs=(pl.BlockSpec((tm, tn), lambda i, j: (i, j)),
                   pl.BlockSpec((None, SUBLANE, tn), lambda i, j: (i, 0, j))),
        scratch_shapes=[pltpu.VMEM((tm + SUBLANE, tn), F32)],
        compiler_params=_cp("parallel", "parallel"),
        name="ffn_up_prompt",
    )(h2, h2, w_up_b, w_up_b, conv_w, conv_b.reshape(1, ff))


def _ffn_up_sample_kernel(h_ref, prev_ref, wa_ref, wb_ref, cw_ref, cb_ref, u_ref, tail_ref, ext_ref, *, db, m):
    h = h_ref[...]
    a = _dot(h, wa_ref[...])
    ext_ref[0:2 * db, :] = prev_ref[...]
    ext_ref[2 * db:2 * db + m, :] = a
    y = (cb_ref[...] + cw_ref[0:1, :] * ext_ref[0:m, :] + cw_ref[1:2, :] * ext_ref[db:db + m, :]
         + cw_ref[2:3, :] * a)
    u_ref[...] = (jax.nn.gelu(y) * _dot(h, wb_ref[...])).astype(u_ref.dtype)
    tail_ref[...] = ext_ref[m:m + 2 * db, :]


def ffn_up_sample(h2, prev, w_up_b, conv_w, conv_b, db, tn):
    m, d = h2.shape
    ff = w_up_b.shape[1] // 2
    nj = ff // tn
    assert CONV_W == 3 and m >= 2 * db and db % SUBLANE == 0
    return pl.pallas_call(
        functools.partial(_ffn_up_sample_kernel, db=db, m=m),
        out_shape=(jax.ShapeDtypeStruct((m, ff), BF16), jax.ShapeDtypeStruct((2 * db, ff), F32)),
        grid=(nj,),
        in_specs=[pl.BlockSpec((m, d), lambda j: (0, 0)),
                  pl.BlockSpec((2 * db, tn), lambda j: (0, j)),
                  pl.BlockSpec((d, tn), lambda j: (0, j)),
                  pl.BlockSpec((d, tn), lambda j: (0, nj + j)),
                  pl.BlockSpec((CONV_W, tn), lambda j: (0, j)),
                  pl.BlockSpec((1, tn), lambda j: (0, j))],
        out_specs=(pl.BlockSpec((m, tn), lambda j: (0, j)), pl.BlockSpec((2 * db, tn), lambda j: (0, j))),
        scratch_shapes=[pltpu.VMEM((m + 2 * db, tn), F32)],
        compiler_params=_cp("parallel"),
        name="ffn_up_sample",
    )(h2, prev, w_up_b, w_up_b, conv_w, conv_b.reshape(1, ff))


TN_IN = 768
TN_FF = 512
TK_DOWN = 1024


def _round_up(n, m):
    return -(-n // m) * m


def _pad_cols(a, n):
    return jnp.pad(a, ((0, 0), (0, n - a.shape[1])))


def _post(x, mix, mods, tm_row, tm_mm, w_out_b, g_post_mix, g_pre_ffn, ffn_up, w_down_b, g_post_ffn):
    f = matmul(mix, w_out_b, tm_mm, 1024)
    x2, h2 = post_mix(f, x, g_post_mix, g_pre_ffn, mods, tm_row)
    u, tail = ffn_up(h2)
    f2 = matmul(u, w_down_b, tm_mm, 2048, TK_DOWN)
    return post_ffn(f2, x2, g_post_ffn, mods, tm_row), tail


def kernel(x_prompt, x_sample, cache_kv, cache_win, state_ret, state_conv, page_table, c_prompt, c_sample,
           w_ada, b_ada, g_pre_mix, w_in, cmp_pe, cmp_w1, cmp_w2, ret_gn, w_out, g_post_mix, g_pre_ffn,
           w_up, conv_w, conv_b, w_down, g_post_ffn):
    depth = w_in.shape[0]
    bp, t, d = x_prompt.shape
    db, s, _ = x_sample.shape
    assert bp == 1
    h, dkv = state_ret.shape[2], state_ret.shape[3]
    n_phys, page, n_rows, g, hd = cache_kv.shape[1:]
    assert h == RET_HEADS and g == NSA_KV and n_rows == 4
    wb = cache_win.shape[2]
    ff = w_up.shape[2] // 2
    rw = h * dkv
    q_col = 4 * rw
    kv_col = q_col + NSA_HEADS * hd
    ng_col = kv_col + 6 * g * hd
    n_gate = NSA_HEADS * 3
    np_ = _round_up(ng_col + n_gate, TN_IN)
    ffp = _round_up(ff, max(TN_FF, TK_DOWN))
    gw = g * hd
    tm_p = min(1024, t)
    tm_s = s * db

    yp = x_prompt.reshape(t, d)
    ys = jnp.swapaxes(x_sample, 0, 1).reshape(s * db, d)
    c_all = jnp.concatenate([c_prompt, jnp.zeros((SUBLANE - 1, d), F32), c_sample], axis=0)
    outs = [[] for _ in range(8)]
    for l in range(depth):
        w_in_b = _pad_cols(w_in[l], np_).astype(BF16)
        w_out_b = w_out[l].astype(BF16)
        w_up_b = jnp.concatenate([_pad_cols(w_up[l][:, :ff], ffp), _pad_cols(w_up[l][:, ff:], ffp)],
                                 axis=1).astype(BF16)
        w_down_b = jnp.pad(w_down[l], ((0, ffp - ff), (0, 0))).astype(BF16)
        conv_w_p = _pad_cols(conv_w[l], ffp)
        conv_b_p = jnp.pad(conv_b[l], (0, ffp - ff))
        pe2 = cmp_pe[l].reshape(2, L_CMP // 2, 2 * hd)
        w1b = cmp_w1[l].reshape(2, L_CMP // 2, 2 * hd, -1).astype(BF16)
        w2b = cmp_w2[l].astype(BF16)
        post_w = (w_out_b, g_post_mix[l], g_pre_ffn[l])

        mods = ada_mods(c_all, w_ada[l], b_ada[l])
        mods_p = mods[0:SUBLANE]
        mods_s = mods[SUBLANE:SUBLANE + db]

        hp = norm_mod(yp, g_pre_mix[l], mods_p, 0, 1, 256)
        p = matmul(hp, w_in_b, tm_p, TN_IN)
        mix_ret, ret_state = retention_prompt(p, ret_gn[l], dkv)
        kcvc = compress_prompt(p, kv_col, pe2, w1b, w2b, hd)
        gates = jnp.swapaxes(p[:, ng_col:ng_col + n_gate].reshape(t, g, 3 * NSA_REP), 0, 1)
        part, unsel = cmp_win_select_prompt(p, q_col, kv_col, kcvc, gates)
        nsa_o = slc_prompt(p, q_col, kv_col, gates, part, unsel)
        mix = jnp.concatenate([mix_ret, nsa_o], axis=1)
        yp_new, tails = _post(yp, mix, mods_p, 256, tm_p, *post_w,
                              lambda h2: ffn_up_prompt(h2, w_up_b, conv_w_p, conv_b_p, tm_p, TN_FF),
                              w_down_b, g_post_ffn[l])
        wp = min(WINDOW, t)
        outs[0].append(p[:, kv_col:kv_col + 4 * gw].reshape(1, t, 4, g, hd))
        outs[1].append(p[t - wp:, kv_col + 4 * gw:kv_col + 6 * gw].reshape(1, wp, 2, g, hd))
        outs[2].append(ret_state[None])
        outs[3].append(tails[-1, SUBLANE - (CONV_W - 1):, :ff][None])
        yp = yp_new

        hs = norm_mod(ys, g_pre_mix[l], mods_s, 0, 1, db)
        ps_t = matmul(hs, w_in_b, tm_s, TN_IN)
        ps = jnp.swapaxes(ps_t.reshape(s, db, np_), 0, 1).reshape(db * s, np_)
        mix_ret_s, ret_s = retention_sample(ps, state_ret.reshape(depth * db, h, dkv, dkv), l * db, ret_gn[l], s)
        cache3 = cache_kv.reshape(depth * n_phys, page * KV_ROWS, hd)
        kcvc_s = compress_sample(cache3, l * n_phys, page_table, cmp_pe[l], cmp_w1[l], cmp_w2[l], hd)

        def head_rows(a, last):
            a = a.reshape(db, s, g, NSA_REP, last)
            return a.transpose(0, 2, 3, 1, 4).reshape(db, g, NSA_REP * s, last)

        q_s = head_rows(ps[:, q_col:kv_col], hd)
        gates_s = head_rows(ps[:, ng_col:ng_col + n_gate], 3)
        kv_new = jnp.pad(ps[:, kv_col:ng_col].reshape(db, s, 6 * gw), ((0, 0), (0, SUBLANE - s), (0, 0)))
        win_rows = ps[:, kv_col + 4 * gw:ng_col].reshape(db, s * WIN_ROWS, hd)
        o_s, win_new = nsa_sample(cache3, l * n_phys, page_table, q_s, gates_s, kcvc_s, kv_new, win_rows,
                                  cache_win.reshape(depth * db, wb * WIN_ROWS, hd), l * db, s)
        nsa_o_s = o_s.reshape(db, g, NSA_REP, s, hd).transpose(0, 3, 1, 2, 4).reshape(db * s, NSA_HEADS * hd)
        mix_s = jnp.concatenate([mix_ret_s, nsa_o_s], axis=1)
        mix_s = jnp.swapaxes(mix_s.reshape(db, s, -1), 0, 1).reshape(s * db, -1)
        prev = _pad_cols(jnp.swapaxes(state_conv[l], 0, 1).reshape((CONV_W - 1) * db, ff), ffp)
        ys_new, tail_s = _post(ys, mix_s, mods_s, db, tm_s, *post_w,
                               lambda h2: ffn_up_sample(h2, prev, w_up_b, conv_w_p, conv_b_p, db, TN_FF),
                               w_down_b, g_post_ffn[l])
        outs[4].append(ps[:, kv_col:kv_col + 4 * gw].reshape(db, s, 4, g, hd))
        outs[5].append(win_new.reshape(db, wb, 2, g, hd))
        outs[6].append(ret_s)
        outs[7].append(jnp.swapaxes(tail_s[:, :ff].reshape(CONV_W - 1, db, ff), 0, 1))
        ys = ys_new

    y_prompt = yp.reshape(1, t, d)
    y_sample = jnp.swapaxes(ys.reshape(s, db, d), 0, 1)
    return (y_prompt, y_sample) + tuple(jnp.stack(o) for o in outs)
```

```python
import functools

import jax
import jax.numpy as jnp
import numpy as np
from jax import lax
from jax.experimental import pallas as pl
from jax.experimental.pallas import tpu as pltpu

F32 = jnp.float32
BF16 = jnp.bfloat16

EPS = 1e-6
NEG = -1e30
FORCE = 1e4
RET_HEADS = 8
RET_CHUNK = 128
NSA_HEADS = 16
NSA_KV = 4
NSA_REP = NSA_HEADS // NSA_KV
L_CMP = 32
STRIDE_CMP = 16
L_SEL = 64
N_SELECT = 16
WINDOW = 512
CONV_W = 3

LANE = 128
SUBLANE = 8
VMEM_LIMIT = 56 * 1024 * 1024


def _cp(*sem):
    return pltpu.CompilerParams(dimension_semantics=sem, vmem_limit_bytes=VMEM_LIMIT)


def _silu(x):
    return x * jax.nn.sigmoid(x)


def _nt_dot(a, b):
    return lax.dot_general(a, b, (((1,), (1,)), ((), ())), preferred_element_type=F32)


def _tn_dot(a, b):
    return lax.dot_general(a, b, (((0,), (0,)), ((), ())), preferred_element_type=F32)


def _dot(a, b):
    return jnp.dot(a, b, preferred_element_type=F32)


def _ada_kernel(c_ref, w_ref, b_ref, o_ref):
    sc = _silu(c_ref[...]).astype(BF16)
    o_ref[...] = _dot(sc, w_ref[...].astype(BF16)) + b_ref[...]


def ada_mods(c_all, w_ada, b_ada, tn=512):
    m, d = c_all.shape
    n = w_ada.shape[1]
    return pl.pallas_call(
        _ada_kernel,
        out_shape=jax.ShapeDtypeStruct((m, n), F32),
        grid=(n // tn,),
        in_specs=[pl.BlockSpec((m, d), lambda j: (0, 0)),
                  pl.BlockSpec((d, tn), lambda j: (0, j)),
                  pl.BlockSpec((1, tn), lambda j: (0, j))],
        out_specs=pl.BlockSpec((m, tn), lambda j: (0, j)),
        compiler_params=_cp("arbitrary"),
        name="ada_mods",
    )(c_all, w_ada, b_ada.reshape(1, n))


def _rows(ref):
    v = ref[...]
    return v[0:1, :] if v.shape[0] == SUBLANE else v


def _rms(x, g):
    return x * lax.rsqrt(jnp.mean(x * x, axis=-1, keepdims=True) + EPS) * g


def _norm_mod_kernel(x_ref, g_ref, shift_ref, scale_ref, o_ref):
    y = _rms(x_ref[...], g_ref[...])
    o_ref[...] = (y * (1.0 + _rows(scale_ref)) + _rows(shift_ref)).astype(o_ref.dtype)


def _mod_spec(mods, tm, k, d):
    r = mods.shape[0]
    assert (r == SUBLANE) != (r == tm)
    return pl.BlockSpec((r, d), lambda i, k=k: (0, k))


def norm_mod(x, g, mods, k_shift, k_scale, tm):
    m, d = x.shape
    row = pl.BlockSpec((tm, d), lambda i: (i, 0))
    return pl.pallas_call(
        _norm_mod_kernel,
        out_shape=jax.ShapeDtypeStruct((m, d), BF16),
        grid=(m // tm,),
        in_specs=[row, pl.BlockSpec((1, d), lambda i: (0, 0)),
                  _mod_spec(mods, tm, k_shift, d), _mod_spec(mods, tm, k_scale, d)],
        out_specs=row,
        compiler_params=_cp("parallel"),
        name="norm_mod",
    )(x, g.reshape(1, d), mods, mods)


def _post_mix_kernel(f_ref, x_ref, g1_ref, g2_ref, gate_ref, shift_ref, scale_ref, x2_ref, h2_ref):
    x2 = x_ref[...] + _rows(gate_ref) * _rms(f_ref[...], g1_ref[...])
    x2_ref[...] = x2
    h2_ref[...] = (_rms(x2, g2_ref[...]) * (1.0 + _rows(scale_ref)) + _rows(shift_ref)).astype(h2_ref.dtype)


def post_mix(f, x, g_post_mix, g_pre_ffn, mods, tm):
    m, d = x.shape
    row = pl.BlockSpec((tm, d), lambda i: (i, 0))
    vec = pl.BlockSpec((1, d), lambda i: (0, 0))
    return pl.pallas_call(
        _post_mix_kernel,
        out_shape=(jax.ShapeDtypeStruct((m, d), F32), jax.ShapeDtypeStruct((m, d), BF16)),
        grid=(m // tm,),
        in_specs=[row, row, vec, vec, _mod_spec(mods, tm, 2, d), _mod_spec(mods, tm, 3, d),
                  _mod_spec(mods, tm, 4, d)],
        out_specs=(row, row),
        compiler_params=_cp("parallel"),
        name="post_mix",
    )(f, x, g_post_mix.reshape(1, d), g_pre_ffn.reshape(1, d), mods, mods, mods)


def _post_ffn_kernel(f_ref, x_ref, g_ref, gate_ref, y_ref):
    y_ref[...] = x_ref[...] + _rows(gate_ref) * _rms(f_ref[...], g_ref[...])


def post_ffn(f, x2, g_post_ffn, mods, tm):
    m, d = x2.shape
    row = pl.BlockSpec((tm, d), lambda i: (i, 0))
    return pl.pallas_call(
        _post_ffn_kernel,
        out_shape=jax.ShapeDtypeStruct((m, d), F32),
        grid=(m // tm,),
        in_specs=[row, row, pl.BlockSpec((1, d), lambda i: (0, 0)), _mod_spec(mods, tm, 5, d)],
        out_specs=row,
        compiler_params=_cp("parallel"),
        name="post_ffn",
    )(f, x2, g_post_ffn.reshape(1, d), mods)


def _mm_kernel(a_ref, w_ref, o_ref, *, nk):
    acc = _dot(a_ref[...], w_ref[...])
    if nk == 1:
        o_ref[...] = acc
    else:
        k = pl.program_id(2)

        @pl.when(k == 0)
        def _():
            o_ref[...] = acc

        @pl.when(k > 0)
        def _():
            o_ref[...] += acc


def matmul(a, w, tm, tn, tk=None):
    m, kd = a.shape
    n = w.shape[1]
    tk = kd if tk is None else tk
    nk = kd // tk
    assert m % tm == 0 and n % tn == 0 and kd % tk == 0
    return pl.pallas_call(
        functools.partial(_mm_kernel, nk=nk),
        out_shape=jax.ShapeDtypeStruct((m, n), F32),
        grid=(m // tm, n // tn, nk),
        in_specs=[pl.BlockSpec((tm, tk), lambda i, j, k: (i, k)),
                  pl.BlockSpec((tk, tn), lambda i, j, k: (k, j))],
        out_specs=pl.BlockSpec((tm, tn), lambda i, j, k: (i, j)),
        compiler_params=_cp("parallel", "parallel", "arbitrary"),
        name="matmul",
    )(a, w)


def _ret_out(o, gate, gn):
    mu = jnp.mean(o, axis=-1, keepdims=True)
    var = jnp.mean(jnp.square(o - mu), axis=-1, keepdims=True)
    return (o - mu) * lax.rsqrt(var + EPS) * gn * _silu(gate)


def _ret_prompt_kernel(q_ref, k_ref, v_ref, gate_ref, dmat_ref, dq_ref, dk_ref, gc_ref, gn_ref,
                       o_ref, st_ref, *, dk_scale):
    @pl.when(pl.program_id(1) == 0)
    def _():
        st_ref[...] = jnp.zeros_like(st_ref)

    k = k_ref[...] * dk_scale
    qb = q_ref[...].astype(BF16)
    vb = v_ref[...].astype(BF16)
    attn = _nt_dot(qb, k.astype(BF16)) * dmat_ref[...]
    st = st_ref[...]
    o = _dot(attn.astype(BF16), vb) + _dot(qb, st.astype(BF16)) * dq_ref[...]
    kdec = (k * dk_ref[...]).astype(BF16)
    st_ref[...] = gc_ref[...] * st + _tn_dot(kdec, vb)
    o_ref[...] = _ret_out(o, gate_ref[...], gn_ref[...]).astype(o_ref.dtype)


def _ret_tables(c):
    log_g = jnp.log(1.0 - jnp.exp2(-5.0 - jnp.arange(RET_HEADS, dtype=F32)))
    i = jnp.arange(c, dtype=F32)
    diff = i[:, None] - i[None, :]
    dmat = jnp.where(diff >= 0, jnp.exp(log_g[:, None, None] * jnp.maximum(diff, 0.0)), 0.0)
    dq = jnp.exp(log_g[:, None] * (i[None, :] + 1.0))[:, :, None]
    dk = jnp.exp(log_g[:, None] * (c - 1.0 - i[None, :]))[:, :, None]
    gc = jnp.exp(log_g * c)[:, None, None]
    return dmat, dq, dk, gc


def retention_prompt(p, ret_gn, dkv):
    t = p.shape[0]
    c = RET_CHUNK
    h = RET_HEADS
    dmat, dq, dk, gc = _ret_tables(c)

    def col(part):
        return pl.BlockSpec((c, dkv), lambda hh, cc, part=part: (cc, part * h + hh))

    def tab(shape):
        return pl.BlockSpec((None,) + shape, lambda hh, cc: (hh, 0, 0))

    return pl.pallas_call(
        functools.partial(_ret_prompt_kernel, dk_scale=dkv ** -0.5),
        out_shape=(jax.ShapeDtypeStruct((t, h * dkv), BF16), jax.ShapeDtypeStruct((h, dkv, dkv), F32)),
        grid=(h, t // c),
        in_specs=[col(0), col(1), col(2), col(3), tab((c, c)), tab((c, 1)), tab((c, 1)), tab((1, 1)),
                  pl.BlockSpec((1, dkv), lambda hh, cc: (0, hh))],
        out_specs=(pl.BlockSpec((c, dkv), lambda hh, cc: (cc, hh)),
                   pl.BlockSpec((None, dkv, dkv), lambda hh, cc: (hh, 0, 0))),
        compiler_params=_cp("parallel", "arbitrary"),
        name="retention_prompt",
    )(p, p, p, p, dmat, dq, dk, gc, ret_gn.reshape(1, h * dkv))


def _ret_sample_kernel(q_ref, k_ref, v_ref, gate_ref, st_ref, dmat_ref, dq_ref, dk_ref, gc_ref, gn_ref,
                       o_ref, nst_ref, *, dkv, s, dk_scale):
    rows = 2 * s
    rid = lax.broadcasted_iota(jnp.int32, (rows, 1), 0)
    first = rid < s
    for h in range(RET_HEADS):
        sl = slice(h * dkv, (h + 1) * dkv)
        q = q_ref[:, sl]
        k = k_ref[:, sl] * dk_scale
        v = v_ref[:, sl]
        attn = _nt_dot(q, k) * dmat_ref[h]
        o = _dot(attn, v)
        dq = dq_ref[h]
        kdec = k * dk_ref[h]
        for b in range(2):
            st = st_ref[b, h]
            mine = first if b == 0 else jnp.logical_not(first)
            o = o + jnp.where(mine, _dot(q, st) * dq, 0.0)
            nst_ref[b, h] = gc_ref[h] * st + _tn_dot(jnp.where(mine, kdec, 0.0), v)
        o_ref[:, sl] = _ret_out(o, gate_ref[:, sl], gn_ref[:, sl]).astype(o_ref.dtype)


def retention_sample(p, state, b0, ret_gn, s):
    _, h, dkv, _ = state.shape
    db = p.shape[0] // s
    rows = 2 * s
    assert rows == SUBLANE and db % 2 == 0
    log_g = jnp.log(1.0 - jnp.exp2(-5.0 - jnp.arange(h, dtype=F32)))
    i = jnp.arange(rows)
    tok = (i % s).astype(F32)
    same = (i[:, None] // s) == (i[None, :] // s)
    diff = tok[:, None] - tok[None, :]
    dmat = jnp.where(same & (diff >= 0), jnp.exp(log_g[:, None, None] * jnp.maximum(diff, 0.0)), 0.0)
    dq = jnp.exp(log_g[:, None] * (tok[None, :] + 1.0))[:, :, None]
    dk = jnp.exp(log_g[:, None] * (s - 1.0 - tok[None, :]))[:, :, None]
    gc = jnp.exp(log_g * s)[:, None, None]
    w = h * dkv

    def col(part):
        return pl.BlockSpec((rows, w), lambda i, part=part: (i, part))

    def full(a):
        return pl.BlockSpec(a.shape, lambda i: (0,) * a.ndim)

    assert b0 % 2 == 0
    st_block = (2, h, dkv, dkv)
    return pl.pallas_call(
        functools.partial(_ret_sample_kernel, dkv=dkv, s=s, dk_scale=dkv ** -0.5),
        out_shape=(jax.ShapeDtypeStruct((db * s, w), BF16), jax.ShapeDtypeStruct((db, h, dkv, dkv), F32)),
        grid=(db // 2,),
        in_specs=[col(0), col(1), col(2), col(3), pl.BlockSpec(st_block, lambda i: (b0 // 2 + i, 0, 0, 0)),
                  full(dmat), full(dq), full(dk), full(gc), pl.BlockSpec((1, w), lambda i: (0, 0))],
        out_specs=(pl.BlockSpec((rows, w), lambda i: (i, 0)), pl.BlockSpec(st_block, lambda i: (i, 0, 0, 0))),
        compiler_params=_cp("parallel"),
        name="retention_sample",
    )(p, p, p, p, state, dmat, dq, dk, gc, ret_gn.reshape(1, w))


def _compress_block(load_rows, pe_ref, w1_ref, w2, accb_ref, nh):
    pairs = L_CMP // 4
    acc_a = None
    acc_b = None
    for i in range(pairs):
        y = jnp.concatenate([load_rows(2 * i), load_rows(2 * i + 1)], axis=1)
        da = _dot((y + pe_ref[i:i + 1, :]).astype(BF16), w1_ref[i])
        db = _dot((y + pe_ref[pairs + i:pairs + i + 1, :]).astype(BF16), w1_ref[pairs + i])
        acc_a = da if acc_a is None else acc_a + da
        acc_b = db if acc_b is None else acc_b + db
    accb_ref[0:nh, :] = acc_b
    accb_ref[nh:nh + SUBLANE, :] = jnp.zeros((SUBLANE, acc_b.shape[1]), F32)
    h = jax.nn.gelu(acc_a + accb_ref[pl.ds(1, nh), :])
    return _dot(h.astype(BF16), w2)


def _masked_softmax(s, mask):
    s = jnp.where(mask, s, NEG)
    e = jnp.exp(s - jnp.max(s, axis=-1, keepdims=True))
    return jnp.where(mask, e / jnp.sum(e, axis=-1, keepdims=True), 0.0)


def _importance(psum, ov):
    hi = psum.astype(BF16)
    lo = (psum - hi.astype(F32)).astype(BF16)
    return _dot(hi, ov) + _dot(lo, ov)


def _select_blocks_t(imp_t, q_pos, n_sel):
    nb = imp_t.shape[0]
    blk = lax.broadcasted_iota(jnp.int32, (nb, 1), 0)
    blkf = blk.astype(F32)
    cur = q_pos // L_SEL
    forced = (blk == 0) | (blk == cur) | (blk == cur - 1)
    score = jnp.where(blk > cur, -1.0, jnp.where(forced, FORCE, imp_t))
    score = jnp.where(blk >= n_sel, -2.0, score)
    taken = -3.0
    work = score
    for _ in range(min(N_SELECT, n_sel)):
        m = jnp.max(work, axis=0, keepdims=True)
        first = jnp.min(jnp.where(work == m, blkf, float(nb)), axis=0, keepdims=True)
        work = jnp.where(blkf == first, taken, work)
    return jnp.where((work == taken) & (score >= 0.0), 1.0, 0.0)


def _overlap_matrix(nc, nb):
    n = jnp.arange(nc)[:, None]
    j = jnp.arange(nb)[None, :]
    c_start = n * STRIDE_CMP
    c_end = c_start + L_CMP - 1
    return ((c_start < (j + 1) * L_SEL) & (c_end >= j * L_SEL)).astype(BF16)


def _expand_matrix(nb, nkeys):
    return (jnp.arange(nb)[:, None] == (jnp.arange(nkeys)[None, :] // L_SEL)).astype(BF16)


def _slopes():
    h = jnp.arange(1, NSA_HEADS + 1, dtype=F32)
    return jnp.exp2(-8.0 * h / NSA_HEADS).reshape(NSA_KV, NSA_REP)


def _compress_prompt_kernel(x_ref, pe_ref, w1_ref, w2_ref, o_ref, accb_ref):
    nh = x_ref.shape[0] // STRIDE_CMP
    o_ref[...] = _compress_block(lambda l: x_ref[pl.ds(l, nh, stride=STRIDE_CMP), :],
                                 pe_ref, w1_ref, w2_ref[...], accb_ref, nh)


def compress_prompt(p, kv_col, cmp_pe, w1b, w2b, hd):
    t = p.shape[0]
    nh = t // STRIDE_CMP
    hid = w1b.shape[-1]
    cb = kv_col // hd
    return pl.pallas_call(
        _compress_prompt_kernel,
        out_shape=jax.ShapeDtypeStruct((2, NSA_KV, nh, hd), F32),
        grid=(2, NSA_KV),
        in_specs=[pl.BlockSpec((t, hd), lambda kind, g: (0, cb + kind * NSA_KV + g)),
                  pl.BlockSpec((None, L_CMP // 2, 2 * hd), lambda kind, g: (kind, 0, 0)),
                  pl.BlockSpec((None, L_CMP // 2, 2 * hd, hid), lambda kind, g: (kind, 0, 0, 0)),
                  pl.BlockSpec((None, hid, hd), lambda kind, g: (kind, 0, 0))],
        out_specs=pl.BlockSpec((None, None, nh, hd), lambda kind, g: (kind, g, 0, 0)),
        scratch_shapes=[pltpu.VMEM((nh + SUBLANE, hid), F32)],
        compiler_params=_cp("parallel", "parallel"),
        name="compress_prompt",
    )(p, cmp_pe, w1b, w2b)


MASK_BIG = 2.0 ** 99
LOG2E = 1.4426950408889634


def _cmp_win_select_kernel(q_ref, kc_ref, vc_ref, ov_ref, sl_ref, gate_ref, *rest, tq, hd, n_sel, n_prev, scale):
    kw_refs = rest[:n_prev + 1]
    vw_refs = rest[n_prev + 1:2 * n_prev + 2]
    part_ref, unsel_ref = rest[2 * n_prev + 2:]
    qt = pl.program_id(1)
    nc = kc_ref.shape[0]
    nw = (n_prev + 1) * tq
    q_pos = qt * tq + lax.broadcasted_iota(jnp.int32, (tq, 1), 0)
    dist_c = q_pos - (lax.broadcasted_iota(jnp.int32, (1, nc), 1) * STRIDE_CMP + (L_CMP - 1))
    mask_c = dist_c >= 0
    dist_cf = dist_c.astype(F32)
    kcb = kc_ref[...].astype(BF16)
    vcb = vc_ref[...].astype(BF16)
    w_pos = (qt - n_prev) * tq + lax.broadcasted_iota(jnp.int32, (1, nw), 1)
    dist_w = q_pos - w_pos
    mask_w = (dist_w >= 0) & (dist_w <= WINDOW) & (w_pos >= 0)
    dist_wf = dist_w.astype(F32)
    kwb = jnp.concatenate([r[...].astype(BF16) for r in kw_refs], axis=0)
    vwb = jnp.concatenate([r[...].astype(BF16) for r in vw_refs], axis=0)
    gate = jax.nn.sigmoid(gate_ref[...])
    heads = range(NSA_REP)
    qbs = [q_ref[:, r * hd:(r + 1) * hd].astype(BF16) for r in heads]
    s_cs = [_nt_dot(qbs[r], kcb) for r in heads]
    s_ws = [_nt_dot(qbs[r], kwb) for r in heads]
    p_cs = [_masked_softmax(s_cs[r] * scale - sl_ref[r][:, 0:1] * dist_cf, mask_c) for r in heads]
    p_ws = [_masked_softmax(s_ws[r] * scale - sl_ref[r][:, 0:1] * dist_wf, mask_w) for r in heads]
    psum = sum(p_cs[1:], p_cs[0])
    for r in heads:
        part_ref[:, r * hd:(r + 1) * hd] = (gate[:, 3 * r:3 * r + 1] * _dot(p_cs[r].astype(BF16), vcb)
                                            + gate[:, 3 * r + 2:3 * r + 3] * _dot(p_ws[r].astype(BF16), vwb))
    q_pos_row = qt * tq + lax.broadcasted_iota(jnp.int32, (1, tq), 1)
    picked_t = _select_blocks_t(_importance(psum, ov_ref[...]).T, q_pos_row, n_sel)
    unsel_ref[...] = (picked_t - 1.0) * MASK_BIG


def cmp_win_select_prompt(p, q_col, kv_col, kcvc, gates, tq=256):
    t = p.shape[0]
    nc, hd = kcvc.shape[2:]
    n_sel = -(-t // L_SEL)
    tq = min(tq, t)
    assert N_SELECT <= n_sel <= LANE and WINDOW % tq == 0 and t % tq == 0
    n_prev = WINDOW // tq
    w = NSA_REP * hd
    cb = kv_col // hd
    slopes = jnp.broadcast_to(_slopes()[:, :, None, None], (NSA_KV, NSA_REP, 1, LANE))

    def win_specs(kind):
        return [pl.BlockSpec((tq, hd), lambda g, i, j=j: (jnp.maximum(i - j, 0), cb + kind * NSA_KV + g))
                for j in range(n_prev, -1, -1)]

    return pl.pallas_call(
        functools.partial(_cmp_win_select_kernel, tq=tq, hd=hd, n_sel=n_sel, n_prev=n_prev, scale=hd ** -0.5),
        out_shape=(jax.ShapeDtypeStruct((t, NSA_KV * w), F32), jax.ShapeDtypeStruct((NSA_KV, LANE, t), F32)),
        grid=(NSA_KV, t // tq),
        in_specs=[pl.BlockSpec((tq, w), lambda g, i: (i, q_col // w + g)),
                  pl.BlockSpec((None, None, nc, hd), lambda g, i: (0, g, 0, 0)),
                  pl.BlockSpec((None, None, nc, hd), lambda g, i: (1, g, 0, 0)),
                  pl.BlockSpec((nc, LANE), lambda g, i: (0, 0)),
                  pl.BlockSpec((None, NSA_REP, 1, LANE), lambda g, i: (g, 0, 0, 0)),
                  pl.BlockSpec((None, tq, 3 * NSA_REP), lambda g, i: (g, i, 0))]
        + win_specs(4) + win_specs(5),
        out_specs=(pl.BlockSpec((tq, w), lambda g, i: (i, g)),
                   pl.BlockSpec((None, LANE, tq), lambda g, i: (g, 0, i))),
        compiler_params=_cp("parallel", "parallel"),
        name="cmp_win_select_prompt",
    )(p, kcvc, kcvc, _overlap_matrix(nc, LANE), slopes, gates, *([p] * (2 * n_prev + 2)))


def _slc_kernel(qi_ref, ki_ref, q_ref, unsel_ref, al_ref, k_ref, vt_ref, kf_ref, gate_ref, part_ref,
                o_ref, qa_ref, al_sc, m_ref, l_ref, acc_ref, *, tq, tk, hd, scale):
    step = pl.program_id(1)
    qt = qi_ref[step]
    kt = ki_ref[step]

    cols = [slice(r * tq, (r + 1) * tq) for r in range(NSA_REP)]

    @pl.when(kt == 0)
    def _():
        q_pos = (qt * tq + lax.broadcasted_iota(jnp.int32, (1, tq), 1)).astype(F32)
        sub = lax.broadcasted_iota(jnp.int32, (SUBLANE, 1), 0)
        for r in range(NSA_REP):
            qa_ref[0:hd, cols[r]] = (q_ref[:, r * hd:(r + 1) * hd] * (scale * LOG2E)).T.astype(BF16)
            tab = jnp.concatenate([al_ref[r]] * (tq // LANE), axis=1)
            slope = tab[ALIBI_SLOPE:ALIBI_SLOPE + 1, :]
            al_sc[:, cols[r]] = jnp.where(sub == ALIBI_ROWCONST, -slope * q_pos,
                                          jnp.where(sub == ALIBI_SLOPE, 0.0, tab))
        qa_ref[hd + FEAT:2 * hd, :] = jnp.zeros((hd - FEAT, NSA_REP * tq), BF16)
        m_ref[...] = jnp.full(m_ref.shape, M_FLOOR, F32)
        l_ref[...] = jnp.zeros(l_ref.shape, F32)
        acc_ref[...] = jnp.zeros(acc_ref.shape, F32)

    picks = unsel_ref[pl.ds(pl.multiple_of(kt * (tk // L_SEL), SUBLANE), tk // L_SEL), :]
    feat = jnp.concatenate([jnp.concatenate([picks] * NSA_REP, axis=1), al_sc[...]], axis=0)
    qa_ref[hd:hd + FEAT, :] = feat.astype(BF16)
    ka = jnp.concatenate([k_ref[...].astype(BF16), kf_ref[...]], axis=1)
    vtb = vt_ref[...].astype(BF16)
    last = (qt * tq + tq - 1) // tk

    def update(scores):
        for r in range(NSA_REP):
            s = scores[r]
            m_old = m_ref[:, cols[r]]
            m_new = jnp.maximum(m_old, jnp.max(s, axis=0, keepdims=True))
            alpha = jnp.exp2(m_old - m_new)
            e = jnp.exp2(s - m_new)
            l_ref[:, cols[r]] = alpha * l_ref[:, cols[r]] + jnp.sum(e, axis=0, keepdims=True)
            acc_ref[:, cols[r]] = alpha * acc_ref[:, cols[r]] + _dot(vtb, e.astype(BF16))
            m_ref[:, cols[r]] = m_new

    @pl.when(kt != last)
    def _():
        update([_dot(ka, qa_ref[:, cols[r]]) for r in range(NSA_REP)])

    @pl.when(kt == last)
    def _():
        causal = (kt * tk + lax.broadcasted_iota(jnp.int32, (tk, 1), 0)
                  <= qt * tq + lax.broadcasted_iota(jnp.int32, (1, tq), 1))
        gate = jax.nn.sigmoid(gate_ref[...])
        update([jnp.where(causal, _dot(ka, qa_ref[:, cols[r]]), NEG) for r in range(NSA_REP)])
        for r in range(NSA_REP):
            sl = slice(r * hd, (r + 1) * hd)
            o_s = (acc_ref[:, cols[r]] / l_ref[:, cols[r]]).T
            o_ref[:, sl] = (part_ref[:, sl] + gate[:, 3 * r + 1:3 * r + 2] * o_s).astype(o_ref.dtype)


FEAT = 2 * SUBLANE
ALIBI_ROWCONST = 6
ALIBI_SLOPE = 7
M_FLOOR = -1e29


def _alibi_tables(t, tk):
    slope = _slopes() * LOG2E
    s1 = slope.astype(BF16).astype(F32)
    s2 = (slope - s1).astype(BF16).astype(F32)
    s3 = (slope - s1 - s2).astype(BF16).astype(F32)
    zero = jnp.zeros_like(slope)
    qf = jnp.stack([s1, s2, s3, s1, s2, s3, zero, slope], axis=-1)
    qf = jnp.broadcast_to(qf[:, :, :, None], qf.shape + (LANE,))
    pos = jnp.arange(t)
    blk = (pos // L_SEL) % (tk // L_SEL)
    onehot = (blk[:, None] == jnp.arange(SUBLANE)[None, :]).astype(F32)
    hi = (pos // L_SEL * L_SEL).astype(F32)[:, None]
    lo = (pos % L_SEL).astype(F32)[:, None]
    one = jnp.ones((t, 1), F32)
    kf = jnp.concatenate([onehot, hi, hi, hi, lo, lo, lo, one, jnp.zeros((t, LANE - FEAT + 1), F32)], axis=1)
    return qf, kf.astype(BF16)


def slc_prompt(p, q_col, kv_col, gates, part, unsel_t, tq=256, tk=512):
    t = p.shape[0]
    hd = part.shape[1] // NSA_HEADS
    w = NSA_REP * hd
    tq, tk = min(tq, t), min(tk, t)
    assert t % tq == 0 and t % tk == 0 and tk % tq == 0 and tk == SUBLANE * L_SEL and hd == LANE
    qi, ki = [], []
    for i in range(t // tq):
        for k in range((i * tq + tq - 1) // tk + 1):
            qi.append(i)
            ki.append(k)
    qi, ki = jnp.asarray(qi, jnp.int32), jnp.asarray(ki, jnp.int32)
    cb = kv_col // hd
    al_q, kfeat = _alibi_tables(t, tk)
    v_t = jnp.swapaxes(p[:, kv_col + 3 * NSA_KV * hd:kv_col + 4 * NSA_KV * hd].reshape(t, NSA_KV, hd), 0, 1)
    v_t = jnp.swapaxes(v_t, 1, 2)
    qrow = lambda g, s, qi, ki: (qi[s], g)
    rows = NSA_REP * tq
    return pl.pallas_call(
        functools.partial(_slc_kernel, tq=tq, tk=tk, hd=hd, scale=hd ** -0.5),
        out_shape=jax.ShapeDtypeStruct((t, NSA_KV * w), BF16),
        grid_spec=pltpu.PrefetchScalarGridSpec(
            num_scalar_prefetch=2,
            grid=(NSA_KV, int(qi.shape[0])),
            in_specs=[pl.BlockSpec((tq, w), lambda g, s, qi, ki: (qi[s], q_col // w + g)),
                      pl.BlockSpec((None, LANE, tq), lambda g, s, qi, ki: (g, 0, qi[s])),
                      pl.BlockSpec((None, NSA_REP, SUBLANE, LANE), lambda g, s, qi, ki: (g, 0, 0, 0)),
                      pl.BlockSpec((tk, hd), lambda g, s, qi, ki: (ki[s], cb + 2 * NSA_KV + g)),
                      pl.BlockSpec((None, hd, tk), lambda g, s, qi, ki: (g, 0, ki[s])),
                      pl.BlockSpec((tk, LANE), lambda g, s, qi, ki: (ki[s], 0)),
                      pl.BlockSpec((None, tq, 3 * NSA_REP), lambda g, s, qi, ki: (g, qi[s], 0)),
                      pl.BlockSpec((tq, w), qrow)],
            out_specs=pl.BlockSpec((tq, w), qrow),
            scratch_shapes=[pltpu.VMEM((2 * hd, rows), BF16), pltpu.VMEM((SUBLANE, rows), F32),
                            pltpu.VMEM((1, rows), F32), pltpu.VMEM((1, rows), F32),
                            pltpu.VMEM((hd, rows), F32)]),
        compiler_params=_cp("parallel", "arbitrary"),
        name="slc_prompt",
    )(qi, ki, p, unsel_t, al_q, p, v_t, kfeat, gates, part)


KV_ROWS = 4 * NSA_KV
WIN_ROWS = 2 * NSA_KV


def _page_specs(n_pages, rows, hd, page0):
    return [pl.BlockSpec((None, rows, hd), lambda b, pt, j=j: (page0 + pt[b * n_pages + j], 0, 0))
            for j in range(n_pages)]


def _rows_by_head(ref, first, rows_per_pos):
    n = ref.shape[0] // rows_per_pos
    x = ref[...].reshape(n, rows_per_pos, ref.shape[1])[:, first:first + SUBLANE, :]
    return jnp.swapaxes(x, 0, 1).astype(BF16)


def _compress_sample_kernel(pt_ref, *refs, n_pages, hd):
    page_refs = refs[:n_pages]
    pe_ref, w1_ref, w2_ref, o_ref, acc_ref, accb_ref = refs[n_pages:]
    page = page_refs[0].shape[0] // KV_ROWS
    per_page = page // STRIDE_CMP
    nh = n_pages * per_page
    pairs = L_CMP // 4
    rows = NSA_KV * nh

    def rows_by_head(l):
        x = jnp.concatenate(
            [pr[pl.ds((m * STRIDE_CMP + l) * KV_ROWS, WIN_ROWS), :] for pr in page_refs for m in range(per_page)],
            axis=0)
        return jnp.swapaxes(x.reshape(nh, WIN_ROWS, hd), 0, 1)

    for i in range(pairs):
        ya, yb = rows_by_head(2 * i), rows_by_head(2 * i + 1)
        for kind in range(2):
            sl = slice(kind * NSA_KV, (kind + 1) * NSA_KV)
            y = jnp.concatenate([ya[sl].reshape(rows, hd), yb[sl].reshape(rows, hd)], axis=1)
            for part in range(2):
                j = part * pairs + i
                d = _dot((y + pe_ref[kind, j:j + 1, :]).astype(BF16), w1_ref[kind, j])
                if i == 0:
                    acc_ref[kind, part] = d
                else:
                    acc_ref[kind, part] += d
    for kind in range(2):
        accb_ref[0:rows, :] = acc_ref[kind, 1]
        accb_ref[rows:rows + SUBLANE, :] = jnp.zeros((SUBLANE, accb_ref.shape[1]), F32)
        h = jax.nn.gelu(acc_ref[kind, 0] + accb_ref[pl.ds(1, rows), :])
        o_ref[kind * NSA_KV:(kind + 1) * NSA_KV] = _dot(h.astype(BF16), w2_ref[kind]).reshape(NSA_KV, nh, hd)


def compress_sample(cache3, page0, page_table, pe2, w1b, w2b, hd):
    db, n_pages = page_table.shape
    page = cache3.shape[1] // KV_ROWS
    nh = n_pages * page // STRIDE_CMP
    hid = w1b.shape[-1]
    assert L_CMP == 2 * STRIDE_CMP
    full = lambda a: pl.BlockSpec(a.shape, lambda b, pt: (0,) * a.ndim)
    return pl.pallas_call(
        functools.partial(_compress_sample_kernel, n_pages=n_pages, hd=hd),
        out_shape=jax.ShapeDtypeStruct((db, 2 * NSA_KV, nh, hd), F32),
        grid_spec=pltpu.PrefetchScalarGridSpec(
            num_scalar_prefetch=1,
            grid=(db,),
            in_specs=_page_specs(n_pages, page * KV_ROWS, hd, page0) + [full(pe2), full(w1b), full(w2b)],
            out_specs=pl.BlockSpec((None, 2 * NSA_KV, nh, hd), lambda b, pt: (b, 0, 0, 0)),
            scratch_shapes=[pltpu.VMEM((2, 2, NSA_KV * nh, hid), F32),
                            pltpu.VMEM((NSA_KV * nh + SUBLANE, hid), F32)]),
        compiler_params=_cp("parallel"),
        name="compress_sample",
    )(page_table.reshape(-1), *([cache3] * n_pages), pe2, w1b, w2b)


def _softmax2(s_a, mask_a, s_b, mask_b):
    s_a = jnp.where(mask_a, s_a, NEG)
    s_b = jnp.where(mask_b, s_b, NEG)
    m = jnp.maximum(jnp.max(s_a, axis=-1, keepdims=True), jnp.max(s_b, axis=-1, keepdims=True))
    e_a = jnp.where(mask_a, jnp.exp(s_a - m), 0.0)
    e_b = jnp.where(mask_b, jnp.exp(s_b - m), 0.0)
    return e_a, e_b, jnp.sum(e_a, axis=-1, keepdims=True) + jnp.sum(e_b, axis=-1, keepdims=True)


def _nsa_sample_kernel(pt_ref, *refs, n_pages, hd, s_new, scale):
    page_refs = refs[:n_pages]
    (q_ref, gate_ref, kcvc_ref, new_ref, nw_ref, win_ref, ov_ref, e_ref, rm_ref, sl_ref, o_ref, wn_ref,
     imp_ref, kv_ref, kvw_ref) = refs[n_pages:]
    page = page_refs[0].shape[0] // KV_ROWS
    past = n_pages * page
    wb = win_ref.shape[0] // WIN_ROWS
    gw = NSA_KV * hd
    rows = NSA_REP * s_new
    nc = kcvc_ref.shape[1]
    for j, pr in enumerate(page_refs):
        kv_ref[:, j * page:(j + 1) * page, :] = _rows_by_head(pr, 2 * NSA_KV, KV_ROWS)
    kvw_ref[...] = _rows_by_head(win_ref, 0, WIN_ROWS)
    n_cmp = (past + s_new - L_CMP) // STRIDE_CMP + 1
    n_sel = -(-(past + s_new) // L_SEL)
    tok = lax.broadcasted_iota(jnp.int32, (rows, 1), 0) % s_new
    q_pos = past + tok
    tnew = lax.broadcasted_iota(jnp.int32, (1, SUBLANE), 1)
    dist_new = tok - tnew
    mask_new = (dist_new >= 0) & (tnew < s_new)
    n = lax.broadcasted_iota(jnp.int32, (1, nc), 1)
    dist_c = q_pos - (n * STRIDE_CMP + (L_CMP - 1))
    mask_c = (dist_c >= 0) & (n < n_cmp)
    imp_ref[...] = jnp.zeros(imp_ref.shape, F32)
    o_cs = []
    for g in range(NSA_KV):
        s_c = (_nt_dot(q_ref[g].astype(BF16), kcvc_ref[g].astype(BF16)) * scale
               - sl_ref[g] * dist_c.astype(F32))
        p_c = _masked_softmax(s_c, mask_c)
        o_cs.append(_dot(p_c.astype(BF16), kcvc_ref[NSA_KV + g].astype(BF16)))
        imp_ref[g * rows:(g + 1) * rows, :] = _importance(_dot(rm_ref[...], p_c), ov_ref[...])
    lane_pos = past + lax.broadcasted_iota(jnp.int32, (1, LANE), 1) % s_new
    picked_all = _select_blocks_t(imp_ref[...].T, lane_pos, n_sel).T
    groups = range(NSA_KV)
    dist_s = q_pos - lax.broadcasted_iota(jnp.int32, (1, past), 1)
    dist_w = wb + tok - lax.broadcasted_iota(jnp.int32, (1, wb), 1)
    mask_w = (dist_w >= 0) & (dist_w <= WINDOW)

    def new_rows(kind, g):
        return new_ref[:, kind * gw + g * hd:kind * gw + (g + 1) * hd]

    def scores(q, keys, dist, g):
        return _nt_dot(q, keys) * scale - sl_ref[g] * dist.astype(F32)

    qs = [q_ref[g].astype(BF16) for g in groups]
    picked = [picked_all[g * rows:(g + 1) * rows, :] for g in groups]
    key_masks = [(_dot(picked[g].astype(BF16), e_ref[...]) > 0.5) & (dist_s >= 0) for g in groups]
    cur_masks = [mask_new & (picked[g][:, past // L_SEL:past // L_SEL + 1] > 0.5) for g in groups]
    s_s = [scores(qs[g], kv_ref[g], dist_s, g) for g in groups]
    s_n = [scores(q_ref[g], new_rows(2, g), dist_new, g) for g in groups]
    s_w = [scores(qs[g], kvw_ref[g], dist_w, g) for g in groups]
    s_wn = [scores(q_ref[g], new_rows(4, g), dist_new, g) for g in groups]
    sm_s = [_softmax2(s_s[g], key_masks[g], s_n[g], cur_masks[g]) for g in groups]
    sm_w = [_softmax2(s_w[g], mask_w, s_wn[g], mask_new) for g in groups]
    for g in groups:
        e_s, e_n, l_s = sm_s[g]
        e_w, e_wn, l_w = sm_w[g]
        o_s = (_dot(e_s.astype(BF16), kv_ref[NSA_KV + g]) + _dot(e_n, new_rows(3, g))) / l_s
        o_w = (_dot(e_w.astype(BF16), kvw_ref[NSA_KV + g]) + _dot(e_wn, new_rows(5, g))) / l_w
        gate = jax.nn.sigmoid(gate_ref[g])
        o_ref[g] = (gate[:, 0:1] * o_cs[g] + gate[:, 1:2] * o_s + gate[:, 2:3] * o_w).astype(o_ref.dtype)
    keep = (wb - s_new) * WIN_ROWS
    wn_ref[0:keep, :] = win_ref[s_new * WIN_ROWS:wb * WIN_ROWS, :]
    wn_ref[keep:wb * WIN_ROWS, :] = nw_ref[...]


def nsa_sample(cache3, page0, page_table, q, gates, kcvc, kv_new, win_new, win, win0, s_new):
    db, n_pages = page_table.shape
    page = cache3.shape[1] // KV_ROWS
    past = n_pages * page
    hd = q.shape[-1]
    rows = q.shape[2]
    nc = kcvc.shape[2]
    wb = win.shape[1] // WIN_ROWS
    assert wb <= past and wb <= WINDOW and N_SELECT <= -(-(past + s_new) // L_SEL) <= LANE
    assert past % L_SEL == 0 and s_new <= SUBLANE and s_new <= L_SEL and NSA_KV * rows <= LANE
    i = jnp.arange(rows)
    rm = (i[:, None] % s_new == i[None, :] % s_new).astype(F32)
    slopes = jnp.repeat(_slopes(), s_new, axis=1)[:, :, None]
    ov = _overlap_matrix(nc, LANE)
    em = _expand_matrix(LANE, past)
    per_b = lambda a: pl.BlockSpec((None,) + a.shape[1:], lambda b, pt: (b,) + (0,) * (a.ndim - 1))
    full = lambda a: pl.BlockSpec(a.shape, lambda b, pt: (0,) * a.ndim)
    return pl.pallas_call(
        functools.partial(_nsa_sample_kernel, n_pages=n_pages, hd=hd, s_new=s_new, scale=hd ** -0.5),
        out_shape=(jax.ShapeDtypeStruct(q.shape, BF16), jax.ShapeDtypeStruct((db,) + win.shape[1:], F32)),
        grid_spec=pltpu.PrefetchScalarGridSpec(
            num_scalar_prefetch=1,
            grid=(db,),
            in_specs=_page_specs(n_pages, page * KV_ROWS, hd, page0)
            + [per_b(q), per_b(gates), per_b(kcvc), per_b(kv_new), per_b(win_new),
               pl.BlockSpec((None,) + win.shape[1:], lambda b, pt: (win0 + b, 0, 0)),
               full(ov), full(em), full(rm), full(slopes)],
            out_specs=(per_b(q), per_b(win)),
            scratch_shapes=[pltpu.VMEM((LANE, LANE), F32), pltpu.VMEM((WIN_ROWS, past, hd), BF16),
                            pltpu.VMEM((WIN_ROWS, wb, hd), BF16)]),
        compiler_params=_cp("parallel"),
        name="nsa_sample",
    )(page_table.reshape(-1), *([cache3] * n_pages), q, gates, kcvc, kv_new, win_new, win, ov, em, rm, slopes)


def _ffn_up_prompt_kernel(h_ref, halo_ref, wa_ref, wb_ref, cw_ref, cb_ref, u_ref, tail_ref, ext_ref, *, tm):
    wa = wa_ref[...]
    h = h_ref[...]
    a = _dot(h, wa)
    prev = _dot(halo_ref[...], wa)
    ext_ref[0:SUBLANE, :] = jnp.where(pl.program_id(0) > 0, prev, 0.0)
    ext_ref[SUBLANE:SUBLANE + tm, :] = a
    y = (cb_ref[...] + cw_ref[0:1, :] * ext_ref[pl.ds(SUBLANE - 2, tm), :]
         + cw_ref[1:2, :] * ext_ref[pl.ds(SUBLANE - 1, tm), :] + cw_ref[2:3, :] * a)
    u_ref[...] = (jax.nn.gelu(y) * _dot(h, wb_ref[...])).astype(u_ref.dtype)
    tail_ref[...] = a[tm - SUBLANE:tm, :]


def ffn_up_prompt(h2, w_up_b, conv_w, conv_b, tm, tn):
    m, d = h2.shape
    ff = w_up_b.shape[1] // 2
    nj = ff // tn
    assert CONV_W == 3 and m % tm == 0 and ff % tn == 0
    return pl.pallas_call(
        functools.partial(_ffn_up_prompt_kernel, tm=tm),
        out_shape=(jax.ShapeDtypeStruct((m, ff), BF16), jax.ShapeDtypeStruct((m // tm, SUBLANE, ff), F32)),
        grid=(m // tm, nj),
        in_specs=[pl.BlockSpec((tm, d), lambda i, j: (i, 0)),
                  pl.BlockSpec((SUBLANE, d), lambda i, j: (jnp.maximum(i * (tm // SUBLANE) - 1, 0), 0)),
                  pl.BlockSpec((d, tn), lambda i, j: (0, j)),
                  pl.BlockSpec((d, tn), lambda i, j: (0, nj + j)),
                  pl.BlockSpec((CONV_W, tn), lambda i, j: (0, j)),
                  pl.BlockSpec((1, tn), lambda i, j: (0, j))],
        out_specs=(pl.BlockSpec((tm, tn), lambda i, j: (i, j)),
                   pl.BlockSpec((None, SUBLANE, tn), lambda i, j: (i, 0, j))),
        scratch_shapes=[pltpu.VMEM((tm + SUBLANE, tn), F32)],
        compiler_params=_cp("parallel", "parallel"),
        name="ffn_up_prompt",
    )(h2, h2, w_up_b, w_up_b, conv_w, conv_b.reshape(1, ff))


def _ffn_up_sample_kernel(h_ref, prev_ref, wa_ref, wb_ref, cw_ref, cb_ref, u_ref, tail_ref, ext_ref, *, db, m):
    h = h_ref[...]
    a = _dot(h, wa_ref[...])
    ext_ref[0:2 * db, :] = prev_ref[...]
    ext_ref[2 * db:2 * db + m, :] = a
    y = (cb_ref[...] + cw_ref[0:1, :] * ext_ref[0:m, :] + cw_ref[1:2, :] * ext_ref[db:db + m, :]
         + cw_ref[2:3, :] * a)
    u_ref[...] = (jax.nn.gelu(y) * _dot(h, wb_ref[...])).astype(u_ref.dtype)
    tail_ref[...] = ext_ref[m:m + 2 * db, :]


def ffn_up_sample(h2, prev, w_up_b, conv_w, conv_b, db, tn):
    m, d = h2.shape
    ff = w_up_b.shape[1] // 2
    nj = ff // tn
    assert CONV_W == 3 and m >= 2 * db and db % SUBLANE == 0
    return pl.pallas_call(
        functools.partial(_ffn_up_sample_kernel, db=db, m=m),
        out_shape=(jax.ShapeDtypeStruct((m, ff), BF16), jax.ShapeDtypeStruct((2 * db, ff), F32)),
        grid=(nj,),
        in_specs=[pl.BlockSpec((m, d), lambda j: (0, 0)),
                  pl.BlockSpec((2 * db, tn), lambda j: (0, j)),
                  pl.BlockSpec((d, tn), lambda j: (0, j)),
                  pl.BlockSpec((d, tn), lambda j: (0, nj + j)),
                  pl.BlockSpec((CONV_W, tn), lambda j: (0, j)),
                  pl.BlockSpec((1, tn), lambda j: (0, j))],
        out_specs=(pl.BlockSpec((m, tn), lambda j: (0, j)), pl.BlockSpec((2 * db, tn), lambda j: (0, j))),
        scratch_shapes=[pltpu.VMEM((m + 2 * db, tn), F32)],
        compiler_params=_cp("parallel"),
        name="ffn_up_sample",
    )(h2, prev, w_up_b, w_up_b, conv_w, conv_b.reshape(1, ff))


TN_IN = 1024
TN_FF = 256
TM_DOWN = 512


def _pad_cols(a, n):
    return jnp.pad(a, ((0, 0), (0, n - a.shape[1])))


def _post(x, mix, mods, tm_row, tm_mm, w_out_b, g_post_mix, g_pre_ffn, ffn_up, w_down_b, g_post_ffn):
    f = matmul(mix, w_out_b, tm_mm, 1024)
    x2, h2 = post_mix(f, x, g_post_mix, g_pre_ffn, mods, tm_row)
    u, tail = ffn_up(h2)
    f2 = matmul(u, w_down_b, min(TM_DOWN, tm_mm), TM_DOWN)
    return post_ffn(f2, x2, g_post_ffn, mods, tm_row), tail


def kernel(x_prompt, x_sample, cache_kv, cache_win, state_ret, state_conv, page_table, c_prompt, c_sample,
           w_ada, b_ada, g_pre_mix, w_in, cmp_pe, cmp_w1, cmp_w2, ret_gn, w_out, g_post_mix, g_pre_ffn,
           w_up, conv_w, conv_b, w_down, g_post_ffn):
    depth = w_in.shape[0]
    bp, t, d = x_prompt.shape
    db, s, _ = x_sample.shape
    assert bp == 1
    h, dkv = state_ret.shape[2], state_ret.shape[3]
    n_phys, page, n_rows, g, hd = cache_kv.shape[1:]
    assert h == RET_HEADS and g == NSA_KV and n_rows == 4
    wb = cache_win.shape[2]
    ff = w_up.shape[2] // 2
    rw = h * dkv
    q_col = 4 * rw
    kv_col = q_col + NSA_HEADS * hd
    ng_col = kv_col + 6 * g * hd
    n_gate = NSA_HEADS * 3
    np_ = ng_col
    assert np_ % TN_IN == 0 and ff % TN_FF == 0 and n_gate <= LANE
    gw = g * hd
    tm_p = min(1024, t)
    tm_s = s * db

    yp = x_prompt.reshape(t, d)
    ys = jnp.swapaxes(x_sample, 0, 1).reshape(s * db, d)
    c_all = jnp.concatenate([c_prompt, jnp.zeros((SUBLANE - 1, d), F32), c_sample], axis=0)
    outs = [[] for _ in range(8)]
    for l in range(depth):
        w_in_b = w_in[l][:, :np_].astype(BF16)
        w_gate_b = _pad_cols(w_in[l][:, np_:], LANE).astype(BF16)
        w_out_b = w_out[l].astype(BF16)
        w_up_b = w_up[l].astype(BF16)
        w_down_b = w_down[l].astype(BF16)
        conv_w_p = conv_w[l]
        conv_b_p = conv_b[l]
        pe2 = cmp_pe[l].reshape(2, L_CMP // 2, 2 * hd)
        w1b = cmp_w1[l].reshape(2, L_CMP // 2, 2 * hd, -1).astype(BF16)
        w2b = cmp_w2[l].astype(BF16)
        post_w = (w_out_b, g_post_mix[l], g_pre_ffn[l])

        mods = ada_mods(c_all, w_ada[l], b_ada[l])
        mods_p = mods[0:SUBLANE]
        mods_s = mods[SUBLANE:SUBLANE + db]

        hp = norm_mod(yp, g_pre_mix[l], mods_p, 0, 1, 256)
        p = matmul(hp, w_in_b, tm_p, TN_IN)
        mix_ret, ret_state = retention_prompt(p, ret_gn[l], dkv)
        kcvc = compress_prompt(p, kv_col, pe2, w1b, w2b, hd)
        p_gate = matmul(hp, w_gate_b, tm_p, LANE)
        gates = jnp.swapaxes(p_gate[:, :n_gate].reshape(t, g, 3 * NSA_REP), 0, 1)
        part, unsel = cmp_win_select_prompt(p, q_col, kv_col, kcvc, gates)
        nsa_o = slc_prompt(p, q_col, kv_col, gates, part, unsel)
        mix = jnp.concatenate([mix_ret, nsa_o], axis=1)
        yp_new, tails = _post(yp, mix, mods_p, 256, tm_p, *post_w,
                              lambda h2: ffn_up_prompt(h2, w_up_b, conv_w_p, conv_b_p, tm_p, TN_FF),
                              w_down_b, g_post_ffn[l])
        wp = min(WINDOW, t)
        outs[0].append(p[:, kv_col:kv_col + 4 * gw].reshape(1, t, 4, g, hd))
        outs[1].append(p[t - wp:, kv_col + 4 * gw:kv_col + 6 * gw].reshape(1, wp, 2, g, hd))
        outs[2].append(ret_state[None])
        outs[3].append(tails[-1, SUBLANE - (CONV_W - 1):, :ff][None])
        yp = yp_new

        hs = norm_mod(ys, g_pre_mix[l], mods_s, 0, 1, db)
        ps_t = matmul(hs, w_in_b, tm_s, TN_IN)
        ps = jnp.swapaxes(ps_t.reshape(s, db, np_), 0, 1).reshape(db * s, np_)
        mix_ret_s, ret_s = retention_sample(ps, state_ret.reshape(depth * db, h, dkv, dkv), l * db, ret_gn[l], s)
        cache3 = cache_kv.reshape(depth * n_phys, page * KV_ROWS, hd)
        kcvc_s = compress_sample(cache3, l * n_phys, page_table, pe2, w1b, w2b, hd)

        def head_rows(a, last):
            a = a.reshape(db, s, g, NSA_REP, last)
            return a.transpose(0, 2, 3, 1, 4).reshape(db, g, NSA_REP * s, last)

        q_s = head_rows(ps[:, q_col:kv_col], hd)
        ps_gate = jnp.swapaxes(matmul(hs, w_gate_b, tm_s, LANE).reshape(s, db, LANE), 0, 1).reshape(db * s, LANE)
        gates_s = head_rows(ps_gate[:, :n_gate], 3)
        kv_new = jnp.pad(ps[:, kv_col:ng_col].reshape(db, s, 6 * gw), ((0, 0), (0, SUBLANE - s), (0, 0)))
        win_rows = ps[:, kv_col + 4 * gw:ng_col].reshape(db, s * WIN_ROWS, hd)
        o_s, win_new = nsa_sample(cache3, l * n_phys, page_table, q_s, gates_s, kcvc_s, kv_new, win_rows,
                                  cache_win.reshape(depth * db, wb * WIN_ROWS, hd), l * db, s)
        nsa_o_s = o_s.reshape(db, g, NSA_REP, s, hd).transpose(0, 3, 1, 2, 4).reshape(db * s, NSA_HEADS * hd)
        mix_s = jnp.concatenate([mix_ret_s, nsa_o_s], axis=1)
        mix_s = jnp.swapaxes(mix_s.reshape(db, s, -1), 0, 1).reshape(s * db, -1)
        prev = jnp.swapaxes(state_conv[l], 0, 1).reshape((CONV_W - 1) * db, ff)
        ys_new, tail_s = _post(ys, mix_s, mods_s, db, tm_s, *post_w,
                               lambda h2: ffn_up_sample(h2, prev, w_up_b, conv_w_p, conv_b_p, db, TN_FF),
                               w_down_b, g_post_ffn[l])
        outs[4].append(ps[:, kv_col:kv_col + 4 * gw].reshape(db, s, 4, g, hd))
        outs[5].append(win_new.reshape(db, wb, 2, g, hd))
        outs[6].append(ret_s)
        outs[7].append(jnp.swapaxes(tail_s[:, :ff].reshape(CONV_W - 1, db, ff), 0, 1))
        ys = ys_new

    y_prompt = yp.reshape(1, t, d)
    y_sample = jnp.swapaxes(ys.reshape(s, db, d), 0, 1)
    return (y_prompt, y_sample) + tuple(jnp.stack(o) for o in outs)
```

```python
import functools

import jax
import jax.numpy as jnp
import numpy as np
from jax import lax
from jax.experimental import pallas as pl
from jax.experimental.pallas import tpu as pltpu

F32 = jnp.float32
BF16 = jnp.bfloat16

EPS = 1e-6
NEG = -1e30
FORCE = 1e4
RET_HEADS = 8
RET_CHUNK = 128
NSA_HEADS = 16
NSA_KV = 4
NSA_REP = NSA_HEADS // NSA_KV
L_CMP = 32
STRIDE_CMP = 16
L_SEL = 64
N_SELECT = 16
WINDOW = 512
CONV_W = 3

LANE = 128
SUBLANE = 8
VMEM_LIMIT = 56 * 1024 * 1024


def _cp(*sem):
    return pltpu.CompilerParams(dimension_semantics=sem, vmem_limit_bytes=VMEM_LIMIT)


def _silu(x):
    return x * jax.nn.sigmoid(x)


def _nt_dot(a, b):
    return lax.dot_general(a, b, (((1,), (1,)), ((), ())), preferred_element_type=F32)


def _tn_dot(a, b):
    return lax.dot_general(a, b, (((0,), (0,)), ((), ())), preferred_element_type=F32)


def _dot(a, b):
    return jnp.dot(a, b, preferred_element_type=F32)


def _ada_kernel(c_ref, w_ref, b_ref, o_ref):
    sc = _silu(c_ref[...]).astype(BF16)
    o_ref[...] = _dot(sc, w_ref[...].astype(BF16)) + b_ref[...]


def ada_mods(c_all, w_ada, b_ada, tn=512):
    m, d = c_all.shape
    n = w_ada.shape[1]
    return pl.pallas_call(
        _ada_kernel,
        out_shape=jax.ShapeDtypeStruct((m, n), F32),
        grid=(n // tn,),
        in_specs=[pl.BlockSpec((m, d), lambda j: (0, 0)),
                  pl.BlockSpec((d, tn), lambda j: (0, j)),
                  pl.BlockSpec((1, tn), lambda j: (0, j))],
        out_specs=pl.BlockSpec((m, tn), lambda j: (0, j)),
        compiler_params=_cp("arbitrary"),
        name="ada_mods",
    )(c_all, w_ada, b_ada.reshape(1, n))


def _rows(ref):
    v = ref[...]
    return v[0:1, :] if v.shape[0] == SUBLANE else v


def _rms(x, g):
    return x * lax.rsqrt(jnp.mean(x * x, axis=-1, keepdims=True) + EPS) * g


def _norm_mod_kernel(x_ref, g_ref, shift_ref, scale_ref, o_ref):
    y = _rms(x_ref[...], g_ref[...])
    o_ref[...] = (y * (1.0 + _rows(scale_ref)) + _rows(shift_ref)).astype(o_ref.dtype)


def _mod_spec(mods, tm, k, d):
    r = mods.shape[0]
    assert (r == SUBLANE) != (r == tm)
    return pl.BlockSpec((r, d), lambda i, k=k: (0, k))


def norm_mod(x, g, mods, k_shift, k_scale, tm):
    m, d = x.shape
    row = pl.BlockSpec((tm, d), lambda i: (i, 0))
    return pl.pallas_call(
        _norm_mod_kernel,
        out_shape=jax.ShapeDtypeStruct((m, d), BF16),
        grid=(m // tm,),
        in_specs=[row, pl.BlockSpec((1, d), lambda i: (0, 0)),
                  _mod_spec(mods, tm, k_shift, d), _mod_spec(mods, tm, k_scale, d)],
        out_specs=row,
        compiler_params=_cp("parallel"),
        name="norm_mod",
    )(x, g.reshape(1, d), mods, mods)


def _post_mix_kernel(f_ref, x_ref, g1_ref, g2_ref, gate_ref, shift_ref, scale_ref, x2_ref, h2_ref):
    x2 = x_ref[...] + _rows(gate_ref) * _rms(f_ref[...], g1_ref[...])
    x2_ref[...] = x2
    h2_ref[...] = (_rms(x2, g2_ref[...]) * (1.0 + _rows(scale_ref)) + _rows(shift_ref)).astype(h2_ref.dtype)


def post_mix(f, x, g_post_mix, g_pre_ffn, mods, tm):
    m, d = x.shape
    row = pl.BlockSpec((tm, d), lambda i: (i, 0))
    vec = pl.BlockSpec((1, d), lambda i: (0, 0))
    return pl.pallas_call(
        _post_mix_kernel,
        out_shape=(jax.ShapeDtypeStruct((m, d), F32), jax.ShapeDtypeStruct((m, d), BF16)),
        grid=(m // tm,),
        in_specs=[row, row, vec, vec, _mod_spec(mods, tm, 2, d), _mod_spec(mods, tm, 3, d),
                  _mod_spec(mods, tm, 4, d)],
        out_specs=(row, row),
        compiler_params=_cp("parallel"),
        name="post_mix",
    )(f, x, g_post_mix.reshape(1, d), g_pre_ffn.reshape(1, d), mods, mods, mods)


def _post_ffn_kernel(f_ref, x_ref, g_ref, gate_ref, y_ref):
    y_ref[...] = x_ref[...] + _rows(gate_ref) * _rms(f_ref[...], g_ref[...])


def post_ffn(f, x2, g_post_ffn, mods, tm):
    m, d = x2.shape
    row = pl.BlockSpec((tm, d), lambda i: (i, 0))
    return pl.pallas_call(
        _post_ffn_kernel,
        out_shape=jax.ShapeDtypeStruct((m, d), F32),
        grid=(m // tm,),
        in_specs=[row, row, pl.BlockSpec((1, d), lambda i: (0, 0)), _mod_spec(mods, tm, 5, d)],
        out_specs=row,
        compiler_params=_cp("parallel"),
        name="post_ffn",
    )(f, x2, g_post_ffn.reshape(1, d), mods)


def _mm_kernel(a_ref, w_ref, o_ref):
    o_ref[...] = _dot(a_ref[...], w_ref[...].astype(BF16))


def matmul(a, w, tm, tn, n=None):
    m, kd = a.shape
    n = w.shape[1] if n is None else n
    assert m % tm == 0 and n % tn == 0 and n <= w.shape[1]
    return pl.pallas_call(
        _mm_kernel,
        out_shape=jax.ShapeDtypeStruct((m, n), F32),
        grid=(m // tm, n // tn),
        in_specs=[pl.BlockSpec((tm, kd), lambda i, j: (i, 0)),
                  pl.BlockSpec((kd, tn), lambda i, j: (0, j))],
        out_specs=pl.BlockSpec((tm, tn), lambda i, j: (i, j)),
        compiler_params=_cp("parallel", "parallel"),
        name="matmul",
    )(a, w)


def _ret_out(o, gate, gn):
    mu = jnp.mean(o, axis=-1, keepdims=True)
    var = jnp.mean(jnp.square(o - mu), axis=-1, keepdims=True)
    return (o - mu) * lax.rsqrt(var + EPS) * gn * _silu(gate)


def _ret_prompt_kernel(q_ref, k_ref, v_ref, gate_ref, dmat_ref, dq_ref, dk_ref, gc_ref, gn_ref,
                       o_ref, st_ref, *, dkv, dk_scale):
    @pl.when(pl.program_id(0) == 0)
    def _():
        st_ref[...] = jnp.zeros_like(st_ref)

    heads = range(RET_HEADS)
    sl = [slice(h * dkv, (h + 1) * dkv) for h in heads]
    ks = [k_ref[:, sl[h]] * dk_scale for h in heads]
    qb = [q_ref[:, sl[h]].astype(BF16) for h in heads]
    vb = [v_ref[:, sl[h]].astype(BF16) for h in heads]
    attn = [_nt_dot(qb[h], ks[h].astype(BF16)) * dmat_ref[h] for h in heads]
    carried = [_dot(qb[h], st_ref[h].astype(BF16)) * dq_ref[h] for h in heads]
    o = [_dot(attn[h].astype(BF16), vb[h]) + carried[h] for h in heads]
    for h in heads:
        st_ref[h] = gc_ref[h] * st_ref[h] + _tn_dot((ks[h] * dk_ref[h]).astype(BF16), vb[h])
    for h in heads:
        o_ref[:, sl[h]] = _ret_out(o[h], gate_ref[:, sl[h]], gn_ref[:, sl[h]]).astype(o_ref.dtype)


def _ret_tables(c):
    log_g = jnp.log(1.0 - jnp.exp2(-5.0 - jnp.arange(RET_HEADS, dtype=F32)))
    i = jnp.arange(c, dtype=F32)
    diff = i[:, None] - i[None, :]
    dmat = jnp.where(diff >= 0, jnp.exp(log_g[:, None, None] * jnp.maximum(diff, 0.0)), 0.0)
    dq = jnp.exp(log_g[:, None] * (i[None, :] + 1.0))[:, :, None]
    dk = jnp.exp(log_g[:, None] * (c - 1.0 - i[None, :]))[:, :, None]
    gc = jnp.exp(log_g * c)[:, None, None]
    return dmat, dq, dk, gc


def retention_prompt(p, ret_gn, dkv):
    t = p.shape[0]
    c = RET_CHUNK
    h = RET_HEADS
    dmat, dq, dk, gc = _ret_tables(c)

    w = h * dkv

    def col(part):
        return pl.BlockSpec((c, w), lambda cc, part=part: (cc, part))

    def full(a):
        return pl.BlockSpec(a.shape, lambda cc: (0,) * a.ndim)

    return pl.pallas_call(
        functools.partial(_ret_prompt_kernel, dkv=dkv, dk_scale=dkv ** -0.5),
        out_shape=(jax.ShapeDtypeStruct((t, w), BF16), jax.ShapeDtypeStruct((h, dkv, dkv), F32)),
        grid=(t // c,),
        in_specs=[col(0), col(1), col(2), col(3), full(dmat), full(dq), full(dk), full(gc),
                  pl.BlockSpec((1, w), lambda cc: (0, 0))],
        out_specs=(pl.BlockSpec((c, w), lambda cc: (cc, 0)),
                   pl.BlockSpec((h, dkv, dkv), lambda cc: (0, 0, 0))),
        compiler_params=_cp("arbitrary"),
        name="retention_prompt",
    )(p, p, p, p, dmat, dq, dk, gc, ret_gn.reshape(1, w))


def _ret_sample_kernel(q_ref, k_ref, v_ref, gate_ref, st_ref, dmat_ref, dq_ref, dk_ref, gc_ref, gn_ref,
                       o_ref, nst_ref, *, dkv, s, dk_scale):
    rows = 2 * s
    rid = lax.broadcasted_iota(jnp.int32, (rows, 1), 0)
    first = rid < s
    for h in range(RET_HEADS):
        sl = slice(h * dkv, (h + 1) * dkv)
        q = q_ref[:, sl]
        k = k_ref[:, sl] * dk_scale
        v = v_ref[:, sl]
        attn = _nt_dot(q, k) * dmat_ref[h]
        o = _dot(attn, v)
        dq = dq_ref[h]
        kdec = k * dk_ref[h]
        for b in range(2):
            st = st_ref[b, h]
            mine = first if b == 0 else jnp.logical_not(first)
            o = o + jnp.where(mine, _dot(q, st) * dq, 0.0)
            nst_ref[b, h] = gc_ref[h] * st + _tn_dot(jnp.where(mine, kdec, 0.0), v)
        o_ref[:, sl] = _ret_out(o, gate_ref[:, sl], gn_ref[:, sl]).astype(o_ref.dtype)


def retention_sample(p, state, b0, ret_gn, s):
    _, h, dkv, _ = state.shape
    db = p.shape[0] // s
    rows = 2 * s
    assert rows == SUBLANE and db % 2 == 0
    log_g = jnp.log(1.0 - jnp.exp2(-5.0 - jnp.arange(h, dtype=F32)))
    i = jnp.arange(rows)
    tok = (i % s).astype(F32)
    same = (i[:, None] // s) == (i[None, :] // s)
    diff = tok[:, None] - tok[None, :]
    dmat = jnp.where(same & (diff >= 0), jnp.exp(log_g[:, None, None] * jnp.maximum(diff, 0.0)), 0.0)
    dq = jnp.exp(log_g[:, None] * (tok[None, :] + 1.0))[:, :, None]
    dk = jnp.exp(log_g[:, None] * (s - 1.0 - tok[None, :]))[:, :, None]
    gc = jnp.exp(log_g * s)[:, None, None]
    w = h * dkv

    def col(part):
        return pl.BlockSpec((rows, w), lambda i, part=part: (i, part))

    def full(a):
        return pl.BlockSpec(a.shape, lambda i: (0,) * a.ndim)

    assert b0 % 2 == 0
    st_block = (2, h, dkv, dkv)
    return pl.pallas_call(
        functools.partial(_ret_sample_kernel, dkv=dkv, s=s, dk_scale=dkv ** -0.5),
        out_shape=(jax.ShapeDtypeStruct((db * s, w), BF16), jax.ShapeDtypeStruct((db, h, dkv, dkv), F32)),
        grid=(db // 2,),
        in_specs=[col(0), col(1), col(2), col(3), pl.BlockSpec(st_block, lambda i: (b0 // 2 + i, 0, 0, 0)),
                  full(dmat), full(dq), full(dk), full(gc), pl.BlockSpec((1, w), lambda i: (0, 0))],
        out_specs=(pl.BlockSpec((rows, w), lambda i: (i, 0)), pl.BlockSpec(st_block, lambda i: (i, 0, 0, 0))),
        compiler_params=_cp("parallel"),
        name="retention_sample",
    )(p, p, p, p, state, dmat, dq, dk, gc, ret_gn.reshape(1, w))


def _compress_block(load_rows, pe_ref, w1_ref, w2, accb_ref, nh):
    pairs = L_CMP // 4
    acc_a = None
    acc_b = None
    for i in range(pairs):
        y = jnp.concatenate([load_rows(2 * i), load_rows(2 * i + 1)], axis=1)
        da = _dot((y + pe_ref[i:i + 1, :]).astype(BF16), w1_ref[i])
        db = _dot((y + pe_ref[pairs + i:pairs + i + 1, :]).astype(BF16), w1_ref[pairs + i])
        acc_a = da if acc_a is None else acc_a + da
        acc_b = db if acc_b is None else acc_b + db
    accb_ref[0:nh, :] = acc_b
    accb_ref[nh:nh + SUBLANE, :] = jnp.zeros((SUBLANE, acc_b.shape[1]), F32)
    h = jax.nn.gelu(acc_a + accb_ref[pl.ds(1, nh), :])
    return _dot(h.astype(BF16), w2)


def _masked_softmax(s, mask):
    s = jnp.where(mask, s, NEG)
    e = jnp.exp(s - jnp.max(s, axis=-1, keepdims=True))
    return jnp.where(mask, e / jnp.sum(e, axis=-1, keepdims=True), 0.0)


def _importance(psum, ov):
    hi = psum.astype(BF16)
    lo = (psum - hi.astype(F32)).astype(BF16)
    return _dot(hi, ov) + _dot(lo, ov)


def _select_blocks_t(imp_t, q_pos, n_sel):
    nb = imp_t.shape[0]
    blk = lax.broadcasted_iota(jnp.int32, (nb, 1), 0)
    blkf = blk.astype(F32)
    cur = q_pos // L_SEL
    forced = (blk == 0) | (blk == cur) | (blk == cur - 1)
    score = jnp.where(blk > cur, -1.0, jnp.where(forced, FORCE, imp_t))
    score = jnp.where(blk >= n_sel, -2.0, score)
    taken = -3.0
    work = score
    for _ in range(min(N_SELECT, n_sel)):
        m = jnp.max(work, axis=0, keepdims=True)
        first = jnp.min(jnp.where(work == m, blkf, float(nb)), axis=0, keepdims=True)
        work = jnp.where(blkf == first, taken, work)
    return jnp.where((work == taken) & (score >= 0.0), 1.0, 0.0)


def _overlap_matrix(nc, nb):
    n = jnp.arange(nc)[:, None]
    j = jnp.arange(nb)[None, :]
    c_start = n * STRIDE_CMP
    c_end = c_start + L_CMP - 1
    return ((c_start < (j + 1) * L_SEL) & (c_end >= j * L_SEL)).astype(BF16)


def _expand_matrix(nb, nkeys):
    return (jnp.arange(nb)[:, None] == (jnp.arange(nkeys)[None, :] // L_SEL)).astype(BF16)


def _slopes():
    h = jnp.arange(1, NSA_HEADS + 1, dtype=F32)
    return jnp.exp2(-8.0 * h / NSA_HEADS).reshape(NSA_KV, NSA_REP)


def _compress_prompt_kernel(x_ref, pe_ref, w1_ref, w2_ref, o_ref, accb_ref):
    nh = x_ref.shape[0] // STRIDE_CMP
    o_ref[...] = _compress_block(lambda l: x_ref[pl.ds(l, nh, stride=STRIDE_CMP), :],
                                 pe_ref, w1_ref, w2_ref[...], accb_ref, nh)


def compress_prompt(p, kv_col, cmp_pe, w1b, w2b, hd):
    t = p.shape[0]
    nh = t // STRIDE_CMP
    hid = w1b.shape[-1]
    cb = kv_col // hd
    return pl.pallas_call(
        _compress_prompt_kernel,
        out_shape=jax.ShapeDtypeStruct((2, NSA_KV, nh, hd), F32),
        grid=(2, NSA_KV),
        in_specs=[pl.BlockSpec((t, hd), lambda kind, g: (0, cb + kind * NSA_KV + g)),
                  pl.BlockSpec((None, L_CMP // 2, 2 * hd), lambda kind, g: (kind, 0, 0)),
                  pl.BlockSpec((None, L_CMP // 2, 2 * hd, hid), lambda kind, g: (kind, 0, 0, 0)),
                  pl.BlockSpec((None, hid, hd), lambda kind, g: (kind, 0, 0))],
        out_specs=pl.BlockSpec((None, None, nh, hd), lambda kind, g: (kind, g, 0, 0)),
        scratch_shapes=[pltpu.VMEM((nh + SUBLANE, hid), F32)],
        compiler_params=_cp("parallel", "parallel"),
        name="compress_prompt",
    )(p, cmp_pe, w1b, w2b)


MASK_BIG = 2.0 ** 99
LOG2E = 1.4426950408889634


def _cmp_win_select_kernel(q_ref, kc_ref, vc_ref, ov_ref, sl_ref, gate_ref, *rest, tq, hd, n_sel, n_prev, scale):
    kw_refs = rest[:n_prev + 1]
    vw_refs = rest[n_prev + 1:2 * n_prev + 2]
    part_ref, unsel_ref = rest[2 * n_prev + 2:]
    qt = pl.program_id(1)
    nc = kc_ref.shape[0]
    nw = (n_prev + 1) * tq
    q_pos = qt * tq + lax.broadcasted_iota(jnp.int32, (tq, 1), 0)
    dist_c = q_pos - (lax.broadcasted_iota(jnp.int32, (1, nc), 1) * STRIDE_CMP + (L_CMP - 1))
    mask_c = dist_c >= 0
    dist_cf = dist_c.astype(F32)
    kcb = kc_ref[...].astype(BF16)
    vcb = vc_ref[...].astype(BF16)
    w_pos = (qt - n_prev) * tq + lax.broadcasted_iota(jnp.int32, (1, nw), 1)
    dist_w = q_pos - w_pos
    mask_w = (dist_w >= 0) & (dist_w <= WINDOW) & (w_pos >= 0)
    dist_wf = dist_w.astype(F32)
    kwb = jnp.concatenate([r[...].astype(BF16) for r in kw_refs], axis=0)
    vwb = jnp.concatenate([r[...].astype(BF16) for r in vw_refs], axis=0)
    gate = jax.nn.sigmoid(gate_ref[...])
    heads = range(NSA_REP)
    qbs = [q_ref[:, r * hd:(r + 1) * hd].astype(BF16) for r in heads]
    s_cs = [_nt_dot(qbs[r], kcb) for r in heads]
    s_ws = [_nt_dot(qbs[r], kwb) for r in heads]
    p_cs = [_masked_softmax(s_cs[r] * scale - sl_ref[r][:, 0:1] * dist_cf, mask_c) for r in heads]
    p_ws = [_masked_softmax(s_ws[r] * scale - sl_ref[r][:, 0:1] * dist_wf, mask_w) for r in heads]
    psum = sum(p_cs[1:], p_cs[0])
    for r in heads:
        part_ref[:, r * hd:(r + 1) * hd] = (gate[:, 3 * r:3 * r + 1] * _dot(p_cs[r].astype(BF16), vcb)
                                            + gate[:, 3 * r + 2:3 * r + 3] * _dot(p_ws[r].astype(BF16), vwb))
    q_pos_row = qt * tq + lax.broadcasted_iota(jnp.int32, (1, tq), 1)
    picked_t = _select_blocks_t(_importance(psum, ov_ref[...]).T, q_pos_row, n_sel)
    unsel_ref[...] = (picked_t - 1.0) * MASK_BIG


def cmp_win_select_prompt(p, q_col, kv_col, kcvc, gates, tq=256):
    t = p.shape[0]
    nc, hd = kcvc.shape[2:]
    n_sel = -(-t // L_SEL)
    tq = min(tq, t)
    assert N_SELECT <= n_sel <= LANE and WINDOW % tq == 0 and t % tq == 0
    n_prev = WINDOW // tq
    w = NSA_REP * hd
    cb = kv_col // hd
    slopes = jnp.broadcast_to(_slopes()[:, :, None, None], (NSA_KV, NSA_REP, 1, LANE))

    def win_specs(kind):
        return [pl.BlockSpec((tq, hd), lambda g, i, j=j: (jnp.maximum(i - j, 0), cb + kind * NSA_KV + g))
                for j in range(n_prev, -1, -1)]

    return pl.pallas_call(
        functools.partial(_cmp_win_select_kernel, tq=tq, hd=hd, n_sel=n_sel, n_prev=n_prev, scale=hd ** -0.5),
        out_shape=(jax.ShapeDtypeStruct((t, NSA_KV * w), F32), jax.ShapeDtypeStruct((NSA_KV, LANE, t), F32)),
        grid=(NSA_KV, t // tq),
        in_specs=[pl.BlockSpec((tq, w), lambda g, i: (i, q_col // w + g)),
                  pl.BlockSpec((None, None, nc, hd), lambda g, i: (0, g, 0, 0)),
                  pl.BlockSpec((None, None, nc, hd), lambda g, i: (1, g, 0, 0)),
                  pl.BlockSpec((nc, LANE), lambda g, i: (0, 0)),
                  pl.BlockSpec((None, NSA_REP, 1, LANE), lambda g, i: (g, 0, 0, 0)),
                  pl.BlockSpec((None, tq, 3 * NSA_REP), lambda g, i: (g, i, 0))]
        + win_specs(4) + win_specs(5),
        out_specs=(pl.BlockSpec((tq, w), lambda g, i: (i, g)),
                   pl.BlockSpec((None, LANE, tq), lambda g, i: (g, 0, i))),
        compiler_params=_cp("parallel", "parallel"),
        name="cmp_win_select_prompt",
    )(p, kcvc, kcvc, _overlap_matrix(nc, LANE), slopes, gates, *([p] * (2 * n_prev + 2)))


def _slc_kernel(qi_ref, ki_ref, q_ref, unsel_ref, al_ref, k_ref, vt_ref, kf_ref, gate_ref, part_ref,
                o_ref, qa_ref, al_sc, m_ref, l_ref, acc_ref, *, tq, tk, hd, scale):
    step = pl.program_id(1)
    qt = qi_ref[step]
    kt = ki_ref[step]

    cols = [slice(r * tq, (r + 1) * tq) for r in range(NSA_REP)]

    @pl.when(kt == 0)
    def _():
        q_pos = (qt * tq + lax.broadcasted_iota(jnp.int32, (1, tq), 1)).astype(F32)
        sub = lax.broadcasted_iota(jnp.int32, (SUBLANE, 1), 0)
        for r in range(NSA_REP):
            qa_ref[0:hd, cols[r]] = (q_ref[:, r * hd:(r + 1) * hd] * (scale * LOG2E)).T.astype(BF16)
            tab = jnp.concatenate([al_ref[r]] * (tq // LANE), axis=1)
            slope = tab[ALIBI_SLOPE:ALIBI_SLOPE + 1, :]
            al_sc[:, cols[r]] = jnp.where(sub == ALIBI_ROWCONST, -slope * q_pos,
                                          jnp.where(sub == ALIBI_SLOPE, 0.0, tab))
        qa_ref[hd + FEAT:2 * hd, :] = jnp.zeros((hd - FEAT, NSA_REP * tq), BF16)
        m_ref[...] = jnp.full(m_ref.shape, M_FLOOR, F32)
        l_ref[...] = jnp.zeros(l_ref.shape, F32)
        acc_ref[...] = jnp.zeros(acc_ref.shape, F32)

    picks = unsel_ref[pl.ds(pl.multiple_of(kt * (tk // L_SEL), SUBLANE), tk // L_SEL), :]
    feat = jnp.concatenate([jnp.concatenate([picks] * NSA_REP, axis=1), al_sc[...]], axis=0)
    qa_ref[hd:hd + FEAT, :] = feat.astype(BF16)
    ka = jnp.concatenate([k_ref[...].astype(BF16), kf_ref[...]], axis=1)
    vtb = vt_ref[...].astype(BF16)
    last = (qt * tq + tq - 1) // tk

    def update(scores):
        for r in range(NSA_REP):
            s = scores[r]
            m_old = m_ref[:, cols[r]]
            m_new = jnp.maximum(m_old, jnp.max(s, axis=0, keepdims=True))
            alpha = jnp.exp2(m_old - m_new)
            e = jnp.exp2(s - m_new)
            l_ref[:, cols[r]] = alpha * l_ref[:, cols[r]] + jnp.sum(e, axis=0, keepdims=True)
            acc_ref[:, cols[r]] = alpha * acc_ref[:, cols[r]] + _dot(vtb, e.astype(BF16))
            m_ref[:, cols[r]] = m_new

    @pl.when(kt != last)
    def _():
        update([_dot(ka, qa_ref[:, cols[r]]) for r in range(NSA_REP)])

    @pl.when(kt == last)
    def _():
        causal = (kt * tk + lax.broadcasted_iota(jnp.int32, (tk, 1), 0)
                  <= qt * tq + lax.broadcasted_iota(jnp.int32, (1, tq), 1))
        gate = jax.nn.sigmoid(gate_ref[...])
        update([jnp.where(causal, _dot(ka, qa_ref[:, cols[r]]), NEG) for r in range(NSA_REP)])
        for r in range(NSA_REP):
            sl = slice(r * hd, (r + 1) * hd)
            o_s = (acc_ref[:, cols[r]] / l_ref[:, cols[r]]).T
            o_ref[:, sl] = (part_ref[:, sl] + gate[:, 3 * r + 1:3 * r + 2] * o_s).astype(o_ref.dtype)


FEAT = 2 * SUBLANE
ALIBI_ROWCONST = 6
ALIBI_SLOPE = 7
M_FLOOR = -1e29


def _alibi_tables(t, tk):
    slope = _slopes() * LOG2E
    s1 = slope.astype(BF16).astype(F32)
    s2 = (slope - s1).astype(BF16).astype(F32)
    s3 = (slope - s1 - s2).astype(BF16).astype(F32)
    zero = jnp.zeros_like(slope)
    qf = jnp.stack([s1, s2, s3, s1, s2, s3, zero, slope], axis=-1)
    qf = jnp.broadcast_to(qf[:, :, :, None], qf.shape + (LANE,))
    pos = jnp.arange(t)
    blk = (pos // L_SEL) % (tk // L_SEL)
    onehot = (blk[:, None] == jnp.arange(SUBLANE)[None, :]).astype(F32)
    hi = (pos // L_SEL * L_SEL).astype(F32)[:, None]
    lo = (pos % L_SEL).astype(F32)[:, None]
    one = jnp.ones((t, 1), F32)
    kf = jnp.concatenate([onehot, hi, hi, hi, lo, lo, lo, one, jnp.zeros((t, LANE - FEAT + 1), F32)], axis=1)
    return qf, kf.astype(BF16)


def slc_prompt(p, q_col, kv_col, gates, part, unsel_t, tq=256, tk=512):
    t = p.shape[0]
    hd = part.shape[1] // NSA_HEADS
    w = NSA_REP * hd
    tq, tk = min(tq, t), min(tk, t)
    assert t % tq == 0 and t % tk == 0 and tk % tq == 0 and tk == SUBLANE * L_SEL and hd == LANE
    qi, ki = [], []
    for i in range(t // tq):
        for k in range((i * tq + tq - 1) // tk + 1):
            qi.append(i)
            ki.append(k)
    qi, ki = jnp.asarray(qi, jnp.int32), jnp.asarray(ki, jnp.int32)
    cb = kv_col // hd
    al_q, kfeat = _alibi_tables(t, tk)
    v_t = jnp.swapaxes(p[:, kv_col + 3 * NSA_KV * hd:kv_col + 4 * NSA_KV * hd].reshape(t, NSA_KV, hd), 0, 1)
    v_t = jnp.swapaxes(v_t, 1, 2)
    qrow = lambda g, s, qi, ki: (qi[s], g)
    rows = NSA_REP * tq
    return pl.pallas_call(
        functools.partial(_slc_kernel, tq=tq, tk=tk, hd=hd, scale=hd ** -0.5),
        out_shape=jax.ShapeDtypeStruct((t, NSA_KV * w), BF16),
        grid_spec=pltpu.PrefetchScalarGridSpec(
            num_scalar_prefetch=2,
            grid=(NSA_KV, int(qi.shape[0])),
            in_specs=[pl.BlockSpec((tq, w), lambda g, s, qi, ki: (qi[s], q_col // w + g)),
                      pl.BlockSpec((None, LANE, tq), lambda g, s, qi, ki: (g, 0, qi[s])),
                      pl.BlockSpec((None, NSA_REP, SUBLANE, LANE), lambda g, s, qi, ki: (g, 0, 0, 0)),
                      pl.BlockSpec((tk, hd), lambda g, s, qi, ki: (ki[s], cb + 2 * NSA_KV + g)),
                      pl.BlockSpec((None, hd, tk), lambda g, s, qi, ki: (g, 0, ki[s])),
                      pl.BlockSpec((tk, LANE), lambda g, s, qi, ki: (ki[s], 0)),
                      pl.BlockSpec((None, tq, 3 * NSA_REP), lambda g, s, qi, ki: (g, qi[s], 0)),
                      pl.BlockSpec((tq, w), qrow)],
            out_specs=pl.BlockSpec((tq, w), qrow),
            scratch_shapes=[pltpu.VMEM((2 * hd, rows), BF16), pltpu.VMEM((SUBLANE, rows), F32),
                            pltpu.VMEM((1, rows), F32), pltpu.VMEM((1, rows), F32),
                            pltpu.VMEM((hd, rows), F32)]),
        compiler_params=_cp("parallel", "arbitrary"),
        name="slc_prompt",
    )(qi, ki, p, unsel_t, al_q, p, v_t, kfeat, gates, part)


PAIR_ROWS = 2 * NSA_KV


def _page_specs(n_pages, page, hd, half, page0):
    return [pl.BlockSpec((None, page, None, PAIR_ROWS, hd),
                         lambda b, pt, j=j: (page0 + pt[b * n_pages + j], 0, half, 0, 0))
            for j in range(n_pages)]


def _rows_by_head(x):
    return jnp.swapaxes(x, 0, 1)


def _compress_sample_kernel(pt_ref, *refs, n_pages, hd):
    page_refs = refs[:n_pages]
    pe_ref, w1_ref, w2_ref, o_ref, acc_ref, accb_ref = refs[n_pages:]
    page = page_refs[0].shape[0]
    per_page = page // STRIDE_CMP
    nh = n_pages * per_page
    pairs = L_CMP // 4
    rows = NSA_KV * nh

    def rows_by_head(l):
        x = jnp.stack([pr[m * STRIDE_CMP + l] for pr in page_refs for m in range(per_page)], axis=0)
        return _rows_by_head(x)

    for i in range(pairs):
        ya, yb = rows_by_head(2 * i), rows_by_head(2 * i + 1)
        for kind in range(2):
            sl = slice(kind * NSA_KV, (kind + 1) * NSA_KV)
            y = jnp.concatenate([ya[sl].reshape(rows, hd), yb[sl].reshape(rows, hd)], axis=1)
            for part in range(2):
                j = part * pairs + i
                d = _dot((y + pe_ref[kind, j:j + 1, :]).astype(BF16), w1_ref[kind, j])
                if i == 0:
                    acc_ref[kind, part] = d
                else:
                    acc_ref[kind, part] += d
    for kind in range(2):
        accb_ref[0:rows, :] = acc_ref[kind, 1]
        accb_ref[rows:rows + SUBLANE, :] = jnp.zeros((SUBLANE, accb_ref.shape[1]), F32)
        h = jax.nn.gelu(acc_ref[kind, 0] + accb_ref[pl.ds(1, rows), :])
        o_ref[kind * NSA_KV:(kind + 1) * NSA_KV] = _dot(h.astype(BF16), w2_ref[kind]).reshape(NSA_KV, nh, hd)


def compress_sample(cache5, page0, page_table, pe2, w1b, w2b, hd):
    db, n_pages = page_table.shape
    page = cache5.shape[1]
    nh = n_pages * page // STRIDE_CMP
    hid = w1b.shape[-1]
    assert L_CMP == 2 * STRIDE_CMP
    full = lambda a: pl.BlockSpec(a.shape, lambda b, pt: (0,) * a.ndim)
    return pl.pallas_call(
        functools.partial(_compress_sample_kernel, n_pages=n_pages, hd=hd),
        out_shape=jax.ShapeDtypeStruct((db, 2 * NSA_KV, nh, hd), F32),
        grid_spec=pltpu.PrefetchScalarGridSpec(
            num_scalar_prefetch=1,
            grid=(db,),
            in_specs=_page_specs(n_pages, page, hd, 0, page0) + [full(pe2), full(w1b), full(w2b)],
            out_specs=pl.BlockSpec((None, 2 * NSA_KV, nh, hd), lambda b, pt: (b, 0, 0, 0)),
            scratch_shapes=[pltpu.VMEM((2, 2, NSA_KV * nh, hid), F32),
                            pltpu.VMEM((NSA_KV * nh + SUBLANE, hid), F32)]),
        compiler_params=_cp("parallel"),
        name="compress_sample",
    )(page_table.reshape(-1), *([cache5] * n_pages), pe2, w1b, w2b)


def _softmax2(s_a, mask_a, s_b, mask_b):
    s_a = jnp.where(mask_a, s_a, NEG)
    s_b = jnp.where(mask_b, s_b, NEG)
    m = jnp.maximum(jnp.max(s_a, axis=-1, keepdims=True), jnp.max(s_b, axis=-1, keepdims=True))
    e_a = jnp.where(mask_a, jnp.exp(s_a - m), 0.0)
    e_b = jnp.where(mask_b, jnp.exp(s_b - m), 0.0)
    return e_a, e_b, jnp.sum(e_a, axis=-1, keepdims=True) + jnp.sum(e_b, axis=-1, keepdims=True)


def _nsa_sample_kernel(pt_ref, *refs, n_pages, hd, s_new, scale):
    page_refs = refs[:n_pages]
    (q_ref, gate_ref, kcvc_ref, new_ref, nw_ref, win_ref, ov_ref, e_ref, rm_ref, sl_ref, o_ref, wn_ref,
     imp_ref, kv_ref, kvw_ref) = refs[n_pages:]
    page = page_refs[0].shape[0]
    past = n_pages * page
    wb = win_ref.shape[0]
    gw = NSA_KV * hd
    rows = NSA_REP * s_new
    nc = kcvc_ref.shape[1]
    for j, pr in enumerate(page_refs):
        kv_ref[:, j * page:(j + 1) * page, :] = _rows_by_head(pr[...]).astype(BF16)
    kvw_ref[...] = _rows_by_head(win_ref[...]).astype(BF16)
    n_cmp = (past + s_new - L_CMP) // STRIDE_CMP + 1
    n_sel = -(-(past + s_new) // L_SEL)
    tok = lax.broadcasted_iota(jnp.int32, (rows, 1), 0) % s_new
    q_pos = past + tok
    tnew = lax.broadcasted_iota(jnp.int32, (1, SUBLANE), 1)
    dist_new = tok - tnew
    mask_new = (dist_new >= 0) & (tnew < s_new)
    n = lax.broadcasted_iota(jnp.int32, (1, nc), 1)
    dist_c = q_pos - (n * STRIDE_CMP + (L_CMP - 1))
    mask_c = (dist_c >= 0) & (n < n_cmp)
    imp_ref[...] = jnp.zeros(imp_ref.shape, F32)
    o_cs = []
    for g in range(NSA_KV):
        s_c = (_nt_dot(q_ref[g].astype(BF16), kcvc_ref[g].astype(BF16)) * scale
               - sl_ref[g] * dist_c.astype(F32))
        p_c = _masked_softmax(s_c, mask_c)
        o_cs.append(_dot(p_c.astype(BF16), kcvc_ref[NSA_KV + g].astype(BF16)))
        imp_ref[g * rows:(g + 1) * rows, :] = _importance(_dot(rm_ref[...], p_c), ov_ref[...])
    lane_pos = past + lax.broadcasted_iota(jnp.int32, (1, LANE), 1) % s_new
    picked_all = _select_blocks_t(imp_ref[...].T, lane_pos, n_sel).T
    groups = range(NSA_KV)
    dist_s = q_pos - lax.broadcasted_iota(jnp.int32, (1, past), 1)
    dist_w = wb + tok - lax.broadcasted_iota(jnp.int32, (1, wb), 1)
    mask_w = (dist_w >= 0) & (dist_w <= WINDOW)

    def new_rows(kind, g):
        return new_ref[:, kind * gw + g * hd:kind * gw + (g + 1) * hd]

    def scores(q, keys, dist, g):
        return _nt_dot(q, keys) * scale - sl_ref[g] * dist.astype(F32)

    qs = [q_ref[g].astype(BF16) for g in groups]
    picked = [picked_all[g * rows:(g + 1) * rows, :] for g in groups]
    key_masks = [(_dot(picked[g].astype(BF16), e_ref[...]) > 0.5) & (dist_s >= 0) for g in groups]
    cur_masks = [mask_new & (picked[g][:, past // L_SEL:past // L_SEL + 1] > 0.5) for g in groups]
    s_s = [scores(qs[g], kv_ref[g], dist_s, g) for g in groups]
    s_n = [scores(q_ref[g], new_rows(2, g), dist_new, g) for g in groups]
    s_w = [scores(qs[g], kvw_ref[g], dist_w, g) for g in groups]
    s_wn = [scores(q_ref[g], new_rows(4, g), dist_new, g) for g in groups]
    sm_s = [_softmax2(s_s[g], key_masks[g], s_n[g], cur_masks[g]) for g in groups]
    sm_w = [_softmax2(s_w[g], mask_w, s_wn[g], mask_new) for g in groups]
    for g in groups:
        e_s, e_n, l_s = sm_s[g]
        e_w, e_wn, l_w = sm_w[g]
        o_s = (_dot(e_s.astype(BF16), kv_ref[NSA_KV + g]) + _dot(e_n, new_rows(3, g))) / l_s
        o_w = (_dot(e_w.astype(BF16), kvw_ref[NSA_KV + g]) + _dot(e_wn, new_rows(5, g))) / l_w
        gate = jax.nn.sigmoid(gate_ref[g])
        o_ref[g] = (gate[:, 0:1] * o_cs[g] + gate[:, 1:2] * o_s + gate[:, 2:3] * o_w).astype(o_ref.dtype)
    wn_ref[0:wb - s_new] = win_ref[s_new:wb]
    wn_ref[wb - s_new:wb] = nw_ref[...]


def nsa_sample(cache5, page0, page_table, q, gates, kcvc, kv_new, win_new, win, win0, s_new):
    db, n_pages = page_table.shape
    page = cache5.shape[1]
    past = n_pages * page
    hd = q.shape[-1]
    rows = q.shape[2]
    nc = kcvc.shape[2]
    wb = win.shape[1]
    assert wb <= past and wb <= WINDOW and N_SELECT <= -(-(past + s_new) // L_SEL) <= LANE
    assert past % L_SEL == 0 and s_new <= SUBLANE and s_new <= L_SEL and NSA_KV * rows <= LANE
    i = jnp.arange(rows)
    rm = (i[:, None] % s_new == i[None, :] % s_new).astype(F32)
    slopes = jnp.repeat(_slopes(), s_new, axis=1)[:, :, None]
    ov = _overlap_matrix(nc, LANE)
    em = _expand_matrix(LANE, past)
    per_b = lambda a: pl.BlockSpec((None,) + a.shape[1:], lambda b, pt: (b,) + (0,) * (a.ndim - 1))
    full = lambda a: pl.BlockSpec(a.shape, lambda b, pt: (0,) * a.ndim)
    return pl.pallas_call(
        functools.partial(_nsa_sample_kernel, n_pages=n_pages, hd=hd, s_new=s_new, scale=hd ** -0.5),
        out_shape=(jax.ShapeDtypeStruct(q.shape, BF16), jax.ShapeDtypeStruct((db,) + win.shape[1:], F32)),
        grid_spec=pltpu.PrefetchScalarGridSpec(
            num_scalar_prefetch=1,
            grid=(db,),
            in_specs=_page_specs(n_pages, page, hd, 1, page0)
            + [per_b(q), per_b(gates), per_b(kcvc), per_b(kv_new), per_b(win_new),
               pl.BlockSpec((None,) + win.shape[1:], lambda b, pt: (win0 + b, 0, 0, 0)),
               full(ov), full(em), full(rm), full(slopes)],
            out_specs=(per_b(q), per_b(win)),
            scratch_shapes=[pltpu.VMEM((LANE, LANE), F32), pltpu.VMEM((PAIR_ROWS, past, hd), BF16),
                            pltpu.VMEM((PAIR_ROWS, wb, hd), BF16)]),
        compiler_params=_cp("parallel"),
        name="nsa_sample",
    )(page_table.reshape(-1), *([cache5] * n_pages), q, gates, kcvc, kv_new, win_new, win, ov, em, rm, slopes)


def _ffn_up_prompt_kernel(h_ref, halo_ref, wa_ref, wb_ref, cw_ref, cb_ref, u_ref, tail_ref, ext_ref, *, tm):
    wa = wa_ref[...].astype(BF16)
    h = h_ref[...]
    a = _dot(h, wa)
    prev = _dot(halo_ref[...], wa)
    ext_ref[0:SUBLANE, :] = jnp.where(pl.program_id(0) > 0, prev, 0.0)
    ext_ref[SUBLANE:SUBLANE + tm, :] = a
    y = (cb_ref[...] + cw_ref[0:1, :] * ext_ref[pl.ds(SUBLANE - 2, tm), :]
         + cw_ref[1:2, :] * ext_ref[pl.ds(SUBLANE - 1, tm), :] + cw_ref[2:3, :] * a)
    u_ref[...] = (jax.nn.gelu(y) * _dot(h, wb_ref[...].astype(BF16))).astype(u_ref.dtype)
    tail_ref[...] = a[tm - SUBLANE:tm, :]


def ffn_up_prompt(h2, w_up, conv_w, conv_b, tm, tn):
    m, d = h2.shape
    ff = w_up.shape[1] // 2
    nj = ff // tn
    assert CONV_W == 3 and m % tm == 0 and ff % tn == 0
    return pl.pallas_call(
        functools.partial(_ffn_up_prompt_kernel, tm=tm),
        out_shape=(jax.ShapeDtypeStruct((m, ff), BF16), jax.ShapeDtypeStruct((m // tm, SUBLANE, ff), F32)),
        grid=(m // tm, nj),
        in_specs=[pl.BlockSpec((tm, d), lambda i, j: (i, 0)),
                  pl.BlockSpec((SUBLANE, d), lambda i, j: (jnp.maximum(i * (tm // SUBLANE) - 1, 0), 0)),
                  pl.BlockSpec((d, tn), lambda i, j: (0, j)),
                  pl.BlockSpec((d, tn), lambda i, j: (0, nj + j)),
                  pl.BlockSpec((CONV_W, tn), lambda i, j: (0, j)),
                  pl.BlockSpec((1, tn), lambda i, j: (0, j))],
        out_specs=(pl.BlockSpec((tm, tn), lambda i, j: (i, j)),
                   pl.BlockSpec((None, SUBLANE, tn), lambda i, j: (i, 0, j))),
        scratch_shapes=[pltpu.VMEM((tm + SUBLANE, tn), F32)],
        compiler_params=_cp("parallel", "parallel"),
        name="ffn_up_prompt",
    )(h2, h2, w_up, w_up, conv_w, conv_b.reshape(1, ff))


def _ffn_up_sample_kernel(h_ref, prev_ref, wa_ref, wb_ref, cw_ref, cb_ref, u_ref, tail_ref, ext_ref, *, db, m):
    h = h_ref[...]
    a = _dot(h, wa_ref[...].astype(BF16))
    ext_ref[0:2 * db, :] = prev_ref[...]
    ext_ref[2 * db:2 * db + m, :] = a
    y = (cb_ref[...] + cw_ref[0:1, :] * ext_ref[0:m, :] + cw_ref[1:2, :] * ext_ref[db:db + m, :]
         + cw_ref[2:3, :] * a)
    u_ref[...] = (jax.nn.gelu(y) * _dot(h, wb_ref[...].astype(BF16))).astype(u_ref.dtype)
    tail_ref[...] = ext_ref[m:m + 2 * db, :]


def ffn_up_sample(h2, prev, w_up, conv_w, conv_b, db, tn):
    m, d = h2.shape
    ff = w_up.shape[1] // 2
    nj = ff // tn
    assert CONV_W == 3 and m >= 2 * db and db % SUBLANE == 0
    return pl.pallas_call(
        functools.partial(_ffn_up_sample_kernel, db=db, m=m),
        out_shape=(jax.ShapeDtypeStruct((m, ff), BF16), jax.ShapeDtypeStruct((2 * db, ff), F32)),
        grid=(nj,),
        in_specs=[pl.BlockSpec((m, d), lambda j: (0, 0)),
                  pl.BlockSpec((2 * db, tn), lambda j: (0, j)),
                  pl.BlockSpec((d, tn), lambda j: (0, j)),
                  pl.BlockSpec((d, tn), lambda j: (0, nj + j)),
                  pl.BlockSpec((CONV_W, tn), lambda j: (0, j)),
                  pl.BlockSpec((1, tn), lambda j: (0, j))],
        out_specs=(pl.BlockSpec((m, tn), lambda j: (0, j)), pl.BlockSpec((2 * db, tn), lambda j: (0, j))),
        scratch_shapes=[pltpu.VMEM((m + 2 * db, tn), F32)],
        compiler_params=_cp("parallel"),
        name="ffn_up_sample",
    )(h2, prev, w_up, w_up, conv_w, conv_b.reshape(1, ff))


TN_IN = 512
TN_OUT = 512
TN_FF = 256
TM_DOWN = 512


def _pad_cols(a, n):
    return jnp.pad(a, ((0, 0), (0, n - a.shape[1])))


def _post(x, mix, mods, tm_row, tm_mm, w_out_l, g_post_mix, g_pre_ffn, ffn_up, w_down_b, g_post_ffn):
    f = matmul(mix, w_out_l, tm_mm, TN_OUT)
    x2, h2 = post_mix(f, x, g_post_mix, g_pre_ffn, mods, tm_row)
    u, tail = ffn_up(h2)
    f2 = matmul(u, w_down_b, min(TM_DOWN, tm_mm), TM_DOWN)
    return post_ffn(f2, x2, g_post_ffn, mods, tm_row), tail


def kernel(x_prompt, x_sample, cache_kv, cache_win, state_ret, state_conv, page_table, c_prompt, c_sample,
           w_ada, b_ada, g_pre_mix, w_in, cmp_pe, cmp_w1, cmp_w2, ret_gn, w_out, g_post_mix, g_pre_ffn,
           w_up, conv_w, conv_b, w_down, g_post_ffn):
    depth = w_in.shape[0]
    bp, t, d = x_prompt.shape
    db, s, _ = x_sample.shape
    assert bp == 1
    h, dkv = state_ret.shape[2], state_ret.shape[3]
    n_phys, page, n_rows, g, hd = cache_kv.shape[1:]
    assert h == RET_HEADS and g == NSA_KV and n_rows == 4
    wb = cache_win.shape[2]
    ff = w_up.shape[2] // 2
    rw = h * dkv
    q_col = 4 * rw
    kv_col = q_col + NSA_HEADS * hd
    ng_col = kv_col + 6 * g * hd
    n_gate = NSA_HEADS * 3
    np_ = ng_col
    assert np_ % TN_IN == 0 and ff % TN_FF == 0 and n_gate <= LANE
    gw = g * hd
    tm_p = min(1024, t)
    tm_s = s * db

    yp = x_prompt.reshape(t, d)
    ys = jnp.swapaxes(x_sample, 0, 1).reshape(s * db, d)
    c_all = jnp.concatenate([c_prompt, jnp.zeros((SUBLANE - 1, d), F32), c_sample], axis=0)
    outs = [[] for _ in range(8)]
    for l in range(depth):
        w_in_l = w_in[l]
        w_gate_b = _pad_cols(w_in[l][:, np_:], LANE).astype(BF16)
        w_out_l = w_out[l]
        w_up_l = w_up[l]
        w_down_b = w_down[l].astype(BF16)
        conv_w_p = conv_w[l]
        conv_b_p = conv_b[l]
        pe2 = cmp_pe[l].reshape(2, L_CMP // 2, 2 * hd)
        w1b = cmp_w1[l].reshape(2, L_CMP // 2, 2 * hd, -1).astype(BF16)
        w2b = cmp_w2[l].astype(BF16)
        post_w = (w_out_l, g_post_mix[l], g_pre_ffn[l])

        mods = ada_mods(c_all, w_ada[l], b_ada[l])
        mods_p = mods[0:SUBLANE]
        mods_s = mods[SUBLANE:SUBLANE + db]

        hp = norm_mod(yp, g_pre_mix[l], mods_p, 0, 1, 256)
        p = matmul(hp, w_in_l, tm_p, TN_IN, np_)
        mix_ret, ret_state = retention_prompt(p, ret_gn[l], dkv)
        kcvc = compress_prompt(p, kv_col, pe2, w1b, w2b, hd)
        p_gate = matmul(hp, w_gate_b, tm_p, LANE)
        gates = jnp.swapaxes(p_gate[:, :n_gate].reshape(t, g, 3 * NSA_REP), 0, 1)
        part, unsel = cmp_win_select_prompt(p, q_col, kv_col, kcvc, gates)
        nsa_o = slc_prompt(p, q_col, kv_col, gates, part, unsel)
        mix = jnp.concatenate([mix_ret, nsa_o], axis=1)
        yp_new, tails = _post(yp, mix, mods_p, 256, tm_p, *post_w,
                              lambda h2: ffn_up_prompt(h2, w_up_l, conv_w_p, conv_b_p, tm_p, TN_FF),
                              w_down_b, g_post_ffn[l])
        wp = min(WINDOW, t)
        outs[0].append(p[:, kv_col:kv_col + 4 * gw].reshape(1, t, 4, g, hd))
        outs[1].append(p[t - wp:, kv_col + 4 * gw:kv_col + 6 * gw].reshape(1, wp, 2, g, hd))
        outs[2].append(ret_state[None])
        outs[3].append(tails[-1, SUBLANE - (CONV_W - 1):, :ff][None])
        yp = yp_new

        hs = norm_mod(ys, g_pre_mix[l], mods_s, 0, 1, db)
        ps_t = matmul(hs, w_in_l, tm_s, TN_IN, np_)
        ps = jnp.swapaxes(ps_t.reshape(s, db, np_), 0, 1).reshape(db * s, np_)
        mix_ret_s, ret_s = retention_sample(ps, state_ret.reshape(depth * db, h, dkv, dkv), l * db, ret_gn[l], s)
        cache5 = cache_kv.reshape(depth * n_phys, page, 2, PAIR_ROWS, hd)
        kcvc_s = compress_sample(cache5, l * n_phys, page_table, pe2, w1b, w2b, hd)

        def head_rows(a, last):
            a = a.reshape(db, s, g, NSA_REP, last)
            return a.transpose(0, 2, 3, 1, 4).reshape(db, g, NSA_REP * s, last)

        q_s = head_rows(ps[:, q_col:kv_col], hd)
        ps_gate = jnp.swapaxes(matmul(hs, w_gate_b, tm_s, LANE).reshape(s, db, LANE), 0, 1).reshape(db * s, LANE)
        gates_s = head_rows(ps_gate[:, :n_gate], 3)
        kv_new = jnp.pad(ps[:, kv_col:ng_col].reshape(db, s, 6 * gw), ((0, 0), (0, SUBLANE - s), (0, 0)))
        win_rows = ps[:, kv_col + 4 * gw:ng_col].reshape(db, s, PAIR_ROWS, hd)
        o_s, win_new = nsa_sample(cache5, l * n_phys, page_table, q_s, gates_s, kcvc_s, kv_new, win_rows,
                                  cache_win.reshape(depth * db, wb, PAIR_ROWS, hd), l * db, s)
        nsa_o_s = o_s.reshape(db, g, NSA_REP, s, hd).transpose(0, 3, 1, 2, 4).reshape(db * s, NSA_HEADS * hd)
        mix_s = jnp.concatenate([mix_ret_s, nsa_o_s], axis=1)
        mix_s = jnp.swapaxes(mix_s.reshape(db, s, -1), 0, 1).reshape(s * db, -1)
        prev = jnp.swapaxes(state_conv[l], 0, 1).reshape((CONV_W - 1) * db, ff)
        ys_new, tail_s = _post(ys, mix_s, mods_s, db, tm_s, *post_w,
                               lambda h2: ffn_up_sample(h2, prev, w_up_l, conv_w_p, conv_b_p, db, TN_FF),
                               w_down_b, g_post_ffn[l])
        outs[4].append(ps[:, kv_col:kv_col + 4 * gw].reshape(db, s, 4, g, hd))
        outs[5].append(win_new.reshape(db, wb, 2, g, hd))
        outs[6].append(ret_s)
        outs[7].append(jnp.swapaxes(tail_s[:, :ff].reshape(CONV_W - 1, db, ff), 0, 1))
        ys = ys_new

    y_prompt = yp.reshape(1, t, d)
    y_sample = jnp.swapaxes(ys.reshape(s, db, d), 0, 1)
    return (y_prompt, y_sample) + tuple(jnp.stack(o) for o in outs)
```

```python
import functools

import jax
import jax.numpy as jnp
import numpy as np
from jax import lax
from jax.experimental import pallas as pl
from jax.experimental.pallas import tpu as pltpu

F32 = jnp.float32
BF16 = jnp.bfloat16

EPS = 1e-6
NEG = -1e30
FORCE = 1e4
RET_HEADS = 8
RET_CHUNK = 128
NSA_HEADS = 16
NSA_KV = 4
NSA_REP = NSA_HEADS // NSA_KV
L_CMP = 32
STRIDE_CMP = 16
L_SEL = 64
N_SELECT = 16
WINDOW = 512
CONV_W = 3

LANE = 128
SUBLANE = 8
VMEM_LIMIT = 56 * 1024 * 1024


def _cp(*sem):
    return pltpu.CompilerParams(dimension_semantics=sem, vmem_limit_bytes=VMEM_LIMIT)


def _silu(x):
    return x * jax.nn.sigmoid(x)


def _nt_dot(a, b):
    return lax.dot_general(a, b, (((1,), (1,)), ((), ())), preferred_element_type=F32)


def _tn_dot(a, b):
    return lax.dot_general(a, b, (((0,), (0,)), ((), ())), preferred_element_type=F32)


def _dot(a, b):
    return jnp.dot(a, b, preferred_element_type=F32)


def _ada_kernel(c_ref, w_ref, b_ref, o_ref):
    sc = _silu(c_ref[...]).astype(BF16)
    o_ref[...] = _dot(sc, w_ref[...].astype(BF16)) + b_ref[...]


def ada_mods(c_all, w_ada, b_ada, tn=512):
    m, d = c_all.shape
    n = w_ada.shape[1]
    return pl.pallas_call(
        _ada_kernel,
        out_shape=jax.ShapeDtypeStruct((m, n), F32),
        grid=(n // tn,),
        in_specs=[pl.BlockSpec((m, d), lambda j: (0, 0)),
                  pl.BlockSpec((d, tn), lambda j: (0, j)),
                  pl.BlockSpec((1, tn), lambda j: (0, j))],
        out_specs=pl.BlockSpec((m, tn), lambda j: (0, j)),
        compiler_params=_cp("arbitrary"),
        name="ada_mods",
    )(c_all, w_ada, b_ada.reshape(1, n))


def _rows(ref):
    v = ref[...]
    return v[0:1, :] if v.shape[0] == SUBLANE else v


def _rms(x, g):
    return x * lax.rsqrt(jnp.mean(x * x, axis=-1, keepdims=True) + EPS) * g


def _norm_mod_kernel(x_ref, g_ref, shift_ref, scale_ref, o_ref):
    y = _rms(x_ref[...], g_ref[...])
    o_ref[...] = (y * (1.0 + _rows(scale_ref)) + _rows(shift_ref)).astype(o_ref.dtype)


def _mod_spec(mods, tm, k, d):
    r = mods.shape[0]
    assert (r == SUBLANE) != (r == tm)
    return pl.BlockSpec((r, d), lambda i, k=k: (0, k))


def norm_mod(x, g, mods, k_shift, k_scale, tm):
    m, d = x.shape
    row = pl.BlockSpec((tm, d), lambda i: (i, 0))
    return pl.pallas_call(
        _norm_mod_kernel,
        out_shape=jax.ShapeDtypeStruct((m, d), BF16),
        grid=(m // tm,),
        in_specs=[row, pl.BlockSpec((1, d), lambda i: (0, 0)),
                  _mod_spec(mods, tm, k_shift, d), _mod_spec(mods, tm, k_scale, d)],
        out_specs=row,
        compiler_params=_cp("parallel"),
        name="norm_mod",
    )(x, g.reshape(1, d), mods, mods)


def _post_mix_kernel(f_ref, x_ref, g1_ref, g2_ref, gate_ref, shift_ref, scale_ref, x2_ref, h2_ref):
    x2 = x_ref[...] + _rows(gate_ref) * _rms(f_ref[...], g1_ref[...])
    x2_ref[...] = x2
    h2_ref[...] = (_rms(x2, g2_ref[...]) * (1.0 + _rows(scale_ref)) + _rows(shift_ref)).astype(h2_ref.dtype)


def post_mix(f, x, g_post_mix, g_pre_ffn, mods, tm):
    m, d = x.shape
    row = pl.BlockSpec((tm, d), lambda i: (i, 0))
    vec = pl.BlockSpec((1, d), lambda i: (0, 0))
    return pl.pallas_call(
        _post_mix_kernel,
        out_shape=(jax.ShapeDtypeStruct((m, d), F32), jax.ShapeDtypeStruct((m, d), BF16)),
        grid=(m // tm,),
        in_specs=[row, row, vec, vec, _mod_spec(mods, tm, 2, d), _mod_spec(mods, tm, 3, d),
                  _mod_spec(mods, tm, 4, d)],
        out_specs=(row, row),
        compiler_params=_cp("parallel"),
        name="post_mix",
    )(f, x, g_post_mix.reshape(1, d), g_pre_ffn.reshape(1, d), mods, mods, mods)


def _post_ffn_kernel(f_ref, x_ref, g_ref, gate_ref, y_ref):
    y_ref[...] = x_ref[...] + _rows(gate_ref) * _rms(f_ref[...], g_ref[...])


def post_ffn(f, x2, g_post_ffn, mods, tm):
    m, d = x2.shape
    row = pl.BlockSpec((tm, d), lambda i: (i, 0))
    return pl.pallas_call(
        _post_ffn_kernel,
        out_shape=jax.ShapeDtypeStruct((m, d), F32),
        grid=(m // tm,),
        in_specs=[row, row, pl.BlockSpec((1, d), lambda i: (0, 0)), _mod_spec(mods, tm, 5, d)],
        out_specs=row,
        compiler_params=_cp("parallel"),
        name="post_ffn",
    )(f, x2, g_post_ffn.reshape(1, d), mods)


def _mm_kernel(a_ref, w_ref, o_ref, *, w_rows_are_outputs):
    w = w_ref[...].astype(BF16)
    o_ref[...] = _nt_dot(a_ref[...], w) if w_rows_are_outputs else _dot(a_ref[...], w)


def matmul(a, w, tm, tn, n=None, w_rows_are_outputs=False):
    m, kd = a.shape
    n_all = w.shape[0] if w_rows_are_outputs else w.shape[1]
    n = n_all if n is None else n
    assert m % tm == 0 and n % tn == 0 and n <= n_all
    w_spec = (pl.BlockSpec((tn, kd), lambda i, j: (j, 0)) if w_rows_are_outputs
              else pl.BlockSpec((kd, tn), lambda i, j: (0, j)))
    return pl.pallas_call(
        functools.partial(_mm_kernel, w_rows_are_outputs=w_rows_are_outputs),
        out_shape=jax.ShapeDtypeStruct((m, n), F32),
        grid=(m // tm, n // tn),
        in_specs=[pl.BlockSpec((tm, kd), lambda i, j: (i, 0)), w_spec],
        out_specs=pl.BlockSpec((tm, tn), lambda i, j: (i, j)),
        compiler_params=_cp("parallel", "parallel"),
        name="matmul",
    )(a, w)


def _ret_out(o, gate, gn):
    mu = jnp.mean(o, axis=-1, keepdims=True)
    var = jnp.mean(jnp.square(o - mu), axis=-1, keepdims=True)
    return (o - mu) * lax.rsqrt(var + EPS) * gn * _silu(gate)


def _ret_prompt_kernel(q_ref, k_ref, v_ref, gate_ref, dmat_ref, dq_ref, dk_ref, gc_ref, gn_ref,
                       o_ref, st_ref, *, dkv, dk_scale):
    @pl.when(pl.program_id(0) == 0)
    def _():
        st_ref[...] = jnp.zeros_like(st_ref)

    heads = range(RET_HEADS)
    sl = [slice(h * dkv, (h + 1) * dkv) for h in heads]
    ks = [k_ref[:, sl[h]] * dk_scale for h in heads]
    qb = [q_ref[:, sl[h]].astype(BF16) for h in heads]
    vb = [v_ref[:, sl[h]].astype(BF16) for h in heads]
    attn = [_nt_dot(qb[h], ks[h].astype(BF16)) * dmat_ref[h] for h in heads]
    carried = [_dot(qb[h], st_ref[h].astype(BF16)) * dq_ref[h] for h in heads]
    o = [_dot(attn[h].astype(BF16), vb[h]) + carried[h] for h in heads]
    for h in heads:
        st_ref[h] = gc_ref[h] * st_ref[h] + _tn_dot((ks[h] * dk_ref[h]).astype(BF16), vb[h])
    for h in heads:
        o_ref[:, sl[h]] = _ret_out(o[h], gate_ref[:, sl[h]], gn_ref[:, sl[h]]).astype(o_ref.dtype)


def _ret_tables(c):
    log_g = jnp.log(1.0 - jnp.exp2(-5.0 - jnp.arange(RET_HEADS, dtype=F32)))
    i = jnp.arange(c, dtype=F32)
    diff = i[:, None] - i[None, :]
    dmat = jnp.where(diff >= 0, jnp.exp(log_g[:, None, None] * jnp.maximum(diff, 0.0)), 0.0)
    dq = jnp.exp(log_g[:, None] * (i[None, :] + 1.0))[:, :, None]
    dk = jnp.exp(log_g[:, None] * (c - 1.0 - i[None, :]))[:, :, None]
    gc = jnp.exp(log_g * c)[:, None, None]
    return dmat, dq, dk, gc


def retention_prompt(p, ret_gn, dkv):
    t = p.shape[0]
    c = RET_CHUNK
    h = RET_HEADS
    dmat, dq, dk, gc = _ret_tables(c)

    w = h * dkv

    def col(part):
        return pl.BlockSpec((c, w), lambda cc, part=part: (cc, part))

    def full(a):
        return pl.BlockSpec(a.shape, lambda cc: (0,) * a.ndim)

    return pl.pallas_call(
        functools.partial(_ret_prompt_kernel, dkv=dkv, dk_scale=dkv ** -0.5),
        out_shape=(jax.ShapeDtypeStruct((t, w), BF16), jax.ShapeDtypeStruct((h, dkv, dkv), F32)),
        grid=(t // c,),
        in_specs=[col(0), col(1), col(2), col(3), full(dmat), full(dq), full(dk), full(gc),
                  pl.BlockSpec((1, w), lambda cc: (0, 0))],
        out_specs=(pl.BlockSpec((c, w), lambda cc: (cc, 0)),
                   pl.BlockSpec((h, dkv, dkv), lambda cc: (0, 0, 0))),
        compiler_params=_cp("arbitrary"),
        name="retention_prompt",
    )(p, p, p, p, dmat, dq, dk, gc, ret_gn.reshape(1, w))


def _ret_sample_kernel(q_ref, k_ref, v_ref, gate_ref, st_ref, dmat_ref, dq_ref, dk_ref, gc_ref, gn_ref,
                       o_ref, nst_ref, *, dkv, s, dk_scale):
    rows = 2 * s
    rid = lax.broadcasted_iota(jnp.int32, (rows, 1), 0)
    first = rid < s
    for h in range(RET_HEADS):
        sl = slice(h * dkv, (h + 1) * dkv)
        q = q_ref[:, sl]
        k = k_ref[:, sl] * dk_scale
        v = v_ref[:, sl]
        attn = _nt_dot(q, k) * dmat_ref[h]
        o = _dot(attn, v)
        dq = dq_ref[h]
        kdec = k * dk_ref[h]
        for b in range(2):
            st = st_ref[b, h]
            mine = first if b == 0 else jnp.logical_not(first)
            o = o + jnp.where(mine, _dot(q, st) * dq, 0.0)
            nst_ref[b, h] = gc_ref[h] * st + _tn_dot(jnp.where(mine, kdec, 0.0), v)
        o_ref[:, sl] = _ret_out(o, gate_ref[:, sl], gn_ref[:, sl]).astype(o_ref.dtype)


def retention_sample(p, state, b0, ret_gn, s):
    _, h, dkv, _ = state.shape
    db = p.shape[0] // s
    rows = 2 * s
    assert rows == SUBLANE and db % 2 == 0
    log_g = jnp.log(1.0 - jnp.exp2(-5.0 - jnp.arange(h, dtype=F32)))
    i = jnp.arange(rows)
    tok = (i % s).astype(F32)
    same = (i[:, None] // s) == (i[None, :] // s)
    diff = tok[:, None] - tok[None, :]
    dmat = jnp.where(same & (diff >= 0), jnp.exp(log_g[:, None, None] * jnp.maximum(diff, 0.0)), 0.0)
    dq = jnp.exp(log_g[:, None] * (tok[None, :] + 1.0))[:, :, None]
    dk = jnp.exp(log_g[:, None] * (s - 1.0 - tok[None, :]))[:, :, None]
    gc = jnp.exp(log_g * s)[:, None, None]
    w = h * dkv

    def col(part):
        return pl.BlockSpec((rows, w), lambda i, part=part: (i, part))

    def full(a):
        return pl.BlockSpec(a.shape, lambda i: (0,) * a.ndim)

    assert b0 % 2 == 0
    st_block = (2, h, dkv, dkv)
    return pl.pallas_call(
        functools.partial(_ret_sample_kernel, dkv=dkv, s=s, dk_scale=dkv ** -0.5),
        out_shape=(jax.ShapeDtypeStruct((db * s, w), BF16), jax.ShapeDtypeStruct((db, h, dkv, dkv), F32)),
        grid=(db // 2,),
        in_specs=[col(0), col(1), col(2), col(3), pl.BlockSpec(st_block, lambda i: (b0 // 2 + i, 0, 0, 0)),
                  full(dmat), full(dq), full(dk), full(gc), pl.BlockSpec((1, w), lambda i: (0, 0))],
        out_specs=(pl.BlockSpec((rows, w), lambda i: (i, 0)), pl.BlockSpec(st_block, lambda i: (i, 0, 0, 0))),
        compiler_params=_cp("parallel"),
        name="retention_sample",
    )(p, p, p, p, state, dmat, dq, dk, gc, ret_gn.reshape(1, w))


def _compress_block(load_rows, pe_ref, w1_ref, w2, accb_ref, nh):
    pairs = L_CMP // 4
    acc_a = None
    acc_b = None
    for i in range(pairs):
        y = jnp.concatenate([load_rows(2 * i), load_rows(2 * i + 1)], axis=1)
        da = _dot((y + pe_ref[i:i + 1, :]).astype(BF16), w1_ref[i])
        db = _dot((y + pe_ref[pairs + i:pairs + i + 1, :]).astype(BF16), w1_ref[pairs + i])
        acc_a = da if acc_a is None else acc_a + da
        acc_b = db if acc_b is None else acc_b + db
    accb_ref[0:nh, :] = acc_b
    accb_ref[nh:nh + SUBLANE, :] = jnp.zeros((SUBLANE, acc_b.shape[1]), F32)
    h = jax.nn.gelu(acc_a + accb_ref[pl.ds(1, nh), :])
    return _dot(h.astype(BF16), w2)


def _masked_softmax(s, mask):
    s = jnp.where(mask, s, NEG)
    e = jnp.exp(s - jnp.max(s, axis=-1, keepdims=True))
    return jnp.where(mask, e / jnp.sum(e, axis=-1, keepdims=True), 0.0)


def _importance(psum, ov):
    hi = psum.astype(BF16)
    lo = (psum - hi.astype(F32)).astype(BF16)
    return _dot(hi, ov) + _dot(lo, ov)


def _select_blocks_t(imp_t, q_pos, n_sel):
    nb = imp_t.shape[0]
    blk = lax.broadcasted_iota(jnp.int32, (nb, 1), 0)
    blkf = blk.astype(F32)
    cur = q_pos // L_SEL
    forced = (blk == 0) | (blk == cur) | (blk == cur - 1)
    score = jnp.where(blk > cur, -1.0, jnp.where(forced, FORCE, imp_t))
    score = jnp.where(blk >= n_sel, -2.0, score)
    taken = -3.0
    work = score
    for _ in range(min(N_SELECT, n_sel)):
        m = jnp.max(work, axis=0, keepdims=True)
        first = jnp.min(jnp.where(work == m, blkf, float(nb)), axis=0, keepdims=True)
        work = jnp.where(blkf == first, taken, work)
    return jnp.where((work == taken) & (score >= 0.0), 1.0, 0.0)


def _overlap_matrix(nc, nb):
    n = jnp.arange(nc)[:, None]
    j = jnp.arange(nb)[None, :]
    c_start = n * STRIDE_CMP
    c_end = c_start + L_CMP - 1
    return ((c_start < (j + 1) * L_SEL) & (c_end >= j * L_SEL)).astype(BF16)


def _expand_matrix(nb, nkeys):
    return (jnp.arange(nb)[:, None] == (jnp.arange(nkeys)[None, :] // L_SEL)).astype(BF16)


def _slopes():
    h = jnp.arange(1, NSA_HEADS + 1, dtype=F32)
    return jnp.exp2(-8.0 * h / NSA_HEADS).reshape(NSA_KV, NSA_REP)


def _compress_prompt_kernel(x_ref, pe_ref, w1_ref, w2_ref, o_ref, accb_ref):
    nh = x_ref.shape[0] // STRIDE_CMP
    o_ref[...] = _compress_block(lambda l: x_ref[pl.ds(l, nh, stride=STRIDE_CMP), :],
                                 pe_ref, w1_ref, w2_ref[...], accb_ref, nh)


def compress_prompt(p, kv_col, cmp_pe, w1b, w2b, hd):
    t = p.shape[0]
    nh = t // STRIDE_CMP
    hid = w1b.shape[-1]
    cb = kv_col // hd
    return pl.pallas_call(
        _compress_prompt_kernel,
        out_shape=jax.ShapeDtypeStruct((2, NSA_KV, nh, hd), F32),
        grid=(2, NSA_KV),
        in_specs=[pl.BlockSpec((t, hd), lambda kind, g: (0, cb + kind * NSA_KV + g)),
                  pl.BlockSpec((None, L_CMP // 2, 2 * hd), lambda kind, g: (kind, 0, 0)),
                  pl.BlockSpec((None, L_CMP // 2, 2 * hd, hid), lambda kind, g: (kind, 0, 0, 0)),
                  pl.BlockSpec((None, hid, hd), lambda kind, g: (kind, 0, 0))],
        out_specs=pl.BlockSpec((None, None, nh, hd), lambda kind, g: (kind, g, 0, 0)),
        scratch_shapes=[pltpu.VMEM((nh + SUBLANE, hid), F32)],
        compiler_params=_cp("parallel", "parallel"),
        name="compress_prompt",
    )(p, cmp_pe, w1b, w2b)


MASK_BIG = 2.0 ** 99
LOG2E = 1.4426950408889634


def _cmp_win_select_kernel(q_ref, kc_ref, vc_ref, ov_ref, sl_ref, gate_ref, *rest, tq, hd, n_sel, n_prev, scale):
    kw_refs = rest[:n_prev + 1]
    vw_refs = rest[n_prev + 1:2 * n_prev + 2]
    part_ref, unsel_ref = rest[2 * n_prev + 2:]
    qt = pl.program_id(1)
    nc = kc_ref.shape[0]
    nw = (n_prev + 1) * tq
    q_pos = qt * tq + lax.broadcasted_iota(jnp.int32, (tq, 1), 0)
    dist_c = q_pos - (lax.broadcasted_iota(jnp.int32, (1, nc), 1) * STRIDE_CMP + (L_CMP - 1))
    mask_c = dist_c >= 0
    dist_cf = dist_c.astype(F32)
    kcb = kc_ref[...].astype(BF16)
    vcb = vc_ref[...].astype(BF16)
    w_pos = (qt - n_prev) * tq + lax.broadcasted_iota(jnp.int32, (1, nw), 1)
    dist_w = q_pos - w_pos
    mask_w = (dist_w >= 0) & (dist_w <= WINDOW) & (w_pos >= 0)
    dist_wf = dist_w.astype(F32)
    kwb = jnp.concatenate([r[...].astype(BF16) for r in kw_refs], axis=0)
    vwb = jnp.concatenate([r[...].astype(BF16) for r in vw_refs], axis=0)
    gate = jax.nn.sigmoid(gate_ref[...])
    heads = range(NSA_REP)
    qbs = [q_ref[:, r * hd:(r + 1) * hd].astype(BF16) for r in heads]
    s_cs = [_nt_dot(qbs[r], kcb) for r in heads]
    s_ws = [_nt_dot(qbs[r], kwb) for r in heads]
    p_cs = [_masked_softmax(s_cs[r] * scale - sl_ref[r][:, 0:1] * dist_cf, mask_c) for r in heads]
    p_ws = [_masked_softmax(s_ws[r] * scale - sl_ref[r][:, 0:1] * dist_wf, mask_w) for r in heads]
    psum = sum(p_cs[1:], p_cs[0])
    for r in heads:
        part_ref[:, r * hd:(r + 1) * hd] = (gate[:, 3 * r:3 * r + 1] * _dot(p_cs[r].astype(BF16), vcb)
                                            + gate[:, 3 * r + 2:3 * r + 3] * _dot(p_ws[r].astype(BF16), vwb))
    q_pos_row = qt * tq + lax.broadcasted_iota(jnp.int32, (1, tq), 1)
    picked_t = _select_blocks_t(_importance(psum, ov_ref[...]).T, q_pos_row, n_sel)
    unsel_ref[...] = (picked_t - 1.0) * MASK_BIG


def cmp_win_select_prompt(p, q_col, kv_col, kcvc, gates, tq=256):
    t = p.shape[0]
    nc, hd = kcvc.shape[2:]
    n_sel = -(-t // L_SEL)
    tq = min(tq, t)
    assert N_SELECT <= n_sel <= LANE and WINDOW % tq == 0 and t % tq == 0
    n_prev = WINDOW // tq
    w = NSA_REP * hd
    cb = kv_col // hd
    slopes = jnp.broadcast_to(_slopes()[:, :, None, None], (NSA_KV, NSA_REP, 1, LANE))

    def win_specs(kind):
        return [pl.BlockSpec((tq, hd), lambda g, i, j=j: (jnp.maximum(i - j, 0), cb + kind * NSA_KV + g))
                for j in range(n_prev, -1, -1)]

    return pl.pallas_call(
        functools.partial(_cmp_win_select_kernel, tq=tq, hd=hd, n_sel=n_sel, n_prev=n_prev, scale=hd ** -0.5),
        out_shape=(jax.ShapeDtypeStruct((t, NSA_KV * w), F32), jax.ShapeDtypeStruct((NSA_KV, LANE, t), F32)),
        grid=(NSA_KV, t // tq),
        in_specs=[pl.BlockSpec((tq, w), lambda g, i: (i, q_col // w + g)),
                  pl.BlockSpec((None, None, nc, hd), lambda g, i: (0, g, 0, 0)),
                  pl.BlockSpec((None, None, nc, hd), lambda g, i: (1, g, 0, 0)),
                  pl.BlockSpec((nc, LANE), lambda g, i: (0, 0)),
                  pl.BlockSpec((None, NSA_REP, 1, LANE), lambda g, i: (g, 0, 0, 0)),
                  pl.BlockSpec((None, tq, 3 * NSA_REP), lambda g, i: (g, i, 0))]
        + win_specs(4) + win_specs(5),
        out_specs=(pl.BlockSpec((tq, w), lambda g, i: (i, g)),
                   pl.BlockSpec((None, LANE, tq), lambda g, i: (g, 0, i))),
        compiler_params=_cp("parallel", "parallel"),
        name="cmp_win_select_prompt",
    )(p, kcvc, kcvc, _overlap_matrix(nc, LANE), slopes, gates, *([p] * (2 * n_prev + 2)))


def _slc_kernel(qi_ref, ki_ref, q_ref, unsel_ref, al_ref, k_ref, vt_ref, kf_ref, gate_ref, part_ref,
                o_ref, qa_ref, al_sc, m_ref, l_ref, acc_ref, *, tq, tk, hd, scale):
    step = pl.program_id(1)
    qt = qi_ref[step]
    kt = ki_ref[step]

    cols = [slice(r * tq, (r + 1) * tq) for r in range(NSA_REP)]

    @pl.when(kt == 0)
    def _():
        q_pos = (qt * tq + lax.broadcasted_iota(jnp.int32, (1, tq), 1)).astype(F32)
        sub = lax.broadcasted_iota(jnp.int32, (SUBLANE, 1), 0)
        for r in range(NSA_REP):
            qa_ref[0:hd, cols[r]] = (q_ref[:, r * hd:(r + 1) * hd] * (scale * LOG2E)).T.astype(BF16)
            tab = jnp.concatenate([al_ref[r]] * (tq // LANE), axis=1)
            slope = tab[ALIBI_SLOPE:ALIBI_SLOPE + 1, :]
            al_sc[:, cols[r]] = jnp.where(sub == ALIBI_ROWCONST, -slope * q_pos,
                                          jnp.where(sub == ALIBI_SLOPE, 0.0, tab))
        qa_ref[hd + FEAT:2 * hd, :] = jnp.zeros((hd - FEAT, NSA_REP * tq), BF16)
        m_ref[...] = jnp.full(m_ref.shape, M_FLOOR, F32)
        l_ref[...] = jnp.zeros(l_ref.shape, F32)
        acc_ref[...] = jnp.zeros(acc_ref.shape, F32)

    picks = unsel_ref[pl.ds(pl.multiple_of(kt * (tk // L_SEL), SUBLANE), tk // L_SEL), :]
    feat = jnp.concatenate([jnp.concatenate([picks] * NSA_REP, axis=1), al_sc[...]], axis=0)
    qa_ref[hd:hd + FEAT, :] = feat.astype(BF16)
    ka = jnp.concatenate([k_ref[...].astype(BF16), kf_ref[...]], axis=1)
    vtb = vt_ref[...].astype(BF16)
    last = (qt * tq + tq - 1) // tk

    def update(scores):
        for r in range(NSA_REP):
            s = scores[r]
            m_old = m_ref[:, cols[r]]
            m_new = jnp.maximum(m_old, jnp.max(s, axis=0, keepdims=True))
            alpha = jnp.exp2(m_old - m_new)
            e = jnp.exp2(s - m_new)
            l_ref[:, cols[r]] = alpha * l_ref[:, cols[r]] + jnp.sum(e, axis=0, keepdims=True)
            acc_ref[:, cols[r]] = alpha * acc_ref[:, cols[r]] + _dot(vtb, e.astype(BF16))
            m_ref[:, cols[r]] = m_new

    @pl.when(kt != last)
    def _():
        update([_dot(ka, qa_ref[:, cols[r]]) for r in range(NSA_REP)])

    @pl.when(kt == last)
    def _():
        causal = (kt * tk + lax.broadcasted_iota(jnp.int32, (tk, 1), 0)
                  <= qt * tq + lax.broadcasted_iota(jnp.int32, (1, tq), 1))
        gate = jax.nn.sigmoid(gate_ref[...])
        update([jnp.where(causal, _dot(ka, qa_ref[:, cols[r]]), NEG) for r in range(NSA_REP)])
        for r in range(NSA_REP):
            sl = slice(r * hd, (r + 1) * hd)
            o_s = (acc_ref[:, cols[r]] / l_ref[:, cols[r]]).T
            o_ref[:, sl] = (part_ref[:, sl] + gate[:, 3 * r + 1:3 * r + 2] * o_s).astype(o_ref.dtype)


FEAT = 2 * SUBLANE
ALIBI_ROWCONST = 6
ALIBI_SLOPE = 7
M_FLOOR = -1e29


def _alibi_tables(t, tk):
    slope = _slopes() * LOG2E
    s1 = slope.astype(BF16).astype(F32)
    s2 = (slope - s1).astype(BF16).astype(F32)
    s3 = (slope - s1 - s2).astype(BF16).astype(F32)
    zero = jnp.zeros_like(slope)
    qf = jnp.stack([s1, s2, s3, s1, s2, s3, zero, slope], axis=-1)
    qf = jnp.broadcast_to(qf[:, :, :, None], qf.shape + (LANE,))
    pos = jnp.arange(t)
    blk = (pos // L_SEL) % (tk // L_SEL)
    onehot = (blk[:, None] == jnp.arange(SUBLANE)[None, :]).astype(F32)
    hi = (pos // L_SEL * L_SEL).astype(F32)[:, None]
    lo = (pos % L_SEL).astype(F32)[:, None]
    one = jnp.ones((t, 1), F32)
    kf = jnp.concatenate([onehot, hi, hi, hi, lo, lo, lo, one, jnp.zeros((t, LANE - FEAT + 1), F32)], axis=1)
    return qf, kf.astype(BF16)


def slc_prompt(p, q_col, kv_col, gates, part, unsel_t, tq=256, tk=512):
    t = p.shape[0]
    hd = part.shape[1] // NSA_HEADS
    w = NSA_REP * hd
    tq, tk = min(tq, t), min(tk, t)
    assert t % tq == 0 and t % tk == 0 and tk % tq == 0 and tk == SUBLANE * L_SEL and hd == LANE
    qi, ki = [], []
    for i in range(t // tq):
        for k in range((i * tq + tq - 1) // tk + 1):
            qi.append(i)
            ki.append(k)
    qi, ki = jnp.asarray(qi, jnp.int32), jnp.asarray(ki, jnp.int32)
    cb = kv_col // hd
    al_q, kfeat = _alibi_tables(t, tk)
    v_t = jnp.swapaxes(p[:, kv_col + 3 * NSA_KV * hd:kv_col + 4 * NSA_KV * hd].reshape(t, NSA_KV, hd), 0, 1)
    v_t = jnp.swapaxes(v_t, 1, 2)
    qrow = lambda g, s, qi, ki: (qi[s], g)
    rows = NSA_REP * tq
    return pl.pallas_call(
        functools.partial(_slc_kernel, tq=tq, tk=tk, hd=hd, scale=hd ** -0.5),
        out_shape=jax.ShapeDtypeStruct((t, NSA_KV * w), BF16),
        grid_spec=pltpu.PrefetchScalarGridSpec(
            num_scalar_prefetch=2,
            grid=(NSA_KV, int(qi.shape[0])),
            in_specs=[pl.BlockSpec((tq, w), lambda g, s, qi, ki: (qi[s], q_col // w + g)),
                      pl.BlockSpec((None, LANE, tq), lambda g, s, qi, ki: (g, 0, qi[s])),
                      pl.BlockSpec((None, NSA_REP, SUBLANE, LANE), lambda g, s, qi, ki: (g, 0, 0, 0)),
                      pl.BlockSpec((tk, hd), lambda g, s, qi, ki: (ki[s], cb + 2 * NSA_KV + g)),
                      pl.BlockSpec((None, hd, tk), lambda g, s, qi, ki: (g, 0, ki[s])),
                      pl.BlockSpec((tk, LANE), lambda g, s, qi, ki: (ki[s], 0)),
                      pl.BlockSpec((None, tq, 3 * NSA_REP), lambda g, s, qi, ki: (g, qi[s], 0)),
                      pl.BlockSpec((tq, w), qrow)],
            out_specs=pl.BlockSpec((tq, w), qrow),
            scratch_shapes=[pltpu.VMEM((2 * hd, rows), BF16), pltpu.VMEM((SUBLANE, rows), F32),
                            pltpu.VMEM((1, rows), F32), pltpu.VMEM((1, rows), F32),
                            pltpu.VMEM((hd, rows), F32)]),
        compiler_params=_cp("parallel", "arbitrary"),
        name="slc_prompt",
    )(qi, ki, p, unsel_t, al_q, p, v_t, kfeat, gates, part)


PAIR_ROWS = 2 * NSA_KV


def _page_specs(n_pages, page, hd, half, page0):
    return [pl.BlockSpec((None, page, None, PAIR_ROWS, hd),
                         lambda b, pt, j=j: (page0 + pt[b * n_pages + j], 0, half, 0, 0))
            for j in range(n_pages)]


def _rows_by_head(x):
    return jnp.swapaxes(x, 0, 1)


def _compress_sample_kernel(pt_ref, *refs, n_pages, hd):
    page_refs = refs[:n_pages]
    pe_ref, w1_ref, w2_ref, o_ref, acc_ref, accb_ref = refs[n_pages:]
    page = page_refs[0].shape[0]
    per_page = page // STRIDE_CMP
    nh = n_pages * per_page
    pairs = L_CMP // 4
    rows = NSA_KV * nh

    def rows_by_head(l):
        x = jnp.stack([pr[m * STRIDE_CMP + l] for pr in page_refs for m in range(per_page)], axis=0)
        return _rows_by_head(x)

    for i in range(pairs):
        ya, yb = rows_by_head(2 * i), rows_by_head(2 * i + 1)
        for kind in range(2):
            sl = slice(kind * NSA_KV, (kind + 1) * NSA_KV)
            y = jnp.concatenate([ya[sl].reshape(rows, hd), yb[sl].reshape(rows, hd)], axis=1)
            for part in range(2):
                j = part * pairs + i
                d = _dot((y + pe_ref[kind, j:j + 1, :]).astype(BF16), w1_ref[kind, j])
                if i == 0:
                    acc_ref[kind, part] = d
                else:
                    acc_ref[kind, part] += d
    for kind in range(2):
        accb_ref[0:rows, :] = acc_ref[kind, 1]
        accb_ref[rows:rows + SUBLANE, :] = jnp.zeros((SUBLANE, accb_ref.shape[1]), F32)
        h = jax.nn.gelu(acc_ref[kind, 0] + accb_ref[pl.ds(1, rows), :])
        o_ref[kind * NSA_KV:(kind + 1) * NSA_KV] = _dot(h.astype(BF16), w2_ref[kind]).reshape(NSA_KV, nh, hd)


def compress_sample(cache5, page0, page_table, pe2, w1b, w2b, hd):
    db, n_pages = page_table.shape
    page = cache5.shape[1]
    nh = n_pages * page // STRIDE_CMP
    hid = w1b.shape[-1]
    assert L_CMP == 2 * STRIDE_CMP
    full = lambda a: pl.BlockSpec(a.shape, lambda b, pt: (0,) * a.ndim)
    return pl.pallas_call(
        functools.partial(_compress_sample_kernel, n_pages=n_pages, hd=hd),
        out_shape=jax.ShapeDtypeStruct((db, 2 * NSA_KV, nh, hd), F32),
        grid_spec=pltpu.PrefetchScalarGridSpec(
            num_scalar_prefetch=1,
            grid=(db,),
            in_specs=_page_specs(n_pages, page, hd, 0, page0) + [full(pe2), full(w1b), full(w2b)],
            out_specs=pl.BlockSpec((None, 2 * NSA_KV, nh, hd), lambda b, pt: (b, 0, 0, 0)),
            scratch_shapes=[pltpu.VMEM((2, 2, NSA_KV * nh, hid), F32),
                            pltpu.VMEM((NSA_KV * nh + SUBLANE, hid), F32)]),
        compiler_params=_cp("parallel"),
        name="compress_sample",
    )(page_table.reshape(-1), *([cache5] * n_pages), pe2, w1b, w2b)


def _softmax2(s_a, mask_a, s_b, mask_b):
    s_a = jnp.where(mask_a, s_a, NEG)
    s_b = jnp.where(mask_b, s_b, NEG)
    m = jnp.maximum(jnp.max(s_a, axis=-1, keepdims=True), jnp.max(s_b, axis=-1, keepdims=True))
    e_a = jnp.where(mask_a, jnp.exp(s_a - m), 0.0)
    e_b = jnp.where(mask_b, jnp.exp(s_b - m), 0.0)
    return e_a, e_b, jnp.sum(e_a, axis=-1, keepdims=True) + jnp.sum(e_b, axis=-1, keepdims=True)


def _nsa_sample_kernel(pt_ref, *refs, n_pages, hd, s_new, scale):
    page_refs = refs[:n_pages]
    (q_ref, gate_ref, kcvc_ref, new_ref, nw_ref, win_ref, ov_ref, e_ref, rm_ref, sl_ref, o_ref, wn_ref,
     imp_ref, kv_ref, kvw_ref) = refs[n_pages:]
    page = page_refs[0].shape[0]
    past = n_pages * page
    wb = win_ref.shape[0]
    gw = NSA_KV * hd
    rows = NSA_REP * s_new
    nc = kcvc_ref.shape[1]
    for j, pr in enumerate(page_refs):
        kv_ref[:, j * page:(j + 1) * page, :] = _rows_by_head(pr[...]).astype(BF16)
    kvw_ref[...] = _rows_by_head(win_ref[...]).astype(BF16)
    n_cmp = (past + s_new - L_CMP) // STRIDE_CMP + 1
    n_sel = -(-(past + s_new) // L_SEL)
    tok = lax.broadcasted_iota(jnp.int32, (rows, 1), 0) % s_new
    q_pos = past + tok
    tnew = lax.broadcasted_iota(jnp.int32, (1, SUBLANE), 1)
    dist_new = tok - tnew
    mask_new = (dist_new >= 0) & (tnew < s_new)
    n = lax.broadcasted_iota(jnp.int32, (1, nc), 1)
    dist_c = q_pos - (n * STRIDE_CMP + (L_CMP - 1))
    mask_c = (dist_c >= 0) & (n < n_cmp)
    imp_ref[...] = jnp.zeros(imp_ref.shape, F32)
    o_cs = []
    for g in range(NSA_KV):
        s_c = (_nt_dot(q_ref[g].astype(BF16), kcvc_ref[g].astype(BF16)) * scale
               - sl_ref[g] * dist_c.astype(F32))
        p_c = _masked_softmax(s_c, mask_c)
        o_cs.append(_dot(p_c.astype(BF16), kcvc_ref[NSA_KV + g].astype(BF16)))
        imp_ref[g * rows:(g + 1) * rows, :] = _importance(_dot(rm_ref[...], p_c), ov_ref[...])
    lane_pos = past + lax.broadcasted_iota(jnp.int32, (1, LANE), 1) % s_new
    picked_all = _select_blocks_t(imp_ref[...].T, lane_pos, n_sel).T
    groups = range(NSA_KV)
    dist_s = q_pos - lax.broadcasted_iota(jnp.int32, (1, past), 1)
    dist_w = wb + tok - lax.broadcasted_iota(jnp.int32, (1, wb), 1)
    mask_w = (dist_w >= 0) & (dist_w <= WINDOW)

    def new_rows(kind, g):
        return new_ref[:, kind * gw + g * hd:kind * gw + (g + 1) * hd]

    def scores(q, keys, dist, g):
        return _nt_dot(q, keys) * scale - sl_ref[g] * dist.astype(F32)

    qs = [q_ref[g].astype(BF16) for g in groups]
    picked = [picked_all[g * rows:(g + 1) * rows, :] for g in groups]
    key_masks = [(_dot(picked[g].astype(BF16), e_ref[...]) > 0.5) & (dist_s >= 0) for g in groups]
    cur_masks = [mask_new & (picked[g][:, past // L_SEL:past // L_SEL + 1] > 0.5) for g in groups]
    s_s = [scores(qs[g], kv_ref[g], dist_s, g) for g in groups]
    s_n = [scores(q_ref[g], new_rows(2, g), dist_new, g) for g in groups]
    s_w = [scores(qs[g], kvw_ref[g], dist_w, g) for g in groups]
    s_wn = [scores(q_ref[g], new_rows(4, g), dist_new, g) for g in groups]
    sm_s = [_softmax2(s_s[g], key_masks[g], s_n[g], cur_masks[g]) for g in groups]
    sm_w = [_softmax2(s_w[g], mask_w, s_wn[g], mask_new) for g in groups]
    for g in groups:
        e_s, e_n, l_s = sm_s[g]
        e_w, e_wn, l_w = sm_w[g]
        o_s = (_dot(e_s.astype(BF16), kv_ref[NSA_KV + g]) + _dot(e_n, new_rows(3, g))) / l_s
        o_w = (_dot(e_w.astype(BF16), kvw_ref[NSA_KV + g]) + _dot(e_wn, new_rows(5, g))) / l_w
        gate = jax.nn.sigmoid(gate_ref[g])
        o_ref[g] = (gate[:, 0:1] * o_cs[g] + gate[:, 1:2] * o_s + gate[:, 2:3] * o_w).astype(o_ref.dtype)
    wn_ref[0:wb - s_new] = win_ref[s_new:wb]
    wn_ref[wb - s_new:wb] = nw_ref[...]


def nsa_sample(cache5, page0, page_table, q, gates, kcvc, kv_new, win_new, win, win0, s_new):
    db, n_pages = page_table.shape
    page = cache5.shape[1]
    past = n_pages * page
    hd = q.shape[-1]
    rows = q.shape[2]
    nc = kcvc.shape[2]
    wb = win.shape[1]
    assert wb <= past and wb <= WINDOW and N_SELECT <= -(-(past + s_new) // L_SEL) <= LANE
    assert past % L_SEL == 0 and s_new <= SUBLANE and s_new <= L_SEL and NSA_KV * rows <= LANE
    i = jnp.arange(rows)
    rm = (i[:, None] % s_new == i[None, :] % s_new).astype(F32)
    slopes = jnp.repeat(_slopes(), s_new, axis=1)[:, :, None]
    ov = _overlap_matrix(nc, LANE)
    em = _expand_matrix(LANE, past)
    per_b = lambda a: pl.BlockSpec((None,) + a.shape[1:], lambda b, pt: (b,) + (0,) * (a.ndim - 1))
    full = lambda a: pl.BlockSpec(a.shape, lambda b, pt: (0,) * a.ndim)
    return pl.pallas_call(
        functools.partial(_nsa_sample_kernel, n_pages=n_pages, hd=hd, s_new=s_new, scale=hd ** -0.5),
        out_shape=(jax.ShapeDtypeStruct(q.shape, BF16), jax.ShapeDtypeStruct((db,) + win.shape[1:], F32)),
        grid_spec=pltpu.PrefetchScalarGridSpec(
            num_scalar_prefetch=1,
            grid=(db,),
            in_specs=_page_specs(n_pages, page, hd, 1, page0)
            + [per_b(q), per_b(gates), per_b(kcvc), per_b(kv_new), per_b(win_new),
               pl.BlockSpec((None,) + win.shape[1:], lambda b, pt: (win0 + b, 0, 0, 0)),
               full(ov), full(em), full(rm), full(slopes)],
            out_specs=(per_b(q), per_b(win)),
            scratch_shapes=[pltpu.VMEM((LANE, LANE), F32), pltpu.VMEM((PAIR_ROWS, past, hd), BF16),
                            pltpu.VMEM((PAIR_ROWS, wb, hd), BF16)]),
        compiler_params=_cp("parallel"),
        name="nsa_sample",
    )(page_table.reshape(-1), *([cache5] * n_pages), q, gates, kcvc, kv_new, win_new, win, ov, em, rm, slopes)


def _ffn_up_prompt_kernel(h_ref, halo_ref, wa_ref, wb_ref, cw_ref, cb_ref, u_ref, tail_ref, ext_ref, *, tm):
    wa = wa_ref[...].astype(BF16)
    h = h_ref[...]
    a = _dot(h, wa)
    prev = _dot(halo_ref[...], wa)
    ext_ref[0:SUBLANE, :] = jnp.where(pl.program_id(0) > 0, prev, 0.0)
    ext_ref[SUBLANE:SUBLANE + tm, :] = a
    y = (cb_ref[...] + cw_ref[0:1, :] * ext_ref[pl.ds(SUBLANE - 2, tm), :]
         + cw_ref[1:2, :] * ext_ref[pl.ds(SUBLANE - 1, tm), :] + cw_ref[2:3, :] * a)
    u_ref[...] = (jax.nn.gelu(y) * _dot(h, wb_ref[...].astype(BF16))).astype(u_ref.dtype)
    tail_ref[...] = a[tm - SUBLANE:tm, :]


def ffn_up_prompt(h2, w_up, conv_w, conv_b, tm, tn):
    m, d = h2.shape
    ff = w_up.shape[1] // 2
    nj = ff // tn
    assert CONV_W == 3 and m % tm == 0 and ff % tn == 0
    return pl.pallas_call(
        functools.partial(_ffn_up_prompt_kernel, tm=tm),
        out_shape=(jax.ShapeDtypeStruct((m, ff), BF16), jax.ShapeDtypeStruct((m // tm, SUBLANE, ff), F32)),
        grid=(m // tm, nj),
        in_specs=[pl.BlockSpec((tm, d), lambda i, j: (i, 0)),
                  pl.BlockSpec((SUBLANE, d), lambda i, j: (jnp.maximum(i * (tm // SUBLANE) - 1, 0), 0)),
                  pl.BlockSpec((d, tn), lambda i, j: (0, j)),
                  pl.BlockSpec((d, tn), lambda i, j: (0, nj + j)),
                  pl.BlockSpec((CONV_W, tn), lambda i, j: (0, j)),
                  pl.BlockSpec((1, tn), lambda i, j: (0, j))],
        out_specs=(pl.BlockSpec((tm, tn), lambda i, j: (i, j)),
                   pl.BlockSpec((None, SUBLANE, tn), lambda i, j: (i, 0, j))),
        scratch_shapes=[pltpu.VMEM((tm + SUBLANE, tn), F32)],
        compiler_params=_cp("parallel", "parallel"),
        name="ffn_up_prompt",
    )(h2, h2, w_up, w_up, conv_w, conv_b.reshape(1, ff))


def _ffn_up_sample_kernel(h_ref, prev_ref, wa_ref, wb_ref, cw_ref, cb_ref, u_ref, tail_ref, ext_ref, *, db, m):
    h = h_ref[...]
    a = _dot(h, wa_ref[...].astype(BF16))
    ext_ref[0:2 * db, :] = prev_ref[...]
    ext_ref[2 * db:2 * db + m, :] = a
    y = (cb_ref[...] + cw_ref[0:1, :] * ext_ref[0:m, :] + cw_ref[1:2, :] * ext_ref[db:db + m, :]
         + cw_ref[2:3, :] * a)
    u_ref[...] = (jax.nn.gelu(y) * _dot(h, wb_ref[...].astype(BF16))).astype(u_ref.dtype)
    tail_ref[...] = ext_ref[m:m + 2 * db, :]


def ffn_up_sample(h2, prev, w_up, conv_w, conv_b, db, tn):
    m, d = h2.shape
    ff = w_up.shape[1] // 2
    nj = ff // tn
    assert CONV_W == 3 and m >= 2 * db and db % SUBLANE == 0
    return pl.pallas_call(
        functools.partial(_ffn_up_sample_kernel, db=db, m=m),
        out_shape=(jax.ShapeDtypeStruct((m, ff), BF16), jax.ShapeDtypeStruct((2 * db, ff), F32)),
        grid=(nj,),
        in_specs=[pl.BlockSpec((m, d), lambda j: (0, 0)),
                  pl.BlockSpec((2 * db, tn), lambda j: (0, j)),
                  pl.BlockSpec((d, tn), lambda j: (0, j)),
                  pl.BlockSpec((d, tn), lambda j: (0, nj + j)),
                  pl.BlockSpec((CONV_W, tn), lambda j: (0, j)),
                  pl.BlockSpec((1, tn), lambda j: (0, j))],
        out_specs=(pl.BlockSpec((m, tn), lambda j: (0, j)), pl.BlockSpec((2 * db, tn), lambda j: (0, j))),
        scratch_shapes=[pltpu.VMEM((m + 2 * db, tn), F32)],
        compiler_params=_cp("parallel"),
        name="ffn_up_sample",
    )(h2, prev, w_up, w_up, conv_w, conv_b.reshape(1, ff))


TN_IN = 512
TN_OUT = 512
TN_FF = 256
TM_DOWN = 512


def _post(x, mix, mods, tm_row, tm_mm, w_out_l, g_post_mix, g_pre_ffn, ffn_up, w_down_b, g_post_ffn):
    f = matmul(mix, w_out_l, tm_mm, TN_OUT)
    x2, h2 = post_mix(f, x, g_post_mix, g_pre_ffn, mods, tm_row)
    u, tail = ffn_up(h2)
    f2 = matmul(u, w_down_b, min(TM_DOWN, tm_mm), TM_DOWN)
    return post_ffn(f2, x2, g_post_ffn, mods, tm_row), tail


def kernel(x_prompt, x_sample, cache_kv, cache_win, state_ret, state_conv, page_table, c_prompt, c_sample,
           w_ada, b_ada, g_pre_mix, w_in, cmp_pe, cmp_w1, cmp_w2, ret_gn, w_out, g_post_mix, g_pre_ffn,
           w_up, conv_w, conv_b, w_down, g_post_ffn):
    depth = w_in.shape[0]
    bp, t, d = x_prompt.shape
    db, s, _ = x_sample.shape
    assert bp == 1
    h, dkv = state_ret.shape[2], state_ret.shape[3]
    n_phys, page, n_rows, g, hd = cache_kv.shape[1:]
    assert h == RET_HEADS and g == NSA_KV and n_rows == 4
    wb = cache_win.shape[2]
    ff = w_up.shape[2] // 2
    rw = h * dkv
    q_col = 4 * rw
    kv_col = q_col + NSA_HEADS * hd
    ng_col = kv_col + 6 * g * hd
    n_gate = NSA_HEADS * 3
    np_ = ng_col
    assert np_ % TN_IN == 0 and ff % TN_FF == 0 and n_gate <= LANE
    gw = g * hd
    tm_p = min(1024, t)
    tm_s = s * db

    yp = x_prompt.reshape(t, d)
    ys = jnp.swapaxes(x_sample, 0, 1).reshape(s * db, d)
    c_all = jnp.concatenate([c_prompt, jnp.zeros((SUBLANE - 1, d), F32), c_sample], axis=0)
    outs = [[] for _ in range(8)]
    for l in range(depth):
        w_in_t = jnp.swapaxes(w_in[l], 0, 1)
        w_gate_t = jnp.pad(w_in_t[np_:], ((0, LANE - n_gate), (0, 0)))
        w_out_l = w_out[l]
        w_up_l = w_up[l]
        w_down_b = w_down[l].astype(BF16)
        conv_w_p = conv_w[l]
        conv_b_p = conv_b[l]
        pe2 = cmp_pe[l].reshape(2, L_CMP // 2, 2 * hd)
        w1b = cmp_w1[l].reshape(2, L_CMP // 2, 2 * hd, -1).astype(BF16)
        w2b = cmp_w2[l].astype(BF16)
        post_w = (w_out_l, g_post_mix[l], g_pre_ffn[l])

        mods = ada_mods(c_all, w_ada[l], b_ada[l])
        mods_p = mods[0:SUBLANE]
        mods_s = mods[SUBLANE:SUBLANE + db]

        hp = norm_mod(yp, g_pre_mix[l], mods_p, 0, 1, 256)
        p = matmul(hp, w_in_t, tm_p, TN_IN, np_, w_rows_are_outputs=True)
        mix_ret, ret_state = retention_prompt(p, ret_gn[l], dkv)
        kcvc = compress_prompt(p, kv_col, pe2, w1b, w2b, hd)
        p_gate = matmul(hp, w_gate_t, tm_p, LANE, w_rows_are_outputs=True)
        gates = jnp.swapaxes(p_gate[:, :n_gate].reshape(t, g, 3 * NSA_REP), 0, 1)
        part, unsel = cmp_win_select_prompt(p, q_col, kv_col, kcvc, gates)
        nsa_o = slc_prompt(p, q_col, kv_col, gates, part, unsel)
        mix = jnp.concatenate([mix_ret, nsa_o], axis=1)
        yp_new, tails = _post(yp, mix, mods_p, 256, tm_p, *post_w,
                              lambda h2: ffn_up_prompt(h2, w_up_l, conv_w_p, conv_b_p, tm_p, TN_FF),
                              w_down_b, g_post_ffn[l])
        wp = min(WINDOW, t)
        outs[0].append(p[:, kv_col:kv_col + 4 * gw].reshape(1, t, 4, g, hd))
        outs[1].append(p[t - wp:, kv_col + 4 * gw:kv_col + 6 * gw].reshape(1, wp, 2, g, hd))
        outs[2].append(ret_state[None])
        outs[3].append(tails[-1, SUBLANE - (CONV_W - 1):, :ff][None])
        yp = yp_new

        hs = norm_mod(ys, g_pre_mix[l], mods_s, 0, 1, db)
        ps_t = matmul(hs, w_in_t, tm_s, TN_IN, np_, w_rows_are_outputs=True)
        ps = jnp.swapaxes(ps_t.reshape(s, db, np_), 0, 1).reshape(db * s, np_)
        mix_ret_s, ret_s = retention_sample(ps, state_ret.reshape(depth * db, h, dkv, dkv), l * db, ret_gn[l], s)
        cache5 = cache_kv.reshape(depth * n_phys, page, 2, PAIR_ROWS, hd)
        kcvc_s = compress_sample(cache5, l * n_phys, page_table, pe2, w1b, w2b, hd)

        def head_rows(a, last):
            a = a.reshape(db, s, g, NSA_REP, last)
            return a.transpose(0, 2, 3, 1, 4).reshape(db, g, NSA_REP * s, last)

        q_s = head_rows(ps[:, q_col:kv_col], hd)
        ps_gate = matmul(hs, w_gate_t, tm_s, LANE, w_rows_are_outputs=True)
        ps_gate = jnp.swapaxes(ps_gate.reshape(s, db, LANE), 0, 1).reshape(db * s, LANE)
        gates_s = head_rows(ps_gate[:, :n_gate], 3)
        kv_new = jnp.pad(ps[:, kv_col:ng_col].reshape(db, s, 6 * gw), ((0, 0), (0, SUBLANE - s), (0, 0)))
        win_rows = ps[:, kv_col + 4 * gw:ng_col].reshape(db, s, PAIR_ROWS, hd)
        o_s, win_new = nsa_sample(cache5, l * n_phys, page_table, q_s, gates_s, kcvc_s, kv_new, win_rows,
                                  cache_win.reshape(depth * db, wb, PAIR_ROWS, hd), l * db, s)
        nsa_o_s = o_s.reshape(db, g, NSA_REP, s, hd).transpose(0, 3, 1, 2, 4).reshape(db * s, NSA_HEADS * hd)
        mix_s = jnp.concatenate([mix_ret_s, nsa_o_s], axis=1)
        mix_s = jnp.swapaxes(mix_s.reshape(db, s, -1), 0, 1).reshape(s * db, -1)
        prev = jnp.swapaxes(state_conv[l], 0, 1).reshape((CONV_W - 1) * db, ff)
        ys_new, tail_s = _post(ys, mix_s, mods_s, db, tm_s, *post_w,
                               lambda h2: ffn_up_sample(h2, prev, w_up_l, conv_w_p, conv_b_p, db, TN_FF),
                               w_down_b, g_post_ffn[l])
        outs[4].append(ps[:, kv_col:kv_col + 4 * gw].reshape(db, s, 4, g, hd))
        outs[5].append(win_new.reshape(db, wb, 2, g, hd))
        outs[6].append(ret_s)
        outs[7].append(jnp.swapaxes(tail_s[:, :ff].reshape(CONV_W - 1, db, ff), 0, 1))
        ys = ys_new

    y_prompt = yp.reshape(1, t, d)
    y_sample = jnp.swapaxes(ys.reshape(s, db, d), 0, 1)
    return (y_prompt, y_sample) + tuple(jnp.stack(o) for o in outs)
```

```python
import functools

import jax
import jax.numpy as jnp
import numpy as np
from jax import lax
from jax.experimental import pallas as pl
from jax.experimental.pallas import tpu as pltpu

F32 = jnp.float32
BF16 = jnp.bfloat16

EPS = 1e-6
NEG = -1e30
FORCE = 1e4
RET_HEADS = 8
RET_CHUNK = 128
NSA_HEADS = 16
NSA_KV = 4
NSA_REP = NSA_HEADS // NSA_KV
L_CMP = 32
STRIDE_CMP = 16
L_SEL = 64
N_SELECT = 16
WINDOW = 512
CONV_W = 3

LANE = 128
SUBLANE = 8
VMEM_LIMIT = 56 * 1024 * 1024


def _cp(*sem):
    return pltpu.CompilerParams(dimension_semantics=sem, vmem_limit_bytes=VMEM_LIMIT)


def _silu(x):
    return x * jax.nn.sigmoid(x)


def _nt_dot(a, b):
    return lax.dot_general(a, b, (((1,), (1,)), ((), ())), preferred_element_type=F32)


def _tn_dot(a, b):
    return lax.dot_general(a, b, (((0,), (0,)), ((), ())), preferred_element_type=F32)


def _dot(a, b):
    return jnp.dot(a, b, preferred_element_type=F32)


def _ada_kernel(c_ref, w_ref, b_ref, o_ref):
    sc = _silu(c_ref[...]).astype(BF16)
    o_ref[...] = _dot(sc, w_ref[...].astype(BF16)) + b_ref[...]


def ada_mods(c_all, w_ada, b_ada, tn=512):
    m, d = c_all.shape
    n = w_ada.shape[1]
    return pl.pallas_call(
        _ada_kernel,
        out_shape=jax.ShapeDtypeStruct((m, n), F32),
        grid=(n // tn,),
        in_specs=[pl.BlockSpec((m, d), lambda j: (0, 0)),
                  pl.BlockSpec((d, tn), lambda j: (0, j)),
                  pl.BlockSpec((1, tn), lambda j: (0, j))],
        out_specs=pl.BlockSpec((m, tn), lambda j: (0, j)),
        compiler_params=_cp("arbitrary"),
        name="ada_mods",
    )(c_all, w_ada, b_ada.reshape(1, n))


def _rows(ref):
    v = ref[...]
    return v[0:1, :] if v.shape[0] == SUBLANE else v


def _rms(x, g):
    return x * lax.rsqrt(jnp.mean(x * x, axis=-1, keepdims=True) + EPS) * g


def _norm_mod_kernel(x_ref, g_ref, shift_ref, scale_ref, o_ref):
    y = _rms(x_ref[...], g_ref[...])
    o_ref[...] = (y * (1.0 + _rows(scale_ref)) + _rows(shift_ref)).astype(o_ref.dtype)


def _mod_spec(mods, tm, k, d):
    r = mods.shape[0]
    assert (r == SUBLANE) != (r == tm)
    return pl.BlockSpec((r, d), lambda i, k=k: (0, k))


def norm_mod(x, g, mods, k_shift, k_scale, tm):
    m, d = x.shape
    row = pl.BlockSpec((tm, d), lambda i: (i, 0))
    return pl.pallas_call(
        _norm_mod_kernel,
        out_shape=jax.ShapeDtypeStruct((m, d), BF16),
        grid=(m // tm,),
        in_specs=[row, pl.BlockSpec((1, d), lambda i: (0, 0)),
                  _mod_spec(mods, tm, k_shift, d), _mod_spec(mods, tm, k_scale, d)],
        out_specs=row,
        compiler_params=_cp("parallel"),
        name="norm_mod",
    )(x, g.reshape(1, d), mods, mods)


def _post_mix_kernel(f_ref, x_ref, g1_ref, g2_ref, gate_ref, shift_ref, scale_ref, x2_ref, h2_ref):
    x2 = x_ref[...] + _rows(gate_ref) * _rms(f_ref[...], g1_ref[...])
    x2_ref[...] = x2
    h2_ref[...] = (_rms(x2, g2_ref[...]) * (1.0 + _rows(scale_ref)) + _rows(shift_ref)).astype(h2_ref.dtype)


def post_mix(f, x, g_post_mix, g_pre_ffn, mods, tm):
    m, d = x.shape
    row = pl.BlockSpec((tm, d), lambda i: (i, 0))
    vec = pl.BlockSpec((1, d), lambda i: (0, 0))
    return pl.pallas_call(
        _post_mix_kernel,
        out_shape=(jax.ShapeDtypeStruct((m, d), F32), jax.ShapeDtypeStruct((m, d), BF16)),
        grid=(m // tm,),
        in_specs=[row, row, vec, vec, _mod_spec(mods, tm, 2, d), _mod_spec(mods, tm, 3, d),
                  _mod_spec(mods, tm, 4, d)],
        out_specs=(row, row),
        compiler_params=_cp("parallel"),
        name="post_mix",
    )(f, x, g_post_mix.reshape(1, d), g_pre_ffn.reshape(1, d), mods, mods, mods)


def _post_ffn_kernel(f_ref, x_ref, g_ref, gate_ref, y_ref):
    y_ref[...] = x_ref[...] + _rows(gate_ref) * _rms(f_ref[...], g_ref[...])


def post_ffn(f, x2, g_post_ffn, mods, tm):
    m, d = x2.shape
    row = pl.BlockSpec((tm, d), lambda i: (i, 0))
    return pl.pallas_call(
        _post_ffn_kernel,
        out_shape=jax.ShapeDtypeStruct((m, d), F32),
        grid=(m // tm,),
        in_specs=[row, row, pl.BlockSpec((1, d), lambda i: (0, 0)), _mod_spec(mods, tm, 5, d)],
        out_specs=row,
        compiler_params=_cp("parallel"),
        name="post_ffn",
    )(f, x2, g_post_ffn.reshape(1, d), mods)


def _mm_kernel(a_ref, w_ref, o_ref, *, w_rows_are_outputs):
    w = w_ref[...].astype(BF16)
    o_ref[...] = _nt_dot(a_ref[...], w) if w_rows_are_outputs else _dot(a_ref[...], w)


def matmul(a, w, tm, tn, n=None, w_rows_are_outputs=False):
    m, kd = a.shape
    n_all = w.shape[0] if w_rows_are_outputs else w.shape[1]
    n = n_all if n is None else n
    assert m % tm == 0 and n % tn == 0 and n <= n_all
    w_spec = (pl.BlockSpec((tn, kd), lambda i, j: (j, 0)) if w_rows_are_outputs
              else pl.BlockSpec((kd, tn), lambda i, j: (0, j)))
    return pl.pallas_call(
        functools.partial(_mm_kernel, w_rows_are_outputs=w_rows_are_outputs),
        out_shape=jax.ShapeDtypeStruct((m, n), F32),
        grid=(m // tm, n // tn),
        in_specs=[pl.BlockSpec((tm, kd), lambda i, j: (i, 0)), w_spec],
        out_specs=pl.BlockSpec((tm, tn), lambda i, j: (i, j)),
        compiler_params=_cp("parallel", "parallel"),
        name="matmul",
    )(a, w)


def _ret_out(o, gate, gn):
    mu = jnp.mean(o, axis=-1, keepdims=True)
    var = jnp.mean(jnp.square(o - mu), axis=-1, keepdims=True)
    return (o - mu) * lax.rsqrt(var + EPS) * gn * _silu(gate)


def _ret_prompt_kernel(q_ref, k_ref, v_ref, gate_ref, dmat_ref, dq_ref, dk_ref, gc_ref, gn_ref,
                       o_ref, st_ref, *, dkv, dk_scale):
    @pl.when(pl.program_id(0) == 0)
    def _():
        st_ref[...] = jnp.zeros_like(st_ref)

    heads = range(RET_HEADS)
    sl = [slice(h * dkv, (h + 1) * dkv) for h in heads]
    ks = [k_ref[:, sl[h]] * dk_scale for h in heads]
    qb = [q_ref[:, sl[h]].astype(BF16) for h in heads]
    vb = [v_ref[:, sl[h]].astype(BF16) for h in heads]
    attn = [_nt_dot(qb[h], ks[h].astype(BF16)) * dmat_ref[h] for h in heads]
    carried = [_dot(qb[h], st_ref[h].astype(BF16)) * dq_ref[h] for h in heads]
    o = [_dot(attn[h].astype(BF16), vb[h]) + carried[h] for h in heads]
    for h in heads:
        st_ref[h] = gc_ref[h] * st_ref[h] + _tn_dot((ks[h] * dk_ref[h]).astype(BF16), vb[h])
    for h in heads:
        o_ref[:, sl[h]] = _ret_out(o[h], gate_ref[:, sl[h]], gn_ref[:, sl[h]]).astype(o_ref.dtype)


def _ret_tables(c):
    log_g = jnp.log(1.0 - jnp.exp2(-5.0 - jnp.arange(RET_HEADS, dtype=F32)))
    i = jnp.arange(c, dtype=F32)
    diff = i[:, None] - i[None, :]
    dmat = jnp.where(diff >= 0, jnp.exp(log_g[:, None, None] * jnp.maximum(diff, 0.0)), 0.0)
    dq = jnp.exp(log_g[:, None] * (i[None, :] + 1.0))[:, :, None]
    dk = jnp.exp(log_g[:, None] * (c - 1.0 - i[None, :]))[:, :, None]
    gc = jnp.exp(log_g * c)[:, None, None]
    return dmat, dq, dk, gc


def retention_prompt(p, ret_gn, dkv):
    t = p.shape[0]
    c = RET_CHUNK
    h = RET_HEADS
    dmat, dq, dk, gc = _ret_tables(c)

    w = h * dkv

    def col(part):
        return pl.BlockSpec((c, w), lambda cc, part=part: (cc, part))

    def full(a):
        return pl.BlockSpec(a.shape, lambda cc: (0,) * a.ndim)

    return pl.pallas_call(
        functools.partial(_ret_prompt_kernel, dkv=dkv, dk_scale=dkv ** -0.5),
        out_shape=(jax.ShapeDtypeStruct((t, w), BF16), jax.ShapeDtypeStruct((h, dkv, dkv), F32)),
        grid=(t // c,),
        in_specs=[col(0), col(1), col(2), col(3), full(dmat), full(dq), full(dk), full(gc),
                  pl.BlockSpec((1, w), lambda cc: (0, 0))],
        out_specs=(pl.BlockSpec((c, w), lambda cc: (cc, 0)),
                   pl.BlockSpec((h, dkv, dkv), lambda cc: (0, 0, 0))),
        compiler_params=_cp("arbitrary"),
        name="retention_prompt",
    )(p, p, p, p, dmat, dq, dk, gc, ret_gn.reshape(1, w))


def _ret_sample_kernel(q_ref, k_ref, v_ref, gate_ref, st_ref, dmat_ref, dq_ref, dk_ref, gc_ref, gn_ref,
                       o_ref, nst_ref, *, dkv, s, dk_scale):
    rows = 2 * s
    rid = lax.broadcasted_iota(jnp.int32, (rows, 1), 0)
    first = rid < s
    for h in range(RET_HEADS):
        sl = slice(h * dkv, (h + 1) * dkv)
        q = q_ref[:, sl]
        k = k_ref[:, sl] * dk_scale
        v = v_ref[:, sl]
        attn = _nt_dot(q, k) * dmat_ref[h]
        o = _dot(attn, v)
        dq = dq_ref[h]
        kdec = k * dk_ref[h]
        for b in range(2):
            st = st_ref[b, h]
            mine = first if b == 0 else jnp.logical_not(first)
            o = o + jnp.where(mine, _dot(q, st) * dq, 0.0)
            nst_ref[b, h] = gc_ref[h] * st + _tn_dot(jnp.where(mine, kdec, 0.0), v)
        o_ref[:, sl] = _ret_out(o, gate_ref[:, sl], gn_ref[:, sl]).astype(o_ref.dtype)


def retention_sample(p, state, b0, ret_gn, s):
    _, h, dkv, _ = state.shape
    db = p.shape[0] // s
    rows = 2 * s
    assert rows == SUBLANE and db % 2 == 0
    log_g = jnp.log(1.0 - jnp.exp2(-5.0 - jnp.arange(h, dtype=F32)))
    i = jnp.arange(rows)
    tok = (i % s).astype(F32)
    same = (i[:, None] // s) == (i[None, :] // s)
    diff = tok[:, None] - tok[None, :]
    dmat = jnp.where(same & (diff >= 0), jnp.exp(log_g[:, None, None] * jnp.maximum(diff, 0.0)), 0.0)
    dq = jnp.exp(log_g[:, None] * (tok[None, :] + 1.0))[:, :, None]
    dk = jnp.exp(log_g[:, None] * (s - 1.0 - tok[None, :]))[:, :, None]
    gc = jnp.exp(log_g * s)[:, None, None]
    w = h * dkv

    def col(part):
        return pl.BlockSpec((rows, w), lambda i, part=part: (i, part))

    def full(a):
        return pl.BlockSpec(a.shape, lambda i: (0,) * a.ndim)

    assert b0 % 2 == 0
    st_block = (2, h, dkv, dkv)
    return pl.pallas_call(
        functools.partial(_ret_sample_kernel, dkv=dkv, s=s, dk_scale=dkv ** -0.5),
        out_shape=(jax.ShapeDtypeStruct((db * s, w), BF16), jax.ShapeDtypeStruct((db, h, dkv, dkv), F32)),
        grid=(db // 2,),
        in_specs=[col(0), col(1), col(2), col(3), pl.BlockSpec(st_block, lambda i: (b0 // 2 + i, 0, 0, 0)),
                  full(dmat), full(dq), full(dk), full(gc), pl.BlockSpec((1, w), lambda i: (0, 0))],
        out_specs=(pl.BlockSpec((rows, w), lambda i: (i, 0)), pl.BlockSpec(st_block, lambda i: (i, 0, 0, 0))),
        compiler_params=_cp("parallel"),
        name="retention_sample",
    )(p, p, p, p, state, dmat, dq, dk, gc, ret_gn.reshape(1, w))


def _compress_block(load_rows, pe_ref, w1_ref, w2, accb_ref, nh):
    pairs = L_CMP // 4
    acc_a = None
    acc_b = None
    for i in range(pairs):
        y = jnp.concatenate([load_rows(2 * i), load_rows(2 * i + 1)], axis=1)
        da = _dot((y + pe_ref[i:i + 1, :]).astype(BF16), w1_ref[i])
        db = _dot((y + pe_ref[pairs + i:pairs + i + 1, :]).astype(BF16), w1_ref[pairs + i])
        acc_a = da if acc_a is None else acc_a + da
        acc_b = db if acc_b is None else acc_b + db
    accb_ref[0:nh, :] = acc_b
    accb_ref[nh:nh + SUBLANE, :] = jnp.zeros((SUBLANE, acc_b.shape[1]), F32)
    h = jax.nn.gelu(acc_a + accb_ref[pl.ds(1, nh), :])
    return _dot(h.astype(BF16), w2)


def _masked_softmax(s, mask):
    s = jnp.where(mask, s, NEG)
    e = jnp.exp(s - jnp.max(s, axis=-1, keepdims=True))
    return jnp.where(mask, e / jnp.sum(e, axis=-1, keepdims=True), 0.0)


def _importance(psum, ov):
    hi = psum.astype(BF16)
    lo = (psum - hi.astype(F32)).astype(BF16)
    return _dot(hi, ov) + _dot(lo, ov)


def _select_blocks_t(imp_t, q_pos, n_sel):
    nb = imp_t.shape[0]
    blk = lax.broadcasted_iota(jnp.int32, (nb, 1), 0)
    blkf = blk.astype(F32)
    cur = q_pos // L_SEL
    forced = (blk == 0) | (blk == cur) | (blk == cur - 1)
    score = jnp.where(blk > cur, -1.0, jnp.where(forced, FORCE, imp_t))
    score = jnp.where(blk >= n_sel, -2.0, score)
    taken = -3.0
    work = score
    for _ in range(min(N_SELECT, n_sel)):
        m = jnp.max(work, axis=0, keepdims=True)
        first = jnp.min(jnp.where(work == m, blkf, float(nb)), axis=0, keepdims=True)
        work = jnp.where(blkf == first, taken, work)
    return jnp.where((work == taken) & (score >= 0.0), 1.0, 0.0)


def _overlap_matrix(nc, nb):
    n = jnp.arange(nc)[:, None]
    j = jnp.arange(nb)[None, :]
    c_start = n * STRIDE_CMP
    c_end = c_start + L_CMP - 1
    return ((c_start < (j + 1) * L_SEL) & (c_end >= j * L_SEL)).astype(BF16)


def _expand_matrix(nb, nkeys):
    return (jnp.arange(nb)[:, None] == (jnp.arange(nkeys)[None, :] // L_SEL)).astype(BF16)


def _slopes():
    h = jnp.arange(1, NSA_HEADS + 1, dtype=F32)
    return jnp.exp2(-8.0 * h / NSA_HEADS).reshape(NSA_KV, NSA_REP)


def _compress_prompt_kernel(x_ref, pe_ref, w1_ref, w2_ref, o_ref, accb_ref):
    nh = x_ref.shape[0] // STRIDE_CMP
    o_ref[...] = _compress_block(lambda l: x_ref[pl.ds(l, nh, stride=STRIDE_CMP), :],
                                 pe_ref, w1_ref, w2_ref[...], accb_ref, nh)


def compress_prompt(p, kv_col, cmp_pe, w1b, w2b, hd):
    t = p.shape[0]
    nh = t // STRIDE_CMP
    hid = w1b.shape[-1]
    cb = kv_col // hd
    return pl.pallas_call(
        _compress_prompt_kernel,
        out_shape=jax.ShapeDtypeStruct((2, NSA_KV, nh, hd), F32),
        grid=(2, NSA_KV),
        in_specs=[pl.BlockSpec((t, hd), lambda kind, g: (0, cb + kind * NSA_KV + g)),
                  pl.BlockSpec((None, L_CMP // 2, 2 * hd), lambda kind, g: (kind, 0, 0)),
                  pl.BlockSpec((None, L_CMP // 2, 2 * hd, hid), lambda kind, g: (kind, 0, 0, 0)),
                  pl.BlockSpec((None, hid, hd), lambda kind, g: (kind, 0, 0))],
        out_specs=pl.BlockSpec((None, None, nh, hd), lambda kind, g: (kind, g, 0, 0)),
        scratch_shapes=[pltpu.VMEM((nh + SUBLANE, hid), F32)],
        compiler_params=_cp("parallel", "parallel"),
        name="compress_prompt",
    )(p, cmp_pe, w1b, w2b)


MASK_BIG = 2.0 ** 99
LOG2E = 1.4426950408889634


def _cmp_win_select_kernel(q_ref, kc_ref, vc_ref, ov_ref, sl_ref, gate_ref, *rest, tq, hd, n_sel, n_prev, scale):
    kw_refs = rest[:n_prev + 1]
    vw_refs = rest[n_prev + 1:2 * n_prev + 2]
    part_ref, unsel_ref = rest[2 * n_prev + 2:]
    qt = pl.program_id(1)
    nc = kc_ref.shape[0]
    nw = (n_prev + 1) * tq
    q_pos = qt * tq + lax.broadcasted_iota(jnp.int32, (tq, 1), 0)
    dist_c = q_pos - (lax.broadcasted_iota(jnp.int32, (1, nc), 1) * STRIDE_CMP + (L_CMP - 1))
    mask_c = dist_c >= 0
    dist_cf = dist_c.astype(F32)
    kcb = kc_ref[...].astype(BF16)
    vcb = vc_ref[...].astype(BF16)
    w_pos = (qt - n_prev) * tq + lax.broadcasted_iota(jnp.int32, (1, nw), 1)
    dist_w = q_pos - w_pos
    mask_w = (dist_w >= 0) & (dist_w <= WINDOW) & (w_pos >= 0)
    dist_wf = dist_w.astype(F32)
    kwb = jnp.concatenate([r[...].astype(BF16) for r in kw_refs], axis=0)
    vwb = jnp.concatenate([r[...].astype(BF16) for r in vw_refs], axis=0)
    gate = jax.nn.sigmoid(gate_ref[...])
    heads = range(NSA_REP)
    qbs = [q_ref[:, r * hd:(r + 1) * hd].astype(BF16) for r in heads]
    s_cs = [_nt_dot(qbs[r], kcb) for r in heads]
    s_ws = [_nt_dot(qbs[r], kwb) for r in heads]
    p_cs = [_masked_softmax(s_cs[r] * scale - sl_ref[r][:, 0:1] * dist_cf, mask_c) for r in heads]
    p_ws = [_masked_softmax(s_ws[r] * scale - sl_ref[r][:, 0:1] * dist_wf, mask_w) for r in heads]
    psum = sum(p_cs[1:], p_cs[0])
    for r in heads:
        part_ref[:, r * hd:(r + 1) * hd] = (gate[:, 3 * r:3 * r + 1] * _dot(p_cs[r].astype(BF16), vcb)
                                            + gate[:, 3 * r + 2:3 * r + 3] * _dot(p_ws[r].astype(BF16), vwb))
    q_pos_row = qt * tq + lax.broadcasted_iota(jnp.int32, (1, tq), 1)
    picked_t = _select_blocks_t(_importance(psum, ov_ref[...]).T, q_pos_row, n_sel)
    unsel_ref[...] = (picked_t - 1.0) * MASK_BIG


def cmp_win_select_prompt(p, q_col, kv_col, kcvc, gates, tq=256):
    t = p.shape[0]
    nc, hd = kcvc.shape[2:]
    n_sel = -(-t // L_SEL)
    tq = min(tq, t)
    assert N_SELECT <= n_sel <= LANE and WINDOW % tq == 0 and t % tq == 0
    n_prev = WINDOW // tq
    w = NSA_REP * hd
    cb = kv_col // hd
    slopes = jnp.broadcast_to(_slopes()[:, :, None, None], (NSA_KV, NSA_REP, 1, LANE))

    def win_specs(kind):
        return [pl.BlockSpec((tq, hd), lambda g, i, j=j: (jnp.maximum(i - j, 0), cb + kind * NSA_KV + g))
                for j in range(n_prev, -1, -1)]

    return pl.pallas_call(
        functools.partial(_cmp_win_select_kernel, tq=tq, hd=hd, n_sel=n_sel, n_prev=n_prev, scale=hd ** -0.5),
        out_shape=(jax.ShapeDtypeStruct((t, NSA_KV * w), F32), jax.ShapeDtypeStruct((NSA_KV, LANE, t), F32)),
        grid=(NSA_KV, t // tq),
        in_specs=[pl.BlockSpec((tq, w), lambda g, i: (i, q_col // w + g)),
                  pl.BlockSpec((None, None, nc, hd), lambda g, i: (0, g, 0, 0)),
                  pl.BlockSpec((None, None, nc, hd), lambda g, i: (1, g, 0, 0)),
                  pl.BlockSpec((nc, LANE), lambda g, i: (0, 0)),
                  pl.BlockSpec((None, NSA_REP, 1, LANE), lambda g, i: (g, 0, 0, 0)),
                  pl.BlockSpec((None, tq, 3 * NSA_REP), lambda g, i: (g, i, 0))]
        + win_specs(4) + win_specs(5),
        out_specs=(pl.BlockSpec((tq, w), lambda g, i: (i, g)),
                   pl.BlockSpec((None, LANE, tq), lambda g, i: (g, 0, i))),
        compiler_params=_cp("parallel", "parallel"),
        name="cmp_win_select_prompt",
    )(p, kcvc, kcvc, _overlap_matrix(nc, LANE), slopes, gates, *([p] * (2 * n_prev + 2)))


def _slc_kernel(qi_ref, ki_ref, q_ref, unsel_ref, al_ref, k_ref, vt_ref, kf_ref, gate_ref, part_ref,
                o_ref, qa_ref, al_sc, m_ref, l_ref, acc_ref, *, tq, tk, hd, scale):
    step = pl.program_id(1)
    qt = qi_ref[step]
    kt = ki_ref[step]

    cols = [slice(r * tq, (r + 1) * tq) for r in range(NSA_REP)]

    @pl.when(kt == 0)
    def _():
        q_pos = (qt * tq + lax.broadcasted_iota(jnp.int32, (1, tq), 1)).astype(F32)
        sub = lax.broadcasted_iota(jnp.int32, (SUBLANE, 1), 0)
        for r in range(NSA_REP):
            qa_ref[0:hd, cols[r]] = (q_ref[:, r * hd:(r + 1) * hd] * (scale * LOG2E)).T.astype(BF16)
            tab = jnp.concatenate([al_ref[r]] * (tq // LANE), axis=1)
            slope = tab[ALIBI_SLOPE:ALIBI_SLOPE + 1, :]
            al_sc[:, cols[r]] = jnp.where(sub == ALIBI_ROWCONST, -slope * q_pos,
                                          jnp.where(sub == ALIBI_SLOPE, 0.0, tab))
        qa_ref[hd + FEAT:2 * hd, :] = jnp.zeros((hd - FEAT, NSA_REP * tq), BF16)
        m_ref[...] = jnp.full(m_ref.shape, M_FLOOR, F32)
        l_ref[...] = jnp.zeros(l_ref.shape, F32)
        acc_ref[...] = jnp.zeros(acc_ref.shape, F32)

    picks = unsel_ref[pl.ds(pl.multiple_of(kt * (tk // L_SEL), SUBLANE), tk // L_SEL), :]
    feat = jnp.concatenate([jnp.concatenate([picks] * NSA_REP, axis=1), al_sc[...]], axis=0)
    qa_ref[hd:hd + FEAT, :] = feat.astype(BF16)
    ka = jnp.concatenate([k_ref[...].astype(BF16), kf_ref[...]], axis=1)
    vtb = vt_ref[...].astype(BF16)
    last = (qt * tq + tq - 1) // tk

    def update(scores):
        for r in range(NSA_REP):
            s = scores[r]
            m_old = m_ref[:, cols[r]]
            m_new = jnp.maximum(m_old, jnp.max(s, axis=0, keepdims=True))
            alpha = jnp.exp2(m_old - m_new)
            e = jnp.exp2(s - m_new)
            l_ref[:, cols[r]] = alpha * l_ref[:, cols[r]] + jnp.sum(e, axis=0, keepdims=True)
            acc_ref[:, cols[r]] = alpha * acc_ref[:, cols[r]] + _dot(vtb, e.astype(BF16))
            m_ref[:, cols[r]] = m_new

    @pl.when(kt != last)
    def _():
        update([_dot(ka, qa_ref[:, cols[r]]) for r in range(NSA_REP)])

    @pl.when(kt == last)
    def _():
        causal = (kt * tk + lax.broadcasted_iota(jnp.int32, (tk, 1), 0)
                  <= qt * tq + lax.broadcasted_iota(jnp.int32, (1, tq), 1))
        gate = jax.nn.sigmoid(gate_ref[...])
        update([jnp.where(causal, _dot(ka, qa_ref[:, cols[r]]), NEG) for r in range(NSA_REP)])
        for r in range(NSA_REP):
            sl = slice(r * hd, (r + 1) * hd)
            o_s = (acc_ref[:, cols[r]] / l_ref[:, cols[r]]).T
            o_ref[:, sl] = (part_ref[:, sl] + gate[:, 3 * r + 1:3 * r + 2] * o_s).astype(o_ref.dtype)


FEAT = 2 * SUBLANE
ALIBI_ROWCONST = 6
ALIBI_SLOPE = 7
M_FLOOR = -1e29


def _alibi_tables(t, tk):
    slope = _slopes() * LOG2E
    s1 = slope.astype(BF16).astype(F32)
    s2 = (slope - s1).astype(BF16).astype(F32)
    s3 = (slope - s1 - s2).astype(BF16).astype(F32)
    zero = jnp.zeros_like(slope)
    qf = jnp.stack([s1, s2, s3, s1, s2, s3, zero, slope], axis=-1)
    qf = jnp.broadcast_to(qf[:, :, :, None], qf.shape + (LANE,))
    pos = jnp.arange(t)
    blk = (pos // L_SEL) % (tk // L_SEL)
    onehot = (blk[:, None] == jnp.arange(SUBLANE)[None, :]).astype(F32)
    hi = (pos // L_SEL * L_SEL).astype(F32)[:, None]
    lo = (pos % L_SEL).astype(F32)[:, None]
    one = jnp.ones((t, 1), F32)
    kf = jnp.concatenate([onehot, hi, hi, hi, lo, lo, lo, one, jnp.zeros((t, LANE - FEAT + 1), F32)], axis=1)
    return qf, kf.astype(BF16)


def slc_prompt(p, q_col, kv_col, gates, part, unsel_t, tq=512, tk=512):
    t = p.shape[0]
    hd = part.shape[1] // NSA_HEADS
    w = NSA_REP * hd
    tq, tk = min(tq, t), min(tk, t)
    assert t % tq == 0 and t % tk == 0 and tk % tq == 0 and tk == SUBLANE * L_SEL and hd == LANE
    qi, ki = [], []
    for i in range(t // tq):
        for k in range((i * tq + tq - 1) // tk + 1):
            qi.append(i)
            ki.append(k)
    qi, ki = jnp.asarray(qi, jnp.int32), jnp.asarray(ki, jnp.int32)
    cb = kv_col // hd
    al_q, kfeat = _alibi_tables(t, tk)
    v_t = jnp.swapaxes(p[:, kv_col + 3 * NSA_KV * hd:kv_col + 4 * NSA_KV * hd].reshape(t, NSA_KV, hd), 0, 1)
    v_t = jnp.swapaxes(v_t, 1, 2)
    qrow = lambda g, s, qi, ki: (qi[s], g)
    rows = NSA_REP * tq
    return pl.pallas_call(
        functools.partial(_slc_kernel, tq=tq, tk=tk, hd=hd, scale=hd ** -0.5),
        out_shape=jax.ShapeDtypeStruct((t, NSA_KV * w), BF16),
        grid_spec=pltpu.PrefetchScalarGridSpec(
            num_scalar_prefetch=2,
            grid=(NSA_KV, int(qi.shape[0])),
            in_specs=[pl.BlockSpec((tq, w), lambda g, s, qi, ki: (qi[s], q_col // w + g)),
                      pl.BlockSpec((None, LANE, tq), lambda g, s, qi, ki: (g, 0, qi[s])),
                      pl.BlockSpec((None, NSA_REP, SUBLANE, LANE), lambda g, s, qi, ki: (g, 0, 0, 0)),
                      pl.BlockSpec((tk, hd), lambda g, s, qi, ki: (ki[s], cb + 2 * NSA_KV + g)),
                      pl.BlockSpec((None, hd, tk), lambda g, s, qi, ki: (g, 0, ki[s])),
                      pl.BlockSpec((tk, LANE), lambda g, s, qi, ki: (ki[s], 0)),
                      pl.BlockSpec((None, tq, 3 * NSA_REP), lambda g, s, qi, ki: (g, qi[s], 0)),
                      pl.BlockSpec((tq, w), qrow)],
            out_specs=pl.BlockSpec((tq, w), qrow),
            scratch_shapes=[pltpu.VMEM((2 * hd, rows), BF16), pltpu.VMEM((SUBLANE, rows), F32),
                            pltpu.VMEM((1, rows), F32), pltpu.VMEM((1, rows), F32),
                            pltpu.VMEM((hd, rows), F32)]),
        compiler_params=_cp("parallel", "arbitrary"),
        name="slc_prompt",
    )(qi, ki, p, unsel_t, al_q, p, v_t, kfeat, gates, part)


PAIR_ROWS = 2 * NSA_KV


def _page_specs(n_pages, page, hd, half, page0):
    return [pl.BlockSpec((None, page, None, PAIR_ROWS, hd),
                         lambda b, pt, j=j: (page0 + pt[b * n_pages + j], 0, half, 0, 0))
            for j in range(n_pages)]


def _rows_by_head(x):
    return jnp.swapaxes(x, 0, 1)


def _compress_sample_kernel(pt_ref, *refs, n_pages, hd):
    page_refs = refs[:n_pages]
    pe_ref, w1_ref, w2_ref, o_ref, acc_ref, accb_ref = refs[n_pages:]
    page = page_refs[0].shape[0]
    per_page = page // STRIDE_CMP
    nh = n_pages * per_page
    pairs = L_CMP // 4
    rows = NSA_KV * nh

    def rows_by_head(l):
        x = jnp.stack([pr[m * STRIDE_CMP + l] for pr in page_refs for m in range(per_page)], axis=0)
        return _rows_by_head(x)

    for i in range(pairs):
        ya, yb = rows_by_head(2 * i), rows_by_head(2 * i + 1)
        for kind in range(2):
            sl = slice(kind * NSA_KV, (kind + 1) * NSA_KV)
            y = jnp.concatenate([ya[sl].reshape(rows, hd), yb[sl].reshape(rows, hd)], axis=1)
            for part in range(2):
                j = part * pairs + i
                d = _dot((y + pe_ref[kind, j:j + 1, :]).astype(BF16), w1_ref[kind, j])
                if i == 0:
                    acc_ref[kind, part] = d
                else:
                    acc_ref[kind, part] += d
    for kind in range(2):
        accb_ref[0:rows, :] = acc_ref[kind, 1]
        accb_ref[rows:rows + SUBLANE, :] = jnp.zeros((SUBLANE, accb_ref.shape[1]), F32)
        h = jax.nn.gelu(acc_ref[kind, 0] + accb_ref[pl.ds(1, rows), :])
        o_ref[kind * NSA_KV:(kind + 1) * NSA_KV] = _dot(h.astype(BF16), w2_ref[kind]).reshape(NSA_KV, nh, hd)


def compress_sample(cache5, page0, page_table, pe2, w1b, w2b, hd):
    db, n_pages = page_table.shape
    page = cache5.shape[1]
    nh = n_pages * page // STRIDE_CMP
    hid = w1b.shape[-1]
    assert L_CMP == 2 * STRIDE_CMP
    full = lambda a: pl.BlockSpec(a.shape, lambda b, pt: (0,) * a.ndim)
    return pl.pallas_call(
        functools.partial(_compress_sample_kernel, n_pages=n_pages, hd=hd),
        out_shape=jax.ShapeDtypeStruct((db, 2 * NSA_KV, nh, hd), F32),
        grid_spec=pltpu.PrefetchScalarGridSpec(
            num_scalar_prefetch=1,
            grid=(db,),
            in_specs=_page_specs(n_pages, page, hd, 0, page0) + [full(pe2), full(w1b), full(w2b)],
            out_specs=pl.BlockSpec((None, 2 * NSA_KV, nh, hd), lambda b, pt: (b, 0, 0, 0)),
            scratch_shapes=[pltpu.VMEM((2, 2, NSA_KV * nh, hid), F32),
                            pltpu.VMEM((NSA_KV * nh + SUBLANE, hid), F32)]),
        compiler_params=_cp("parallel"),
        name="compress_sample",
    )(page_table.reshape(-1), *([cache5] * n_pages), pe2, w1b, w2b)


def _softmax2(s_a, mask_a, s_b, mask_b):
    s_a = jnp.where(mask_a, s_a, NEG)
    s_b = jnp.where(mask_b, s_b, NEG)
    m = jnp.maximum(jnp.max(s_a, axis=-1, keepdims=True), jnp.max(s_b, axis=-1, keepdims=True))
    e_a = jnp.where(mask_a, jnp.exp(s_a - m), 0.0)
    e_b = jnp.where(mask_b, jnp.exp(s_b - m), 0.0)
    return e_a, e_b, jnp.sum(e_a, axis=-1, keepdims=True) + jnp.sum(e_b, axis=-1, keepdims=True)


def _nsa_sample_kernel(pt_ref, *refs, n_pages, hd, s_new, scale):
    page_refs = refs[:n_pages]
    (q_ref, gate_ref, kcvc_ref, new_ref, nw_ref, win_ref, ov_ref, e_ref, rm_ref, sl_ref, o_ref, wn_ref,
     imp_ref, kv_ref, kvw_ref) = refs[n_pages:]
    page = page_refs[0].shape[0]
    past = n_pages * page
    wb = win_ref.shape[0]
    gw = NSA_KV * hd
    rows = NSA_REP * s_new
    nc = kcvc_ref.shape[1]
    for j, pr in enumerate(page_refs):
        kv_ref[:, j * page:(j + 1) * page, :] = _rows_by_head(pr[...]).astype(BF16)
    kvw_ref[...] = _rows_by_head(win_ref[...]).astype(BF16)
    n_cmp = (past + s_new - L_CMP) // STRIDE_CMP + 1
    n_sel = -(-(past + s_new) // L_SEL)
    tok = lax.broadcasted_iota(jnp.int32, (rows, 1), 0) % s_new
    q_pos = past + tok
    tnew = lax.broadcasted_iota(jnp.int32, (1, SUBLANE), 1)
    dist_new = tok - tnew
    mask_new = (dist_new >= 0) & (tnew < s_new)
    n = lax.broadcasted_iota(jnp.int32, (1, nc), 1)
    dist_c = q_pos - (n * STRIDE_CMP + (L_CMP - 1))
    mask_c = (dist_c >= 0) & (n < n_cmp)
    imp_ref[...] = jnp.zeros(imp_ref.shape, F32)
    o_cs = []
    for g in range(NSA_KV):
        s_c = (_nt_dot(q_ref[g].astype(BF16), kcvc_ref[g].astype(BF16)) * scale
               - sl_ref[g] * dist_c.astype(F32))
        p_c = _masked_softmax(s_c, mask_c)
        o_cs.append(_dot(p_c.astype(BF16), kcvc_ref[NSA_KV + g].astype(BF16)))
        imp_ref[g * rows:(g + 1) * rows, :] = _importance(_dot(rm_ref[...], p_c), ov_ref[...])
    lane_pos = past + lax.broadcasted_iota(jnp.int32, (1, LANE), 1) % s_new
    picked_all = _select_blocks_t(imp_ref[...].T, lane_pos, n_sel).T
    groups = range(NSA_KV)
    dist_s = q_pos - lax.broadcasted_iota(jnp.int32, (1, past), 1)
    dist_w = wb + tok - lax.broadcasted_iota(jnp.int32, (1, wb), 1)
    mask_w = (dist_w >= 0) & (dist_w <= WINDOW)

    def new_rows(kind, g):
        return new_ref[:, kind * gw + g * hd:kind * gw + (g + 1) * hd]

    def scores(q, keys, dist, g):
        return _nt_dot(q, keys) * scale - sl_ref[g] * dist.astype(F32)

    qs = [q_ref[g].astype(BF16) for g in groups]
    picked = [picked_all[g * rows:(g + 1) * rows, :] for g in groups]
    key_masks = [(_dot(picked[g].astype(BF16), e_ref[...]) > 0.5) & (dist_s >= 0) for g in groups]
    cur_masks = [mask_new & (picked[g][:, past // L_SEL:past // L_SEL + 1] > 0.5) for g in groups]
    s_s = [scores(qs[g], kv_ref[g], dist_s, g) for g in groups]
    s_n = [scores(q_ref[g], new_rows(2, g), dist_new, g) for g in groups]
    s_w = [scores(qs[g], kvw_ref[g], dist_w, g) for g in groups]
    s_wn = [scores(q_ref[g], new_rows(4, g), dist_new, g) for g in groups]
    sm_s = [_softmax2(s_s[g], key_masks[g], s_n[g], cur_masks[g]) for g in groups]
    sm_w = [_softmax2(s_w[g], mask_w, s_wn[g], mask_new) for g in groups]
    for g in groups:
        e_s, e_n, l_s = sm_s[g]
        e_w, e_wn, l_w = sm_w[g]
        o_s = (_dot(e_s.astype(BF16), kv_ref[NSA_KV + g]) + _dot(e_n, new_rows(3, g))) / l_s
        o_w = (_dot(e_w.astype(BF16), kvw_ref[NSA_KV + g]) + _dot(e_wn, new_rows(5, g))) / l_w
        gate = jax.nn.sigmoid(gate_ref[g])
        o_ref[g] = (gate[:, 0:1] * o_cs[g] + gate[:, 1:2] * o_s + gate[:, 2:3] * o_w).astype(o_ref.dtype)
    wn_ref[0:wb - s_new] = win_ref[s_new:wb]
    wn_ref[wb - s_new:wb] = nw_ref[...]


def nsa_sample(cache5, page0, page_table, q, gates, kcvc, kv_new, win_new, win, win0, s_new):
    db, n_pages = page_table.shape
    page = cache5.shape[1]
    past = n_pages * page
    hd = q.shape[-1]
    rows = q.shape[2]
    nc = kcvc.shape[2]
    wb = win.shape[1]
    assert wb <= past and wb <= WINDOW and N_SELECT <= -(-(past + s_new) // L_SEL) <= LANE
    assert past % L_SEL == 0 and s_new <= SUBLANE and s_new <= L_SEL and NSA_KV * rows <= LANE
    i = jnp.arange(rows)
    rm = (i[:, None] % s_new == i[None, :] % s_new).astype(F32)
    slopes = jnp.repeat(_slopes(), s_new, axis=1)[:, :, None]
    ov = _overlap_matrix(nc, LANE)
    em = _expand_matrix(LANE, past)
    per_b = lambda a: pl.BlockSpec((None,) + a.shape[1:], lambda b, pt: (b,) + (0,) * (a.ndim - 1))
    full = lambda a: pl.BlockSpec(a.shape, lambda b, pt: (0,) * a.ndim)
    return pl.pallas_call(
        functools.partial(_nsa_sample_kernel, n_pages=n_pages, hd=hd, s_new=s_new, scale=hd ** -0.5),
        out_shape=(jax.ShapeDtypeStruct(q.shape, BF16), jax.ShapeDtypeStruct((db,) + win.shape[1:], F32)),
        grid_spec=pltpu.PrefetchScalarGridSpec(
            num_scalar_prefetch=1,
            grid=(db,),
            in_specs=_page_specs(n_pages, page, hd, 1, page0)
            + [per_b(q), per_b(gates), per_b(kcvc), per_b(kv_new), per_b(win_new),
               pl.BlockSpec((None,) + win.shape[1:], lambda b, pt: (win0 + b, 0, 0, 0)),
               full(ov), full(em), full(rm), full(slopes)],
            out_specs=(per_b(q), per_b(win)),
            scratch_shapes=[pltpu.VMEM((LANE, LANE), F32), pltpu.VMEM((PAIR_ROWS, past, hd), BF16),
                            pltpu.VMEM((PAIR_ROWS, wb, hd), BF16)]),
        compiler_params=_cp("parallel"),
        name="nsa_sample",
    )(page_table.reshape(-1), *([cache5] * n_pages), q, gates, kcvc, kv_new, win_new, win, ov, em, rm, slopes)


def _ffn_up_prompt_kernel(h_ref, halo_ref, wa_ref, wb_ref, cw_ref, cb_ref, u_ref, tail_ref, ext_ref, *, tm):
    wa = wa_ref[...].astype(BF16)
    h = h_ref[...]
    a = _dot(h, wa)
    prev = _dot(halo_ref[...], wa)
    ext_ref[0:SUBLANE, :] = jnp.where(pl.program_id(0) > 0, prev, 0.0)
    ext_ref[SUBLANE:SUBLANE + tm, :] = a
    y = (cb_ref[...] + cw_ref[0:1, :] * ext_ref[pl.ds(SUBLANE - 2, tm), :]
         + cw_ref[1:2, :] * ext_ref[pl.ds(SUBLANE - 1, tm), :] + cw_ref[2:3, :] * a)
    u_ref[...] = (jax.nn.gelu(y) * _dot(h, wb_ref[...].astype(BF16))).astype(u_ref.dtype)
    tail_ref[...] = a[tm - SUBLANE:tm, :]


def ffn_up_prompt(h2, w_up, conv_w, conv_b, tm, tn):
    m, d = h2.shape
    ff = w_up.shape[1] // 2
    nj = ff // tn
    assert CONV_W == 3 and m % tm == 0 and ff % tn == 0
    return pl.pallas_call(
        functools.partial(_ffn_up_prompt_kernel, tm=tm),
        out_shape=(jax.ShapeDtypeStruct((m, ff), BF16), jax.ShapeDtypeStruct((m // tm, SUBLANE, ff), F32)),
        grid=(m // tm, nj),
        in_specs=[pl.BlockSpec((tm, d), lambda i, j: (i, 0)),
                  pl.BlockSpec((SUBLANE, d), lambda i, j: (jnp.maximum(i * (tm // SUBLANE) - 1, 0), 0)),
                  pl.BlockSpec((d, tn), lambda i, j: (0, j)),
                  pl.BlockSpec((d, tn), lambda i, j: (0, nj + j)),
                  pl.BlockSpec((CONV_W, tn), lambda i, j: (0, j)),
                  pl.BlockSpec((1, tn), lambda i, j: (0, j))],
        out_specs=(pl.BlockSpec((tm, tn), lambda i, j: (i, j)),
                   pl.BlockSpec((None, SUBLANE, tn), lambda i, j: (i, 0, j))),
        scratch_shapes=[pltpu.VMEM((tm + SUBLANE, tn), F32)],
        compiler_params=_cp("parallel", "parallel"),
        name="ffn_up_prompt",
    )(h2, h2, w_up, w_up, conv_w, conv_b.reshape(1, ff))


def _ffn_up_sample_kernel(h_ref, prev_ref, wa_ref, wb_ref, cw_ref, cb_ref, u_ref, tail_ref, ext_ref, *, db, m):
    h = h_ref[...]
    a = _dot(h, wa_ref[...].astype(BF16))
    ext_ref[0:2 * db, :] = prev_ref[...]
    ext_ref[2 * db:2 * db + m, :] = a
    y = (cb_ref[...] + cw_ref[0:1, :] * ext_ref[0:m, :] + cw_ref[1:2, :] * ext_ref[db:db + m, :]
         + cw_ref[2:3, :] * a)
    u_ref[...] = (jax.nn.gelu(y) * _dot(h, wb_ref[...].astype(BF16))).astype(u_ref.dtype)
    tail_ref[...] = ext_ref[m:m + 2 * db, :]


def ffn_up_sample(h2, prev, w_up, conv_w, conv_b, db, tn):
    m, d = h2.shape
    ff = w_up.shape[1] // 2
    nj = ff // tn
    assert CONV_W == 3 and m >= 2 * db and db % SUBLANE == 0
    return pl.pallas_call(
        functools.partial(_ffn_up_sample_kernel, db=db, m=m),
        out_shape=(jax.ShapeDtypeStruct((m, ff), BF16), jax.ShapeDtypeStruct((2 * db, ff), F32)),
        grid=(nj,),
        in_specs=[pl.BlockSpec((m, d), lambda j: (0, 0)),
                  pl.BlockSpec((2 * db, tn), lambda j: (0, j)),
                  pl.BlockSpec((d, tn), lambda j: (0, j)),
                  pl.BlockSpec((d, tn), lambda j: (0, nj + j)),
                  pl.BlockSpec((CONV_W, tn), lambda j: (0, j)),
                  pl.BlockSpec((1, tn), lambda j: (0, j))],
        out_specs=(pl.BlockSpec((m, tn), lambda j: (0, j)), pl.BlockSpec((2 * db, tn), lambda j: (0, j))),
        scratch_shapes=[pltpu.VMEM((m + 2 * db, tn), F32)],
        compiler_params=_cp("parallel"),
        name="ffn_up_sample",
    )(h2, prev, w_up, w_up, conv_w, conv_b.reshape(1, ff))


TN_IN = 512
TN_OUT = 512
TN_FF = 256
TM_DOWN = 512


def _post(x, mix, mods, tm_row, tm_mm, w_out_l, g_post_mix, g_pre_ffn, ffn_up, w_down_b, g_post_ffn):
    f = matmul(mix, w_out_l, tm_mm, TN_OUT)
    x2, h2 = post_mix(f, x, g_post_mix, g_pre_ffn, mods, tm_row)
    u, tail = ffn_up(h2)
    f2 = matmul(u, w_down_b, min(TM_DOWN, tm_mm), TM_DOWN)
    return post_ffn(f2, x2, g_post_ffn, mods, tm_row), tail


def kernel(x_prompt, x_sample, cache_kv, cache_win, state_ret, state_conv, page_table, c_prompt, c_sample,
           w_ada, b_ada, g_pre_mix, w_in, cmp_pe, cmp_w1, cmp_w2, ret_gn, w_out, g_post_mix, g_pre_ffn,
           w_up, conv_w, conv_b, w_down, g_post_ffn):
    depth = w_in.shape[0]
    bp, t, d = x_prompt.shape
    db, s, _ = x_sample.shape
    assert bp == 1
    h, dkv = state_ret.shape[2], state_ret.shape[3]
    n_phys, page, n_rows, g, hd = cache_kv.shape[1:]
    assert h == RET_HEADS and g == NSA_KV and n_rows == 4
    wb = cache_win.shape[2]
    ff = w_up.shape[2] // 2
    rw = h * dkv
    q_col = 4 * rw
    kv_col = q_col + NSA_HEADS * hd
    ng_col = kv_col + 6 * g * hd
    n_gate = NSA_HEADS * 3
    np_ = ng_col
    assert np_ % TN_IN == 0 and ff % TN_FF == 0 and n_gate <= LANE
    gw = g * hd
    tm_p = min(1024, t)
    tm_s = s * db

    yp = x_prompt.reshape(t, d)
    ys = jnp.swapaxes(x_sample, 0, 1).reshape(s * db, d)
    c_all = jnp.concatenate([c_prompt, jnp.zeros((SUBLANE - 1, d), F32), c_sample], axis=0)
    outs = [[] for _ in range(8)]
    for l in range(depth):
        w_in_t = jnp.swapaxes(w_in[l], 0, 1)
        w_gate_t = jnp.pad(w_in_t[np_:], ((0, LANE - n_gate), (0, 0)))
        w_out_l = w_out[l]
        w_up_l = w_up[l]
        w_down_b = w_down[l].astype(BF16)
        conv_w_p = conv_w[l]
        conv_b_p = conv_b[l]
        pe2 = cmp_pe[l].reshape(2, L_CMP // 2, 2 * hd)
        w1b = cmp_w1[l].reshape(2, L_CMP // 2, 2 * hd, -1).astype(BF16)
        w2b = cmp_w2[l].astype(BF16)
        post_w = (w_out_l, g_post_mix[l], g_pre_ffn[l])

        mods = ada_mods(c_all, w_ada[l], b_ada[l])
        mods_p = mods[0:SUBLANE]
        mods_s = mods[SUBLANE:SUBLANE + db]

        hp = norm_mod(yp, g_pre_mix[l], mods_p, 0, 1, 256)
        p = matmul(hp, w_in_t, tm_p, TN_IN, np_, w_rows_are_outputs=True)
        mix_ret, ret_state = retention_prompt(p, ret_gn[l], dkv)
        kcvc = compress_prompt(p, kv_col, pe2, w1b, w2b, hd)
        p_gate = matmul(hp, w_gate_t, tm_p, LANE, w_rows_are_outputs=True)
        gates = jnp.swapaxes(p_gate[:, :n_gate].reshape(t, g, 3 * NSA_REP), 0, 1)
        part, unsel = cmp_win_select_prompt(p, q_col, kv_col, kcvc, gates)
        nsa_o = slc_prompt(p, q_col, kv_col, gates, part, unsel)
        mix = jnp.concatenate([mix_ret, nsa_o], axis=1)
        yp_new, tails = _post(yp, mix, mods_p, 256, tm_p, *post_w,
                              lambda h2: ffn_up_prompt(h2, w_up_l, conv_w_p, conv_b_p, tm_p, TN_FF),
                              w_down_b, g_post_ffn[l])
        wp = min(WINDOW, t)
        outs[0].append(p[:, kv_col:kv_col + 4 * gw].reshape(1, t, 4, g, hd))
        outs[1].append(p[t - wp:, kv_col + 4 * gw:kv_col + 6 * gw].reshape(1, wp, 2, g, hd))
        outs[2].append(ret_state[None])
        outs[3].append(tails[-1, SUBLANE - (CONV_W - 1):, :ff][None])
        yp = yp_new

        hs = norm_mod(ys, g_pre_mix[l], mods_s, 0, 1, db)
        ps_t = matmul(hs, w_in_t, tm_s, TN_IN, np_, w_rows_are_outputs=True)
        ps = jnp.swapaxes(ps_t.reshape(s, db, np_), 0, 1).reshape(db * s, np_)
        mix_ret_s, ret_s = retention_sample(ps, state_ret.reshape(depth * db, h, dkv, dkv), l * db, ret_gn[l], s)
        cache5 = cache_kv.reshape(depth * n_phys, page, 2, PAIR_ROWS, hd)
        kcvc_s = compress_sample(cache5, l * n_phys, page_table, pe2, w1b, w2b, hd)

        def head_rows(a, last):
            a = a.reshape(db, s, g, NSA_REP, last)
            return a.transpose(0, 2, 3, 1, 4).reshape(db, g, NSA_REP * s, last)

        q_s = head_rows(ps[:, q_col:kv_col], hd)
        ps_gate = matmul(hs, w_gate_t, tm_s, LANE, w_rows_are_outputs=True)
        ps_gate = jnp.swapaxes(ps_gate.reshape(s, db, LANE), 0, 1).reshape(db * s, LANE)
        gates_s = head_rows(ps_gate[:, :n_gate], 3)
        kv_new = jnp.pad(ps[:, kv_col:ng_col].reshape(db, s, 6 * gw), ((0, 0), (0, SUBLANE - s), (0, 0)))
        win_rows = ps[:, kv_col + 4 * gw:ng_col].reshape(db, s, PAIR_ROWS, hd)
        o_s, win_new = nsa_sample(cache5, l * n_phys, page_table, q_s, gates_s, kcvc_s, kv_new, win_rows,
                                  cache_win.reshape(depth * db, wb, PAIR_ROWS, hd), l * db, s)
        nsa_o_s = o_s.reshape(db, g, NSA_REP, s, hd).transpose(0, 3, 1, 2, 4).reshape(db * s, NSA_HEADS * hd)
        mix_s = jnp.concatenate([mix_ret_s, nsa_o_s], axis=1)
        mix_s = jnp.swapaxes(mix_s.reshape(db, s, -1), 0, 1).reshape(s * db, -1)
        prev = jnp.swapaxes(state_conv[l], 0, 1).reshape((CONV_W - 1) * db, ff)
        ys_new, tail_s = _post(ys, mix_s, mods_s, db, tm_s, *post_w,
                               lambda h2: ffn_up_sample(h2, prev, w_up_l, conv_w_p, conv_b_p, db, TN_FF),
                               w_down_b, g_post_ffn[l])
        outs[4].append(ps[:, kv_col:kv_col + 4 * gw].reshape(db, s, 4, g, hd))
        outs[5].append(win_new.reshape(db, wb, 2, g, hd))
        outs[6].append(ret_s)
        outs[7].append(jnp.swapaxes(tail_s[:, :ff].reshape(CONV_W - 1, db, ff), 0, 1))
        ys = ys_new

    y_prompt = yp.reshape(1, t, d)
    y_sample = jnp.swapaxes(ys.reshape(s, db, d), 0, 1)
    return (y_prompt, y_sample) + tuple(jnp.stack(o) for o in outs)
```
